```python
import math
import jax, jax.numpy as jnp
from jax import lax
import numpy as np

D_MODEL = 1024
BATCH = 8
SEQ = 4096
DEPTH = 1

HEAD_DIM = 64
N_Q_HEADS = 16
N_KV_HEADS = 4
GROUP = N_Q_HEADS // N_KV_HEADS
ATTN_WIDTH = N_Q_HEADS * HEAD_DIM
KV_WIDTH = N_KV_HEADS * HEAD_DIM
WINDOW = 128
BLOCK = 128
ROT_DIM = HEAD_DIM // 4
ROPE_THETA = 500000.0
NEG_INF = -1e30
LRU_WIDTH = 1024
LRU_BLOCKS = 16
LRU_BLOCK_W = LRU_WIDTH // LRU_BLOCKS
CONV_W = 4
LRU_C = 8.0
MIX_WIDTH = ATTN_WIDTH + LRU_WIDTH
IN_DIM = ATTN_WIDTH + 2 * KV_WIDTH + ATTN_WIDTH + 2 * LRU_WIDTH
SPLITS = tuple(np.cumsum([ATTN_WIDTH, KV_WIDTH, KV_WIDTH, ATTN_WIDTH, LRU_WIDTH]).tolist())
EPS = 1e-6

kernel_name = "hymba_swa_sink_rglru_hybrid"


def rmsnorm(x, g):
    xf = x.astype(jnp.float32)
    y = xf * lax.rsqrt(jnp.mean(xf * xf, axis=-1, keepdims=True) + EPS)
    return (y * g.astype(jnp.float32)).astype(x.dtype)


def partial_rope(t, cos, sin):
    half = ROT_DIM // 2
    t1 = t[..., :half].astype(jnp.float32)
    t2 = t[..., half:ROT_DIM].astype(jnp.float32)
    rot = jnp.concatenate([t1 * cos - t2 * sin, t2 * cos + t1 * sin], axis=-1)
    return jnp.concatenate([rot.astype(t.dtype), t[..., ROT_DIM:]], axis=-1)


def swa_sink_attention(q, k, v, sinks):
    b, s = q.shape[0], q.shape[1]
    nb = s // BLOCK
    qb = q.reshape(b, nb, BLOCK, N_KV_HEADS, GROUP, HEAD_DIM)

    def with_prev(t):
        tb = t.reshape(b, nb, BLOCK, N_KV_HEADS, HEAD_DIM)
        prev = jnp.pad(tb, ((0, 0), (1, 0), (0, 0), (0, 0), (0, 0)))[:, :-1]
        return jnp.concatenate([prev, tb], axis=2)

    kc, vc = with_prev(k), with_prev(v)
    scores = jnp.einsum('bnqhgd,bnkhd->bnhgqk', qb, kc,
                        preferred_element_type=jnp.float32) * (HEAD_DIM ** -0.5)
    qi = jnp.arange(BLOCK)[:, None]
    kj = jnp.arange(2 * BLOCK)[None, :]
    diff = qi + BLOCK - kj
    band = (diff >= 0) & (diff < WINDOW)
    blk = jnp.arange(nb)[:, None, None]
    valid = band[None] & ((blk > 0) | (kj >= BLOCK)[None])
    scores = jnp.where(valid[None, :, None, None], scores, NEG_INF)
    sink = sinks.astype(jnp.float32).reshape(N_KV_HEADS, GROUP)[None, None, :, :, None, None]
    m = jnp.maximum(jnp.max(scores, axis=-1, keepdims=True), sink)
    p = jnp.exp(scores - m)
    denom = jnp.sum(p, axis=-1, keepdims=True) + jnp.exp(sink - m)
    o = jnp.einsum('bnhgqk,bnkhd->bnqhgd', p / denom, vc.astype(jnp.float32))
    return o.reshape(b, s, ATTN_WIDTH).astype(q.dtype)


def causal_depthwise_conv(x, w, bias):
    s = x.shape[1]
    xp = jnp.pad(x, ((0, 0), (CONV_W - 1, 0), (0, 0)))
    y = bias[None, None, :]
    for tap in range(CONV_W):
        y = y + xp[:, tap:tap + s] * w[tap][None, None, :]
    return y


def rg_lru(x, w_r, b_r, w_i, b_i, lam):
    b, s, _ = x.shape
    xf = x.astype(jnp.float32)
    xb = xf.reshape(b, s, LRU_BLOCKS, LRU_BLOCK_W)
    r = jax.nn.sigmoid(jnp.einsum('bsnc,ncd->bsnd', xb, w_r.astype(jnp.float32)).reshape(b, s, LRU_WIDTH)
                       + b_r.astype(jnp.float32))
    i = jax.nn.sigmoid(jnp.einsum('bsnc,ncd->bsnd', xb, w_i.astype(jnp.float32)).reshape(b, s, LRU_WIDTH)
                       + b_i.astype(jnp.float32))
    log_a = -LRU_C * r * jax.nn.softplus(-lam.astype(jnp.float32))
    a = jnp.exp(log_a)
    u = jnp.sqrt(-jnp.expm1(2.0 * log_a)) * (i * xf)

    def combine(left, right):
        a1, b1 = left
        a2, b2 = right
        return a1 * a2, a2 * b1 + b2

    _, h = lax.associative_scan(combine, (a, u), axis=1)
    return h.astype(x.dtype)


def setup_inputs(seed: int = 0) -> dict:
    key = jax.random.key(seed)
    ks = jax.random.split(key, 16)
    f32 = jnp.float32
    x = jax.random.normal(ks[0], (BATCH, SEQ, D_MODEL), f32)
    ln_gain = 1.0 + 0.02 * jax.random.normal(ks[1], (DEPTH, D_MODEL), f32)
    w_in = jax.random.normal(ks[2], (DEPTH, D_MODEL, IN_DIM), f32) * D_MODEL ** -0.5
    sinks = 0.5 * jax.random.normal(ks[3], (DEPTH, N_Q_HEADS), f32)
    conv_w = jax.random.normal(ks[4], (DEPTH, CONV_W, LRU_WIDTH), f32) * CONV_W ** -0.5
    conv_b = 0.02 * jax.random.normal(ks[5], (DEPTH, LRU_WIDTH), f32)
    w_rgate = jax.random.normal(ks[6], (DEPTH, LRU_BLOCKS, LRU_BLOCK_W, LRU_BLOCK_W), f32) * LRU_BLOCK_W ** -0.5
    b_rgate = 0.02 * jax.random.normal(ks[7], (DEPTH, LRU_WIDTH), f32)
    w_igate = jax.random.normal(ks[8], (DEPTH, LRU_BLOCKS, LRU_BLOCK_W, LRU_BLOCK_W), f32) * LRU_BLOCK_W ** -0.5
    b_igate = 0.02 * jax.random.normal(ks[9], (DEPTH, LRU_WIDTH), f32)
    a0 = jax.random.uniform(ks[10], (DEPTH, LRU_WIDTH), f32, 0.9, 0.999)
    sig = a0 ** (1.0 / LRU_C)
    lru_lambda = jnp.log(sig) - jnp.log1p(-sig)
    attn_out_gain = 1.0 + 0.02 * jax.random.normal(ks[11], (DEPTH, ATTN_WIDTH), f32)
    lru_out_gain = 1.0 + 0.02 * jax.random.normal(ks[12], (DEPTH, LRU_WIDTH), f32)
    w_out = jax.random.normal(ks[13], (DEPTH, MIX_WIDTH, D_MODEL), f32) * MIX_WIDTH ** -0.5
    final_gain = 1.0 + 0.02 * jax.random.normal(ks[14], (D_MODEL,), f32)
    return {"x": x, "ln_gain": ln_gain, "w_in": w_in, "sinks": sinks,
            "conv_w": conv_w, "conv_b": conv_b, "w_rgate": w_rgate, "b_rgate": b_rgate,
            "w_igate": w_igate, "b_igate": b_igate, "lru_lambda": lru_lambda,
            "attn_out_gain": attn_out_gain, "lru_out_gain": lru_out_gain,
            "w_out": w_out, "final_gain": final_gain}


def reference(x, ln_gain, w_in, sinks, conv_w, conv_b, w_rgate, b_rgate, w_igate, b_igate,
              lru_lambda, attn_out_gain, lru_out_gain, w_out, final_gain):
    b, s, _ = x.shape
    pos = jnp.arange(s, dtype=jnp.float32)
    inv_freq = ROPE_THETA ** (-jnp.arange(0, ROT_DIM, 2, dtype=jnp.float32) / ROT_DIM)
    ang = pos[:, None] * inv_freq[None, :]
    cos = jnp.cos(ang)[None, :, None, :]
    sin = jnp.sin(ang)[None, :, None, :]
    for l in range(DEPTH):
        h = rmsnorm(x, ln_gain[l])
        z = jnp.einsum('bsd,de->bse', h, w_in[l])
        q, k, v, g_attn, x_lru, g_lru = jnp.split(z, SPLITS, axis=-1)
        q = partial_rope(q.reshape(b, s, N_Q_HEADS, HEAD_DIM), cos, sin)
        k = partial_rope(k.reshape(b, s, N_KV_HEADS, HEAD_DIM), cos, sin)
        v = v.reshape(b, s, N_KV_HEADS, HEAD_DIM)
        y_attn = swa_sink_attention(q, k, v, sinks[l])
        y_attn = rmsnorm(y_attn * jax.nn.silu(g_attn), attn_out_gain[l])
        u = causal_depthwise_conv(x_lru, conv_w[l], conv_b[l])
        y_lru = rg_lru(u, w_rgate[l], b_rgate[l], w_igate[l], b_igate[l], lru_lambda[l])
        y_lru = rmsnorm(y_lru * jax.nn.silu(g_lru), lru_out_gain[l])
        y = jnp.einsum('bse,ed->bsd', jnp.concatenate([y_attn, y_lru], axis=-1), w_out[l])
        x = x + y
    return rmsnorm(x, final_gain)
```

```python
import functools

import numpy as np
import jax
import jax.numpy as jnp
from jax import lax
from jax.experimental import pallas as pl
from jax.experimental.pallas import tpu as pltpu

D_MODEL = 1024
HEAD_DIM = 64
N_Q_HEADS = 16
N_KV_HEADS = 4
GROUP = N_Q_HEADS // N_KV_HEADS
ATTN_WIDTH = N_Q_HEADS * HEAD_DIM
KV_WIDTH = N_KV_HEADS * HEAD_DIM
WINDOW = 128
ROT_DIM = HEAD_DIM // 4
ROPE_THETA = 500000.0
NEG_INF = -1e30
LRU_WIDTH = 1024
LRU_BLOCKS = 16
LRU_BLOCK_W = LRU_WIDTH // LRU_BLOCKS
CONV_W = 4
LRU_C = 8.0
IN_DIM = 2 * ATTN_WIDTH + 2 * KV_WIDTH + 2 * LRU_WIDTH
EPS = 1e-6

LANES = 128
MXU_DIM = 256
VMEM_LIMIT_BYTES = 58 * 1024 * 1024

TS = 64
KEYS = WINDOW + TS
KEY_ROWS = 256
PITCH = TS + 8
N_SLAB = LRU_WIDTH // LANES
N_GRP = LRU_WIDTH // MXU_DIM

OFF_Q, OFF_KV, OFF_GA = 0, ATTN_WIDTH, ATTN_WIDTH + 2 * KV_WIDTH
OFF_XL, OFF_GL = OFF_GA + ATTN_WIDTH, OFF_GA + ATTN_WIDTH + LRU_WIDTH

V_LN, V_CB, V_BR, V_BI, V_LAM, V_AG, V_LG, V_FG, V_CW = 0, 1, 2, 3, 4, 5, 6, 7, 8
V_ROWS = 16


def _sigmoid(x):
    return 0.5 * jnp.tanh(0.5 * x) + 0.5


def _rms_scale(v):
    return lax.rsqrt(jnp.mean(v * v, axis=-1, keepdims=True) + EPS)


def _layer_kernel(sinks_ref, x_ref, rope_ref, vecs_ref, win_ref, wg_ref, wout_ref, o_ref,
                  h_scr, q_scr, kv_scr, ga_scr, xl_scr, gl_scr, xc_scr, a_scr, u_scr,
                  hs_scr, ycat_scr, cw_scr, hst_scr, *, batch):
    f32, bf16 = jnp.float32, jnp.bfloat16
    rows = batch * TS
    step = pl.program_id(0)

    @pl.when(step == 0)
    def _init_carries():
        kv_scr[...] = jnp.zeros_like(kv_scr)
        cw_scr[...] = jnp.zeros_like(cw_scr)
        hst_scr[...] = jnp.zeros_like(hst_scr)

    def vec(row):
        return vecs_ref[row:row + 1, :]

    def norm_b(b, c):
        xb = x_ref[b]
        hb = xb * _rms_scale(xb) * vec(V_LN)
        h_scr[pl.ds(pl.multiple_of(b * TS, TS), TS), :] = hb.astype(bf16)
        return c
    lax.fori_loop(0, batch, norm_b, 0)

    hmat = h_scr[...]

    def proj(c0, c1):
        return jnp.dot(hmat, win_ref[:, c0:c1], preferred_element_type=f32)

    def rope(z, tbl):
        cos_t, sin_hi, sin_lo = rope_ref[tbl], rope_ref[tbl + 1], rope_ref[tbl + 2]
        return (z * cos_t[None] + pltpu.roll(z, 8, 2) * sin_hi[None]
                + pltpu.roll(z, LANES - 8, 2) * sin_lo[None])

    for g in range(GROUP):
        zq = proj(OFF_Q + g * MXU_DIM, OFF_Q + (g + 1) * MXU_DIM).reshape(batch, TS, MXU_DIM)
        for sl in range(MXU_DIM // LANES):
            r = rope(zq[:, :, sl * LANES:(sl + 1) * LANES], 0)
            c0 = g * MXU_DIM + sl * LANES
            q_scr[:, c0:c0 + LANES] = r.reshape(rows, LANES).astype(bf16)

    zkv = proj(OFF_KV, OFF_KV + 2 * KV_WIDTH).reshape(batch, TS, 2 * KV_WIDTH)
    for sl in range(KV_WIDTH // LANES):
        r = rope(zkv[:, :, sl * LANES:(sl + 1) * LANES], 3)
        kv_scr[:, WINDOW:KEYS, sl * LANES:(sl + 1) * LANES] = r.astype(bf16)
    kv_scr[:, WINDOW:KEYS, KV_WIDTH:] = zkv[:, :, KV_WIDTH:].astype(bf16)

    half = ATTN_WIDTH // 2
    for c in range(2):
        ga_scr[:, c * half:(c + 1) * half] = proj(OFF_GA + c * half, OFF_GA + (c + 1) * half)
        gl_scr[:, c * half:(c + 1) * half] = proj(OFF_GL + c * half, OFF_GL + (c + 1) * half)
        zx = proj(OFF_XL + c * half, OFF_XL + (c + 1) * half)
        for sl in range(half // LANES):
            for b in range(batch):
                xl_scr[c * (half // LANES) + sl, b * PITCH:b * PITCH + TS, :] = (
                    zx[b * TS:(b + 1) * TS, sl * LANES:(sl + 1) * LANES])

    qi = lax.broadcasted_iota(jnp.int32, (TS, KEY_ROWS), 0)
    kj = lax.broadcasted_iota(jnp.int32, (TS, KEY_ROWS), 1)
    valid = (kj > qi) & (kj <= qi + WINDOW) & (kj >= WINDOW - step * TS)
    bias = jnp.where(valid, 0.0, NEG_INF).astype(f32)
    lane_head = lax.broadcasted_iota(jnp.int32, (1, KV_WIDTH), 1) // HEAD_DIM

    def attn_b(b, c):
        r0 = pl.multiple_of(b * TS, TS)
        k_all = kv_scr[b, :, 0:KV_WIDTH]
        v_all = kv_scr[b, :, KV_WIDTH:]
        q4 = jnp.concatenate(
            [q_scr[pl.ds(r0, TS), g * MXU_DIM:(g + 1) * MXU_DIM] for g in range(GROUP)], axis=0)
        k_bd = jnp.concatenate(
            [jnp.where(lane_head == k, k_all, jnp.zeros_like(k_all)) for k in range(N_KV_HEADS)],
            axis=0)
        sc = lax.dot_general(q4, k_bd, (((1,), (1,)), ((), ())), preferred_element_type=f32)
        o_acc = jnp.zeros((GROUP * TS, KV_WIDTH), f32)
        for k in range(N_KV_HEADS):
            probs = []
            for g in range(GROUP):
                blk = sc[g * TS:(g + 1) * TS, k * KEY_ROWS:(k + 1) * KEY_ROWS] + bias
                sink = sinks_ref[k * GROUP + g]
                m = jnp.maximum(jnp.max(blk, axis=1, keepdims=True), sink)
                p = jnp.exp(blk - m)
                den = jnp.sum(p, axis=1, keepdims=True) + jnp.exp(sink - m)
                probs.append((p * (1.0 / den)).astype(bf16))
            p_k = jnp.concatenate(probs, axis=0)
            v_k = jnp.where(lane_head == k, v_all, jnp.zeros_like(v_all))
            o_acc = o_acc + jnp.dot(p_k, v_k, preferred_element_type=f32)
        ob = jnp.concatenate([o_acc[g * TS:(g + 1) * TS] for g in range(GROUP)], axis=1)
        ga = ga_scr[pl.ds(r0, TS), :]
        v = ob * (ga * _sigmoid(ga))
        y = v * _rms_scale(v) * vec(V_AG)
        ycat_scr[pl.ds(r0, TS), 0:ATTN_WIDTH] = y.astype(bf16)
        return c
    lax.fori_loop(0, batch, attn_b, 0)

    kv_scr[:, 0:WINDOW, :] = kv_scr[:, TS:KEYS, :]

    def conv_t(t, carry):
        x1, x2, x3 = carry
        xt = jnp.concatenate(
            [xl_scr[c, pl.ds(t, batch, stride=PITCH), :] for c in range(N_SLAB)], axis=1)
        u = (vec(V_CB) + vec(V_CW + 3) * xt + vec(V_CW + 2) * x1
             + vec(V_CW + 1) * x2 + vec(V_CW) * x3)
        t8 = pl.multiple_of(t * batch, batch)
        for j in range(N_GRP):
            xc_scr[j, pl.ds(t8, batch), :] = u[:, j * MXU_DIM:(j + 1) * MXU_DIM]
        return xt, x1, x2
    tail = lax.fori_loop(0, TS, conv_t, (cw_scr[0], cw_scr[1], cw_scr[2]), unroll=4)
    for i in range(CONV_W - 1):
        cw_scr[i] = tail[i]

    lam = vec(V_LAM)
    softplus_neg_lam = jnp.maximum(-lam, 0.0) + jnp.log1p(jnp.exp(-jnp.abs(lam)))
    log_a_scale = -LRU_C * softplus_neg_lam
    for j in range(N_GRP):
        cs = slice(j * MXU_DIM, (j + 1) * MXU_DIM)
        xg = xc_scr[j]
        pre = jnp.dot(xg.astype(bf16), wg_ref[j], preferred_element_type=f32)
        r_gate = _sigmoid(pre[:, :MXU_DIM] + vec(V_BR)[:, cs])
        i_gate = _sigmoid(pre[:, MXU_DIM:] + vec(V_BI)[:, cs])
        log_a = r_gate * log_a_scale[:, cs]
        a_scr[j] = jnp.exp(log_a)
        th = jnp.tanh(log_a)
        u_scr[j] = jnp.sqrt(-2.0 * th / (1.0 - th)) * (i_gate * xg)

    def scan_t(t, h):
        t8 = pl.multiple_of(t * batch, batch)
        at = jnp.concatenate([a_scr[j, pl.ds(t8, batch), :] for j in range(N_GRP)], axis=1)
        ut = jnp.concatenate([u_scr[j, pl.ds(t8, batch), :] for j in range(N_GRP)], axis=1)
        h = at * h + ut
        for c in range(N_SLAB):
            hs_scr[c, pl.ds(t, batch, stride=PITCH), :] = h[:, c * LANES:(c + 1) * LANES]
        return h
    hst_scr[...] = lax.fori_loop(0, TS, scan_t, hst_scr[...], unroll=8)

    def lru_out_b(b, c):
        r0 = pl.multiple_of(b * TS, TS)
        p0 = pl.multiple_of(b * PITCH, 8)
        hb = jnp.concatenate([hs_scr[c2, pl.ds(p0, TS), :] for c2 in range(N_SLAB)], axis=1)
        gl = gl_scr[pl.ds(r0, TS), :]
        v = hb * (gl * _sigmoid(gl))
        y = v * _rms_scale(v) * vec(V_LG)
        ycat_scr[pl.ds(r0, TS), ATTN_WIDTH:] = y.astype(bf16)
        return c
    lax.fori_loop(0, batch, lru_out_b, 0)

    ga_scr[...] = jnp.dot(ycat_scr[...], wout_ref[...], preferred_element_type=f32)

    def out_b(b, c):
        x2 = x_ref[b] + ga_scr[pl.ds(pl.multiple_of(b * TS, TS), TS), :]
        o_ref[b] = x2 * _rms_scale(x2) * vec(V_FG)
        return c
    lax.fori_loop(0, batch, out_b, 0)


def _rope_tables(seq):
    pos = jnp.arange(seq, dtype=jnp.float32)
    inv_freq = ROPE_THETA ** (-jnp.arange(0, ROT_DIM, 2, dtype=jnp.float32) / ROT_DIM)
    ang = pos[:, None] * inv_freq[None, :]
    cos, sin = jnp.cos(ang), jnp.sin(ang)
    half = ROT_DIM // 2
    d = np.arange(LANES) % HEAD_DIM
    fidx = d % half
    lo = (d < half)[None, :]
    hi = ((d >= half) & (d < ROT_DIM))[None, :]
    cos_t = jnp.where(lo | hi, cos[:, fidx], 1.0)
    sin_hi = jnp.where(hi, sin[:, fidx], 0.0)
    sin_lo = jnp.where(lo, -sin[:, fidx], 0.0)
    k_tbl = jnp.stack([cos_t, sin_hi, sin_lo])
    return jnp.concatenate([k_tbl * (HEAD_DIM ** -0.5), k_tbl], axis=0)


def _q_perm():
    g, k, d = np.meshgrid(np.arange(GROUP), np.arange(N_KV_HEADS), np.arange(HEAD_DIM), indexing="ij")
    return ((k * GROUP + g) * HEAD_DIM + d).reshape(-1)


def kernel(x, ln_gain, w_in, sinks, conv_w, conv_b, w_rgate, b_rgate, w_igate, b_igate,
           lru_lambda, attn_out_gain, lru_out_gain, w_out, final_gain):
    batch, seq, d_model = x.shape
    assert d_model == D_MODEL and seq % TS == 0 and ln_gain.shape[0] == 1
    f32, bf16 = jnp.float32, jnp.bfloat16
    rows = batch * TS

    perm = _q_perm()
    w = w_in[0]
    w_q, w_kv = w[:, :ATTN_WIDTH], w[:, ATTN_WIDTH:ATTN_WIDTH + 2 * KV_WIDTH]
    w_ga = w[:, ATTN_WIDTH + 2 * KV_WIDTH:2 * ATTN_WIDTH + 2 * KV_WIDTH]
    w_rest = w[:, 2 * ATTN_WIDTH + 2 * KV_WIDTH:]
    win = jnp.concatenate([w_q[:, perm], w_kv, w_ga[:, perm], w_rest], axis=1).astype(bf16)
    wout = jnp.concatenate([w_out[0][:ATTN_WIDTH][perm], w_out[0][ATTN_WIDTH:]], axis=0).astype(bf16)

    def block_diag(wb):
        per = MXU_DIM // LRU_BLOCK_W
        wb = wb.reshape(N_GRP, per, LRU_BLOCK_W, LRU_BLOCK_W)
        eye = jnp.eye(per, dtype=wb.dtype)
        return jnp.einsum("jpcd,pq->jpcqd", wb, eye).reshape(N_GRP, MXU_DIM, MXU_DIM)
    wg = jnp.concatenate([block_diag(w_rgate[0]), block_diag(w_igate[0])], axis=2).astype(bf16)

    vecs = jnp.zeros((V_ROWS, D_MODEL), f32)
    vecs = vecs.at[V_LN].set(ln_gain[0]).at[V_CB].set(conv_b[0]).at[V_BR].set(b_rgate[0])
    vecs = vecs.at[V_BI].set(b_igate[0]).at[V_LAM].set(lru_lambda[0])
    vecs = vecs.at[V_AG].set(attn_out_gain[0][perm]).at[V_LG].set(lru_out_gain[0])
    vecs = vecs.at[V_FG].set(final_gain).at[V_CW:V_CW + CONV_W].set(conv_w[0])

    rope_tbl = _rope_tables(seq)

    const = lambda *_: (0, 0)
    single = dict(pipeline_mode=pl.Buffered(1))
    grid_spec = pl.GridSpec(
        grid=(seq // TS,),
        in_specs=[
            pl.BlockSpec(memory_space=pltpu.SMEM),
            pl.BlockSpec((batch, TS, D_MODEL), lambda s: (0, s, 0)),
            pl.BlockSpec((6, TS, LANES), lambda s: (0, s, 0)),
            pl.BlockSpec((V_ROWS, D_MODEL), const, **single),
            pl.BlockSpec((D_MODEL, IN_DIM), const, **single),
            pl.BlockSpec((N_GRP, MXU_DIM, 2 * MXU_DIM), lambda s: (0, 0, 0), **single),
            pl.BlockSpec((2 * ATTN_WIDTH, D_MODEL), const, **single),
        ],
        out_specs=pl.BlockSpec((batch, TS, D_MODEL), lambda s: (0, s, 0)),
        scratch_shapes=[
            pltpu.VMEM((rows, D_MODEL), bf16),
            pltpu.VMEM((rows, ATTN_WIDTH), bf16),
            pltpu.VMEM((batch, KEY_ROWS, 2 * KV_WIDTH), bf16),
            pltpu.VMEM((rows, ATTN_WIDTH), f32),
            pltpu.VMEM((N_SLAB, batch * PITCH, LANES), f32),
            pltpu.VMEM((rows, LRU_WIDTH), f32),
            pltpu.VMEM((N_GRP, rows, MXU_DIM), f32),
            pltpu.VMEM((N_GRP, rows, MXU_DIM), f32),
            pltpu.VMEM((N_GRP, rows, MXU_DIM), f32),
            pltpu.VMEM((N_SLAB, batch * PITCH, LANES), f32),
            pltpu.VMEM((rows, 2 * ATTN_WIDTH), bf16),
            pltpu.VMEM((CONV_W - 1, batch, LRU_WIDTH), f32),
            pltpu.VMEM((batch, LRU_WIDTH), f32),
        ],
    )
    return pl.pallas_call(
        functools.partial(_layer_kernel, batch=batch),
        grid_spec=grid_spec,
        out_shape=jax.ShapeDtypeStruct(x.shape, x.dtype),
        compiler_params=pltpu.CompilerParams(
            dimension_semantics=("arbitrary",), vmem_limit_bytes=VMEM_LIMIT_BYTES),
        name="hymba_layer",
    )(sinks[0], x, rope_tbl, vecs, win, wg, wout)
```

```python
import functools

import numpy as np
import jax
import jax.numpy as jnp
from jax import lax
from jax.experimental import pallas as pl
from jax.experimental.pallas import tpu as pltpu

D_MODEL = 1024
HEAD_DIM = 64
N_Q_HEADS = 16
N_KV_HEADS = 4
GROUP = N_Q_HEADS // N_KV_HEADS
ATTN_WIDTH = N_Q_HEADS * HEAD_DIM
KV_WIDTH = N_KV_HEADS * HEAD_DIM
WINDOW = 128
ROT_DIM = HEAD_DIM // 4
ROPE_THETA = 500000.0
NEG_INF = -1e30
LRU_WIDTH = 1024
LRU_BLOCKS = 16
LRU_BLOCK_W = LRU_WIDTH // LRU_BLOCKS
CONV_W = 4
LRU_C = 8.0
IN_DIM = 2 * ATTN_WIDTH + 2 * KV_WIDTH + 2 * LRU_WIDTH
EPS = 1e-6

LANES = 128
MXU_DIM = 256
VMEM_LIMIT_BYTES = 58 * 1024 * 1024

TS = 64
N_SLOT = WINDOW // TS + 1
KEYS = N_SLOT * TS
LOG2E = 1.4426950408889634
PITCH = TS + 8
N_SLAB = LRU_WIDTH // LANES
N_GRP = LRU_WIDTH // MXU_DIM

OFF_Q, OFF_KV, OFF_GA = 0, ATTN_WIDTH, ATTN_WIDTH + 2 * KV_WIDTH
OFF_XL, OFF_GL = OFF_GA + ATTN_WIDTH, OFF_GA + ATTN_WIDTH + LRU_WIDTH

V_LN, V_CB, V_BR, V_BI, V_LAM, V_AG, V_LG, V_FG, V_CW = 0, 1, 2, 3, 4, 5, 6, 7, 8
V_ROWS = 16


def _sigmoid(x):
    return 0.5 * jnp.tanh(0.5 * x) + 0.5


def _rms_scale(v):
    return lax.rsqrt(jnp.mean(v * v, axis=-1, keepdims=True) + EPS)


def _layer_kernel(sinks_ref, x_ref, rope_ref, vecs_ref, win_ref, wg_ref, wout_ref, o_ref,
                  h_scr, q_scr, km_scr, vt_scr, ga_scr, xl_scr, gl_scr, xc_scr, a_scr, u_scr,
                  hs_scr, ycat_scr, cw_scr, hst_scr, *, batch):
    f32, bf16 = jnp.float32, jnp.bfloat16
    rows = batch * TS
    step = pl.program_id(0)

    @pl.when(step == 0)
    def _init_carries():
        km_scr[...] = jnp.zeros_like(km_scr)
        vt_scr[...] = jnp.zeros_like(vt_scr)
        cw_scr[...] = jnp.zeros_like(cw_scr)
        hst_scr[...] = jnp.zeros_like(hst_scr)

    def vec(row):
        return vecs_ref[row:row + 1, :]

    def norm_b(b, c):
        xb = x_ref[b]
        hb = xb * _rms_scale(xb) * vec(V_LN)
        h_scr[pl.ds(pl.multiple_of(b * TS, TS), TS), :] = hb.astype(bf16)
        return c
    lax.fori_loop(0, batch, norm_b, 0)

    hmat = h_scr[...]

    def proj(c0, c1):
        return jnp.dot(hmat, win_ref[:, c0:c1], preferred_element_type=f32)

    def rope(z, tbl):
        cos_t, sin_hi, sin_lo = rope_ref[tbl], rope_ref[tbl + 1], rope_ref[tbl + 2]
        return (z * cos_t[None] + pltpu.roll(z, 8, 2) * sin_hi[None]
                + pltpu.roll(z, LANES - 8, 2) * sin_lo[None])

    for g in range(GROUP):
        zq = proj(OFF_Q + g * MXU_DIM, OFF_Q + (g + 1) * MXU_DIM).reshape(batch, TS, MXU_DIM)
        for sl in range(MXU_DIM // LANES):
            r = rope(zq[:, :, sl * LANES:(sl + 1) * LANES], 0)
            c0 = g * MXU_DIM + sl * LANES
            q_scr[:, c0:c0 + LANES] = r.reshape(rows, LANES).astype(bf16)

    slot = lax.rem(step, N_SLOT)
    slot_row = pl.multiple_of(slot * TS, TS)
    zkv = proj(OFF_KV, OFF_KV + 2 * KV_WIDTH).reshape(batch, TS, 2 * KV_WIDTH)
    lane_half = lax.broadcasted_iota(jnp.int32, (1, 1, LANES), 2) // HEAD_DIM
    for sl in range(KV_WIDTH // LANES):
        r = rope(zkv[:, :, sl * LANES:(sl + 1) * LANES], 3).astype(bf16)
        for hh in range(LANES // HEAD_DIM):
            km_scr[:, sl * (LANES // HEAD_DIM) + hh, pl.ds(slot_row, TS), :] = jnp.where(
                lane_half == hh, r, jnp.zeros_like(r))
    for b in range(batch):
        vt_scr[b, slot] = zkv[b, :, KV_WIDTH:].T.astype(bf16)

    half = ATTN_WIDTH // 2
    for c in range(2):
        ga_scr[:, c * half:(c + 1) * half] = proj(OFF_GA + c * half, OFF_GA + (c + 1) * half)
        gl_scr[:, c * half:(c + 1) * half] = proj(OFF_GL + c * half, OFF_GL + (c + 1) * half)
        zx = proj(OFF_XL + c * half, OFF_XL + (c + 1) * half)
        for sl in range(half // LANES):
            for b in range(batch):
                xl_scr[c * (half // LANES) + sl, b * PITCH:b * PITCH + TS, :] = (
                    zx[b * TS:(b + 1) * TS, sl * LANES:(sl + 1) * LANES])

    jj = lax.broadcasted_iota(jnp.int32, (TS, GROUP * TS), 0)
    ii = lax.broadcasted_iota(jnp.int32, (TS, GROUP * TS), 1) % TS
    blocks = []
    for sg in range(N_SLOT):
        age = lax.rem(step + (N_SLOT - sg), N_SLOT)
        valid = ((age == 0) & (jj <= ii)) | (age == 1) | ((age == 2) & (jj > ii))
        blocks.append(jnp.where(valid & (step >= age), 0.0, NEG_INF).astype(f32))
    bias_t = jnp.concatenate(blocks, axis=0)
    lane_g = lax.broadcasted_iota(jnp.int32, (1, GROUP * TS), 1) // TS
    sink_rows = []
    for k in range(N_KV_HEADS):
        row = jnp.zeros((1, GROUP * TS), f32)
        for g in range(GROUP):
            row = jnp.where(lane_g == g, sinks_ref[k * GROUP + g] * LOG2E, row)
        sink_rows.append(row)

    def attn_b(b, c):
        r0 = pl.multiple_of(b * TS, TS)
        q_pair = [jnp.concatenate(
            [q_scr[pl.ds(r0, TS), g * MXU_DIM + sl * LANES:g * MXU_DIM + (sl + 1) * LANES]
             for g in range(GROUP)], axis=0) for sl in range(KV_WIDTH // LANES)]
        o_heads = []
        for k in range(N_KV_HEADS):
            st = lax.dot_general(km_scr[b, k], q_pair[k // 2], (((1,), (1,)), ((), ())),
                                 preferred_element_type=f32)
            st = st + bias_t
            m = jnp.maximum(jnp.max(st, axis=0, keepdims=True), sink_rows[k])
            p = jnp.exp2(st - m)
            den = jnp.sum(p, axis=0, keepdims=True) + jnp.exp2(sink_rows[k] - m)
            pb = p.astype(bf16)
            o = jnp.zeros((HEAD_DIM, GROUP * TS), f32)
            for sg in range(N_SLOT):
                o = o + jnp.dot(vt_scr[b, sg, k * HEAD_DIM:(k + 1) * HEAD_DIM, :],
                                pb[sg * TS:(sg + 1) * TS], preferred_element_type=f32)
            o_heads.append(o * (1.0 / den))
        o_rows = jnp.concatenate(o_heads, axis=0).T
        ob = jnp.concatenate([o_rows[g * TS:(g + 1) * TS] for g in range(GROUP)], axis=1)
        ga = ga_scr[pl.ds(r0, TS), :]
        v = ob * (ga * _sigmoid(ga))
        y = v * _rms_scale(v) * vec(V_AG)
        ycat_scr[pl.ds(r0, TS), 0:ATTN_WIDTH] = y.astype(bf16)
        return c
    lax.fori_loop(0, batch, attn_b, 0, unroll=8)

    def conv_t(t, carry):
        x1, x2, x3 = carry
        xt = jnp.concatenate(
            [xl_scr[c, pl.ds(t, batch, stride=PITCH), :] for c in range(N_SLAB)], axis=1)
        u = (vec(V_CB) + vec(V_CW + 3) * xt + vec(V_CW + 2) * x1
             + vec(V_CW + 1) * x2 + vec(V_CW) * x3)
        t8 = pl.multiple_of(t * batch, batch)
        for j in range(N_GRP):
            xc_scr[j, pl.ds(t8, batch), :] = u[:, j * MXU_DIM:(j + 1) * MXU_DIM]
        return xt, x1, x2
    tail = lax.fori_loop(0, TS, conv_t, (cw_scr[0], cw_scr[1], cw_scr[2]), unroll=4)
    for i in range(CONV_W - 1):
        cw_scr[i] = tail[i]

    lam = vec(V_LAM)
    softplus_neg_lam = jnp.maximum(-lam, 0.0) + jnp.log1p(jnp.exp(-jnp.abs(lam)))
    log_a_scale = -LRU_C * softplus_neg_lam
    for j in range(N_GRP):
        cs = slice(j * MXU_DIM, (j + 1) * MXU_DIM)
        xg = xc_scr[j]
        pre = jnp.dot(xg.astype(bf16), wg_ref[j], preferred_element_type=f32)
        r_gate = _sigmoid(pre[:, :MXU_DIM] + vec(V_BR)[:, cs])
        i_gate = _sigmoid(pre[:, MXU_DIM:] + vec(V_BI)[:, cs])
        log_a = r_gate * log_a_scale[:, cs]
        a_scr[j] = jnp.exp(log_a)
        th = jnp.tanh(log_a)
        u_scr[j] = jnp.sqrt(-2.0 * th / (1.0 - th)) * (i_gate * xg)

    def scan_t(t, h):
        t8 = pl.multiple_of(t * batch, batch)
        at = jnp.concatenate([a_scr[j, pl.ds(t8, batch), :] for j in range(N_GRP)], axis=1)
        ut = jnp.concatenate([u_scr[j, pl.ds(t8, batch), :] for j in range(N_GRP)], axis=1)
        h = at * h + ut
        for c in range(N_SLAB):
            hs_scr[c, pl.ds(t, batch, stride=PITCH), :] = h[:, c * LANES:(c + 1) * LANES]
        return h
    hst_scr[...] = lax.fori_loop(0, TS, scan_t, hst_scr[...], unroll=8)

    def lru_out_b(b, c):
        r0 = pl.multiple_of(b * TS, TS)
        p0 = pl.multiple_of(b * PITCH, 8)
        hb = jnp.concatenate([hs_scr[c2, pl.ds(p0, TS), :] for c2 in range(N_SLAB)], axis=1)
        gl = gl_scr[pl.ds(r0, TS), :]
        v = hb * (gl * _sigmoid(gl))
        y = v * _rms_scale(v) * vec(V_LG)
        ycat_scr[pl.ds(r0, TS), ATTN_WIDTH:] = y.astype(bf16)
        return c
    lax.fori_loop(0, batch, lru_out_b, 0)

    ga_scr[...] = jnp.dot(ycat_scr[...], wout_ref[...], preferred_element_type=f32)

    def out_b(b, c):
        x2 = x_ref[b] + ga_scr[pl.ds(pl.multiple_of(b * TS, TS), TS), :]
        o_ref[b] = x2 * _rms_scale(x2) * vec(V_FG)
        return c
    lax.fori_loop(0, batch, out_b, 0)


def _rope_tables(seq):
    pos = jnp.arange(seq, dtype=jnp.float32)
    inv_freq = ROPE_THETA ** (-jnp.arange(0, ROT_DIM, 2, dtype=jnp.float32) / ROT_DIM)
    ang = pos[:, None] * inv_freq[None, :]
    cos, sin = jnp.cos(ang), jnp.sin(ang)
    half = ROT_DIM // 2
    d = np.arange(LANES) % HEAD_DIM
    fidx = d % half
    lo = (d < half)[None, :]
    hi = ((d >= half) & (d < ROT_DIM))[None, :]
    cos_t = jnp.where(lo | hi, cos[:, fidx], 1.0)
    sin_hi = jnp.where(hi, sin[:, fidx], 0.0)
    sin_lo = jnp.where(lo, -sin[:, fidx], 0.0)
    k_tbl = jnp.stack([cos_t, sin_hi, sin_lo])
    return jnp.concatenate([k_tbl * (HEAD_DIM ** -0.5 * LOG2E), k_tbl], axis=0)


def _q_perm():
    g, k, d = np.meshgrid(np.arange(GROUP), np.arange(N_KV_HEADS), np.arange(HEAD_DIM), indexing="ij")
    return ((k * GROUP + g) * HEAD_DIM + d).reshape(-1)


def kernel(x, ln_gain, w_in, sinks, conv_w, conv_b, w_rgate, b_rgate, w_igate, b_igate,
           lru_lambda, attn_out_gain, lru_out_gain, w_out, final_gain):
    batch, seq, d_model = x.shape
    assert d_model == D_MODEL and seq % TS == 0 and ln_gain.shape[0] == 1
    f32, bf16 = jnp.float32, jnp.bfloat16
    rows = batch * TS

    perm = _q_perm()
    w = w_in[0]
    w_q, w_kv = w[:, :ATTN_WIDTH], w[:, ATTN_WIDTH:ATTN_WIDTH + 2 * KV_WIDTH]
    w_ga = w[:, ATTN_WIDTH + 2 * KV_WIDTH:2 * ATTN_WIDTH + 2 * KV_WIDTH]
    w_rest = w[:, 2 * ATTN_WIDTH + 2 * KV_WIDTH:]
    win = jnp.concatenate([w_q[:, perm], w_kv, w_ga[:, perm], w_rest], axis=1).astype(bf16)
    wout = jnp.concatenate([w_out[0][:ATTN_WIDTH][perm], w_out[0][ATTN_WIDTH:]], axis=0).astype(bf16)

    def block_diag(wb):
        per = MXU_DIM // LRU_BLOCK_W
        wb = wb.reshape(N_GRP, per, LRU_BLOCK_W, LRU_BLOCK_W)
        eye = jnp.eye(per, dtype=wb.dtype)
        return jnp.einsum("jpcd,pq->jpcqd", wb, eye).reshape(N_GRP, MXU_DIM, MXU_DIM)
    wg = jnp.concatenate([block_diag(w_rgate[0]), block_diag(w_igate[0])], axis=2).astype(bf16)

    vecs = jnp.zeros((V_ROWS, D_MODEL), f32)
    vecs = vecs.at[V_LN].set(ln_gain[0]).at[V_CB].set(conv_b[0]).at[V_BR].set(b_rgate[0])
    vecs = vecs.at[V_BI].set(b_igate[0]).at[V_LAM].set(lru_lambda[0])
    vecs = vecs.at[V_AG].set(attn_out_gain[0][perm]).at[V_LG].set(lru_out_gain[0])
    vecs = vecs.at[V_FG].set(final_gain).at[V_CW:V_CW + CONV_W].set(conv_w[0])

    rope_tbl = _rope_tables(seq)

    const = lambda *_: (0, 0)
    single = dict(pipeline_mode=pl.Buffered(1))
    grid_spec = pl.GridSpec(
        grid=(seq // TS,),
        in_specs=[
            pl.BlockSpec(memory_space=pltpu.SMEM),
            pl.BlockSpec((batch, TS, D_MODEL), lambda s: (0, s, 0)),
            pl.BlockSpec((6, TS, LANES), lambda s: (0, s, 0)),
            pl.BlockSpec((V_ROWS, D_MODEL), const, **single),
            pl.BlockSpec((D_MODEL, IN_DIM), const, **single),
            pl.BlockSpec((N_GRP, MXU_DIM, 2 * MXU_DIM), lambda s: (0, 0, 0), **single),
            pl.BlockSpec((2 * ATTN_WIDTH, D_MODEL), const, **single),
        ],
        out_specs=pl.BlockSpec((batch, TS, D_MODEL), lambda s: (0, s, 0)),
        scratch_shapes=[
            pltpu.VMEM((rows, D_MODEL), bf16),
            pltpu.VMEM((rows, ATTN_WIDTH), bf16),
            pltpu.VMEM((batch, N_KV_HEADS, KEYS, LANES), bf16),
            pltpu.VMEM((batch, N_SLOT, KV_WIDTH, TS), bf16),
            pltpu.VMEM((rows, ATTN_WIDTH), f32),
            pltpu.VMEM((N_SLAB, batch * PITCH, LANES), f32),
            pltpu.VMEM((rows, LRU_WIDTH), f32),
            pltpu.VMEM((N_GRP, rows, MXU_DIM), f32),
            pltpu.VMEM((N_GRP, rows, MXU_DIM), f32),
            pltpu.VMEM((N_GRP, rows, MXU_DIM), f32),
            pltpu.VMEM((N_SLAB, batch * PITCH, LANES), f32),
            pltpu.VMEM((rows, 2 * ATTN_WIDTH), bf16),
            pltpu.VMEM((CONV_W - 1, batch, LRU_WIDTH), f32),
            pltpu.VMEM((batch, LRU_WIDTH), f32),
        ],
    )
    return pl.pallas_call(
        functools.partial(_layer_kernel, batch=batch),
        grid_spec=grid_spec,
        out_shape=jax.ShapeDtypeStruct(x.shape, x.dtype),
        compiler_params=pltpu.CompilerParams(
            dimension_semantics=("arbitrary",), vmem_limit_bytes=VMEM_LIMIT_BYTES),
        name="hymba_layer",
    )(sinks[0], x, rope_tbl, vecs, win, wg, wout)
```

```python
import functools

import numpy as np
import jax
import jax.numpy as jnp
from jax import lax
from jax.experimental import pallas as pl
from jax.experimental.pallas import tpu as pltpu

D_MODEL = 1024
HEAD_DIM = 64
N_Q_HEADS = 16
N_KV_HEADS = 4
GROUP = N_Q_HEADS // N_KV_HEADS
ATTN_WIDTH = N_Q_HEADS * HEAD_DIM
KV_WIDTH = N_KV_HEADS * HEAD_DIM
WINDOW = 128
ROT_DIM = HEAD_DIM // 4
ROPE_THETA = 500000.0
NEG_INF = -1e30
LRU_WIDTH = 1024
LRU_BLOCKS = 16
LRU_BLOCK_W = LRU_WIDTH // LRU_BLOCKS
CONV_W = 4
LRU_C = 8.0
IN_DIM = 2 * ATTN_WIDTH + 2 * KV_WIDTH + 2 * LRU_WIDTH
EPS = 1e-6

LANES = 128
MXU_DIM = 256
VMEM_LIMIT_BYTES = 58 * 1024 * 1024

TS = 64
N_SLOT = WINDOW // TS + 1
KEYS = N_SLOT * TS
LOG2E = 1.4426950408889634
Q_SCALE = HEAD_DIM ** -0.5 * LOG2E
PITCH = TS + 8
N_SLAB = LRU_WIDTH // LANES
N_GRP = LRU_WIDTH // MXU_DIM

OFF_Q, OFF_KV, OFF_GA = 0, ATTN_WIDTH, ATTN_WIDTH + 2 * KV_WIDTH
OFF_XL, OFF_GL = OFF_GA + ATTN_WIDTH, OFF_GA + ATTN_WIDTH + LRU_WIDTH

V_LN, V_CB, V_BR, V_BI, V_LAM, V_AG, V_LG, V_FG, V_CW = 0, 1, 2, 3, 4, 5, 6, 7, 8
V_ROWS = 16


def _sigmoid(x):
    return 0.5 * jnp.tanh(0.5 * x) + 0.5


def _rms_scale(v):
    return lax.rsqrt(jnp.mean(v * v, axis=-1, keepdims=True) + EPS)


def _layer_kernel(sinks_ref, x_ref, rope_ref, vecs_ref, win_ref, wg_ref, wout_ref, o_ref,
                  h_scr, q_scr, km_scr, vt_scr, ga_scr, xl_scr, gl_scr, xc_scr, a_scr, u_scr,
                  hs_scr, ycat_scr, cw_scr, hst_scr, cvb_scr, *, batch):
    f32, bf16 = jnp.float32, jnp.bfloat16
    rows = batch * TS
    step = pl.program_id(0)

    @pl.when(step == 0)
    def _init_carries():
        km_scr[...] = jnp.zeros_like(km_scr)
        vt_scr[...] = jnp.zeros_like(vt_scr)
        cw_scr[...] = jnp.zeros_like(cw_scr)
        hst_scr[...] = jnp.zeros_like(hst_scr)
        for i, row in enumerate([V_CB] + [V_CW + tap for tap in range(CONV_W)]):
            cvb_scr[i] = jnp.broadcast_to(vecs_ref[row:row + 1, :], (batch, LRU_WIDTH))

    def vec(row):
        return vecs_ref[row:row + 1, :]

    def norm_b(b, c):
        xb = x_ref[b]
        hb = xb * _rms_scale(xb) * vec(V_LN)
        h_scr[pl.ds(pl.multiple_of(b * TS, TS), TS), :] = hb.astype(bf16)
        return c
    lax.fori_loop(0, batch, norm_b, 0, unroll=True)

    hmat = h_scr[...]

    def proj(c0, c1):
        return jnp.dot(hmat, win_ref[:, c0:c1], preferred_element_type=f32)

    k_tab = (rope_ref[0], rope_ref[1], rope_ref[2])
    q_tab = tuple(t * Q_SCALE for t in k_tab)

    def rope(z, tab):
        cos_t, sin_hi, sin_lo = tab
        return (z * cos_t[None] + pltpu.roll(z, 8, 2) * sin_hi[None]
                + pltpu.roll(z, LANES - 8, 2) * sin_lo[None])

    for g in range(GROUP):
        zq = proj(OFF_Q + g * MXU_DIM, OFF_Q + (g + 1) * MXU_DIM).reshape(batch, TS, MXU_DIM)
        for sl in range(MXU_DIM // LANES):
            r = rope(zq[:, :, sl * LANES:(sl + 1) * LANES], q_tab)
            c0 = g * MXU_DIM + sl * LANES
            q_scr[:, c0:c0 + LANES] = r.reshape(rows, LANES).astype(bf16)

    slot = lax.rem(step, N_SLOT)
    slot_row = pl.multiple_of(slot * TS, TS)
    zkv = proj(OFF_KV, OFF_KV + 2 * KV_WIDTH).reshape(batch, TS, 2 * KV_WIDTH)
    lane_half = lax.broadcasted_iota(jnp.int32, (1, 1, LANES), 2) // HEAD_DIM
    for sl in range(KV_WIDTH // LANES):
        r = rope(zkv[:, :, sl * LANES:(sl + 1) * LANES], k_tab).astype(bf16)
        for hh in range(LANES // HEAD_DIM):
            km_scr[:, sl * (LANES // HEAD_DIM) + hh, pl.ds(slot_row, TS), :] = jnp.where(
                lane_half == hh, r, jnp.zeros_like(r))
    for b in range(batch):
        vt_scr[b, slot] = zkv[b, :, KV_WIDTH:].T.astype(bf16)

    half = ATTN_WIDTH // 2
    for c in range(2):
        ga_scr[:, c * half:(c + 1) * half] = proj(OFF_GA + c * half, OFF_GA + (c + 1) * half)
        gl_scr[:, c * half:(c + 1) * half] = proj(OFF_GL + c * half, OFF_GL + (c + 1) * half)
        zx = proj(OFF_XL + c * half, OFF_XL + (c + 1) * half)
        for sl in range(half // LANES):
            for b in range(batch):
                xl_scr[c * (half // LANES) + sl, b * PITCH:b * PITCH + TS, :] = (
                    zx[b * TS:(b + 1) * TS, sl * LANES:(sl + 1) * LANES])

    jj = lax.broadcasted_iota(jnp.int32, (TS, GROUP * TS), 0)
    ii = lax.broadcasted_iota(jnp.int32, (TS, GROUP * TS), 1) % TS
    blocks = []
    for sg in range(N_SLOT):
        age = lax.rem(step + (N_SLOT - sg), N_SLOT)
        valid = ((age == 0) & (jj <= ii)) | (age == 1) | ((age == 2) & (jj > ii))
        blocks.append(jnp.where(valid & (step >= age), 0.0, NEG_INF).astype(f32))
    bias_t = jnp.concatenate(blocks, axis=0)
    lane_g = lax.broadcasted_iota(jnp.int32, (1, GROUP * TS), 1) // TS
    sink_rows = []
    for k in range(N_KV_HEADS):
        row = jnp.zeros((1, GROUP * TS), f32)
        for g in range(GROUP):
            row = jnp.where(lane_g == g, sinks_ref[k * GROUP + g] * LOG2E, row)
        sink_rows.append(row)

    def attn_b(b, c):
        r0 = pl.multiple_of(b * TS, TS)
        q_pair = [jnp.concatenate(
            [q_scr[pl.ds(r0, TS), g * MXU_DIM + sl * LANES:g * MXU_DIM + (sl + 1) * LANES]
             for g in range(GROUP)], axis=0) for sl in range(KV_WIDTH // LANES)]
        o_heads = []
        for k in range(N_KV_HEADS):
            st = lax.dot_general(km_scr[b, k], q_pair[k // 2], (((1,), (1,)), ((), ())),
                                 preferred_element_type=f32)
            st = st + bias_t
            m = jnp.maximum(jnp.max(st, axis=0, keepdims=True), sink_rows[k])
            p = jnp.exp2(st - m)
            den = jnp.sum(p, axis=0, keepdims=True) + jnp.exp2(sink_rows[k] - m)
            pb = p.astype(bf16)
            o = jnp.zeros((HEAD_DIM, GROUP * TS), f32)
            for sg in range(N_SLOT):
                o = o + jnp.dot(vt_scr[b, sg, k * HEAD_DIM:(k + 1) * HEAD_DIM, :],
                                pb[sg * TS:(sg + 1) * TS], preferred_element_type=f32)
            o_heads.append(o * (1.0 / den))
        o_rows = jnp.concatenate(o_heads, axis=0).T
        ob = jnp.concatenate([o_rows[g * TS:(g + 1) * TS] for g in range(GROUP)], axis=1)
        ga = ga_scr[pl.ds(r0, TS), :]
        v = ob * (ga * _sigmoid(ga))
        y = v * _rms_scale(v) * vec(V_AG)
        ycat_scr[pl.ds(r0, TS), 0:ATTN_WIDTH] = y.astype(bf16)
        return c
    lax.fori_loop(0, batch, attn_b, 0, unroll=8)

    def conv_t(t, carry):
        x1, x2, x3 = carry
        xt = jnp.concatenate(
            [xl_scr[c, pl.ds(t, batch, stride=PITCH), :] for c in range(N_SLAB)], axis=1)
        u = cvb_scr[0] + cvb_scr[4] * xt + cvb_scr[3] * x1 + cvb_scr[2] * x2 + cvb_scr[1] * x3
        t8 = pl.multiple_of(t * batch, batch)
        for j in range(N_GRP):
            xc_scr[j, pl.ds(t8, batch), :] = u[:, j * MXU_DIM:(j + 1) * MXU_DIM]
        return xt, x1, x2
    tail = lax.fori_loop(0, TS, conv_t, (cw_scr[0], cw_scr[1], cw_scr[2]), unroll=4)
    for i in range(CONV_W - 1):
        cw_scr[i] = tail[i]

    lam = vec(V_LAM)
    softplus_neg_lam = jnp.maximum(-lam, 0.0) + jnp.log1p(jnp.exp(-jnp.abs(lam)))
    log_a_scale = -LRU_C * softplus_neg_lam
    for j in range(N_GRP):
        cs = slice(j * MXU_DIM, (j + 1) * MXU_DIM)
        xg = xc_scr[j]
        pre = jnp.dot(xg.astype(bf16), wg_ref[j], preferred_element_type=f32)
        r_gate = _sigmoid(pre[:, :MXU_DIM] + vec(V_BR)[:, cs])
        i_gate = _sigmoid(pre[:, MXU_DIM:] + vec(V_BI)[:, cs])
        log_a = r_gate * log_a_scale[:, cs]
        a_scr[j] = jnp.exp(log_a)
        th = jnp.tanh(log_a)
        u_scr[j] = jnp.sqrt(-2.0 * th / (1.0 - th)) * (i_gate * xg)

    def scan_t(t, h):
        t8 = pl.multiple_of(t * batch, batch)
        at = jnp.concatenate([a_scr[j, pl.ds(t8, batch), :] for j in range(N_GRP)], axis=1)
        ut = jnp.concatenate([u_scr[j, pl.ds(t8, batch), :] for j in range(N_GRP)], axis=1)
        h = at * h + ut
        for c in range(N_SLAB):
            hs_scr[c, pl.ds(t, batch, stride=PITCH), :] = h[:, c * LANES:(c + 1) * LANES]
        return h
    hst_scr[...] = lax.fori_loop(0, TS, scan_t, hst_scr[...], unroll=8)

    def lru_out_b(b, c):
        r0 = pl.multiple_of(b * TS, TS)
        p0 = pl.multiple_of(b * PITCH, 8)
        hb = jnp.concatenate([hs_scr[c2, pl.ds(p0, TS), :] for c2 in range(N_SLAB)], axis=1)
        gl = gl_scr[pl.ds(r0, TS), :]
        v = hb * (gl * _sigmoid(gl))
        y = v * _rms_scale(v) * vec(V_LG)
        ycat_scr[pl.ds(r0, TS), ATTN_WIDTH:] = y.astype(bf16)
        return c
    lax.fori_loop(0, batch, lru_out_b, 0, unroll=True)

    ga_scr[...] = jnp.dot(ycat_scr[...], wout_ref[...], preferred_element_type=f32)

    def out_b(b, c):
        x2 = x_ref[b] + ga_scr[pl.ds(pl.multiple_of(b * TS, TS), TS), :]
        o_ref[b] = x2 * _rms_scale(x2) * vec(V_FG)
        return c
    lax.fori_loop(0, batch, out_b, 0, unroll=True)


def _rope_tables(seq):
    half = ROT_DIM // 2
    pos = jnp.arange(seq, dtype=jnp.float32)
    inv_freq = ROPE_THETA ** (-jnp.arange(0, ROT_DIM, 2, dtype=jnp.float32) / ROT_DIM)
    ang = pos[:, None] * jnp.tile(inv_freq, LANES // half)[None, :]
    cos, sin = jnp.cos(ang), jnp.sin(ang)
    d = np.arange(LANES) % HEAD_DIM
    lo = (d < half)[None, :]
    hi = ((d >= half) & (d < ROT_DIM))[None, :]
    return jnp.stack([jnp.where(lo | hi, cos, 1.0),
                      jnp.where(hi, sin, 0.0),
                      jnp.where(lo, -sin, 0.0)])


def _group_major(a, axis):
    shp = a.shape
    a = a.reshape(shp[:axis] + (N_KV_HEADS, GROUP, HEAD_DIM) + shp[axis + 1:])
    return jnp.swapaxes(a, axis, axis + 1).reshape(shp)


def kernel(x, ln_gain, w_in, sinks, conv_w, conv_b, w_rgate, b_rgate, w_igate, b_igate,
           lru_lambda, attn_out_gain, lru_out_gain, w_out, final_gain):
    batch, seq, d_model = x.shape
    assert d_model == D_MODEL and seq % TS == 0 and ln_gain.shape[0] == 1 and N_SLOT == 3
    f32, bf16 = jnp.float32, jnp.bfloat16
    rows = batch * TS

    w = w_in[0]
    win = jnp.concatenate(
        [_group_major(w[:, :OFF_KV], 1), w[:, OFF_KV:OFF_GA],
         _group_major(w[:, OFF_GA:OFF_XL], 1), w[:, OFF_XL:]], axis=1).astype(bf16)
    wout = jnp.concatenate(
        [_group_major(w_out[0][:ATTN_WIDTH], 0), w_out[0][ATTN_WIDTH:]], axis=0).astype(bf16)

    def block_diag(wb):
        per = MXU_DIM // LRU_BLOCK_W
        wb = wb.reshape(N_GRP, per, LRU_BLOCK_W, 1, LRU_BLOCK_W)
        eye = np.eye(per, dtype=np.float32).reshape(1, per, 1, per, 1)
        return (wb * eye).reshape(N_GRP, MXU_DIM, MXU_DIM)
    wg = jnp.concatenate([block_diag(w_rgate[0]), block_diag(w_igate[0])], axis=2).astype(bf16)

    vec_rows = [None] * (V_CW + CONV_W)
    vec_rows[V_LN], vec_rows[V_CB], vec_rows[V_BR] = ln_gain, conv_b, b_rgate
    vec_rows[V_BI], vec_rows[V_LAM], vec_rows[V_LG] = b_igate, lru_lambda, lru_out_gain
    vec_rows[V_AG], vec_rows[V_FG] = _group_major(attn_out_gain, 1), final_gain[None]
    vec_rows[V_CW:] = [conv_w[0][tap:tap + 1] for tap in range(CONV_W)]
    vecs = jnp.concatenate(vec_rows + [jnp.zeros((V_ROWS - len(vec_rows), D_MODEL), f32)], axis=0)

    rope_tbl = _rope_tables(seq)

    const = lambda *_: (0, 0)
    single = dict(pipeline_mode=pl.Buffered(1))
    grid_spec = pl.GridSpec(
        grid=(seq // TS,),
        in_specs=[
            pl.BlockSpec(memory_space=pltpu.SMEM),
            pl.BlockSpec((batch, TS, D_MODEL), lambda s: (0, s, 0)),
            pl.BlockSpec((3, TS, LANES), lambda s: (0, s, 0)),
            pl.BlockSpec((V_ROWS, D_MODEL), const, **single),
            pl.BlockSpec((D_MODEL, IN_DIM), const, **single),
            pl.BlockSpec((N_GRP, MXU_DIM, 2 * MXU_DIM), lambda s: (0, 0, 0), **single),
            pl.BlockSpec((2 * ATTN_WIDTH, D_MODEL), const, **single),
        ],
        out_specs=pl.BlockSpec((batch, TS, D_MODEL), lambda s: (0, s, 0)),
        scratch_shapes=[
            pltpu.VMEM((rows, D_MODEL), bf16),
            pltpu.VMEM((rows, ATTN_WIDTH), bf16),
            pltpu.VMEM((batch, N_KV_HEADS, KEYS, LANES), bf16),
            pltpu.VMEM((batch, N_SLOT, KV_WIDTH, TS), bf16),
            pltpu.VMEM((rows, ATTN_WIDTH), f32),
            pltpu.VMEM((N_SLAB, batch * PITCH, LANES), f32),
            pltpu.VMEM((rows, LRU_WIDTH), f32),
            pltpu.VMEM((N_GRP, rows, MXU_DIM), f32),
            pltpu.VMEM((N_GRP, rows, MXU_DIM), f32),
            pltpu.VMEM((N_GRP, rows, MXU_DIM), f32),
            pltpu.VMEM((N_SLAB, batch * PITCH, LANES), f32),
            pltpu.VMEM((rows, 2 * ATTN_WIDTH), bf16),
            pltpu.VMEM((CONV_W - 1, batch, LRU_WIDTH), f32),
            pltpu.VMEM((batch, LRU_WIDTH), f32),
            pltpu.VMEM((CONV_W + 1, batch, LRU_WIDTH), f32),
        ],
    )
    return pl.pallas_call(
        functools.partial(_layer_kernel, batch=batch),
        grid_spec=grid_spec,
        out_shape=jax.ShapeDtypeStruct(x.shape, x.dtype),
        compiler_params=pltpu.CompilerParams(
            dimension_semantics=("arbitrary",), vmem_limit_bytes=VMEM_LIMIT_BYTES),
        name="hymba_layer",
    )(sinks[0], x, rope_tbl, vecs, win, wg, wout)
```

```python
import functools

import numpy as np
import jax
import jax.numpy as jnp
from jax import lax
from jax.experimental import pallas as pl
from jax.experimental.pallas import tpu as pltpu

D_MODEL = 1024
HEAD_DIM = 64
N_Q_HEADS = 16
N_KV_HEADS = 4
GROUP = N_Q_HEADS // N_KV_HEADS
ATTN_WIDTH = N_Q_HEADS * HEAD_DIM
KV_WIDTH = N_KV_HEADS * HEAD_DIM
WINDOW = 128
ROT_DIM = HEAD_DIM // 4
ROPE_THETA = 500000.0
NEG_INF = -1e30
LRU_WIDTH = 1024
LRU_BLOCKS = 16
LRU_BLOCK_W = LRU_WIDTH // LRU_BLOCKS
CONV_W = 4
LRU_C = 8.0
EPS = 1e-6

LANES = 128
MXU_DIM = 256
VMEM_LIMIT_BYTES = 58 * 1024 * 1024

TS = 64
N_SLOT = WINDOW // TS + 1
KEYS = N_SLOT * TS
LOG2E = 1.4426950408889634
Q_SCALE = HEAD_DIM ** -0.5 * LOG2E
PITCH = TS + 8
N_SLAB = LRU_WIDTH // LANES
N_GRP = LRU_WIDTH // MXU_DIM

OFF_KV, OFF_GA, OFF_XL = ATTN_WIDTH, ATTN_WIDTH + 2 * KV_WIDTH, 2 * ATTN_WIDTH + 2 * KV_WIDTH

V_LN, V_CB, V_BR, V_BI, V_LAM, V_AG, V_LG, V_FG, V_CW = 0, 1, 2, 3, 4, 5, 6, 7, 8
V_ROWS = 16


def _sigmoid(x):
    return 0.5 * jnp.tanh(0.5 * x) + 0.5


def _rms_scale(v):
    return lax.rsqrt(jnp.mean(v * v, axis=-1, keepdims=True) + EPS)


def _layer_kernel(sinks_ref, x_ref, rope_ref, vecs_ref, wq_ref, wkv_ref, wga_ref, wlru_ref,
                  wg_ref, woa_ref, wol_ref, o_ref,
                  h_scr, q_scr, km_scr, vt_scr, ga_scr, xl_scr, gl_scr, xc_scr, a_scr, u_scr,
                  hs_scr, ycat_scr, cw_scr, hst_scr, cvb_scr, *, batch):
    f32, bf16 = jnp.float32, jnp.bfloat16
    rows = batch * TS
    step = pl.program_id(0)

    @pl.when(step == 0)
    def _init_carries():
        km_scr[...] = jnp.zeros_like(km_scr)
        vt_scr[...] = jnp.zeros_like(vt_scr)
        cw_scr[...] = jnp.zeros_like(cw_scr)
        hst_scr[...] = jnp.zeros_like(hst_scr)
        for i, row in enumerate([V_CB] + [V_CW + tap for tap in range(CONV_W)]):
            cvb_scr[i] = jnp.broadcast_to(vecs_ref[row:row + 1, :], (batch, LRU_WIDTH))

    def vec(row):
        return vecs_ref[row:row + 1, :]

    def norm_b(b, c):
        xb = x_ref[b]
        hb = xb * _rms_scale(xb) * vec(V_LN)
        h_scr[pl.ds(pl.multiple_of(b * TS, TS), TS), :] = hb.astype(bf16)
        return c
    lax.fori_loop(0, batch, norm_b, 0, unroll=True)

    hmat = h_scr[...]

    def proj(w_ref, c0, c1):
        return jnp.dot(hmat, w_ref[:, c0:c1], preferred_element_type=f32)

    k_tab = (rope_ref[0], rope_ref[1], rope_ref[2])
    q_tab = tuple(t * Q_SCALE for t in k_tab)

    def rope(z, tab):
        cos_t, sin_hi, sin_lo = tab
        return (z * cos_t[None] + pltpu.roll(z, 8, 2) * sin_hi[None]
                + pltpu.roll(z, LANES - 8, 2) * sin_lo[None])

    for g in range(GROUP):
        zq = proj(wq_ref, g * MXU_DIM, (g + 1) * MXU_DIM).reshape(batch, TS, MXU_DIM)
        for sl in range(MXU_DIM // LANES):
            r = rope(zq[:, :, sl * LANES:(sl + 1) * LANES], q_tab)
            c0 = g * MXU_DIM + sl * LANES
            q_scr[:, c0:c0 + LANES] = r.reshape(rows, LANES).astype(bf16)

    slot = lax.rem(step, N_SLOT)
    slot_row = pl.multiple_of(slot * TS, TS)
    zkv = proj(wkv_ref, 0, 2 * KV_WIDTH).reshape(batch, TS, 2 * KV_WIDTH)
    lane_half = lax.broadcasted_iota(jnp.int32, (1, 1, LANES), 2) // HEAD_DIM
    for sl in range(KV_WIDTH // LANES):
        r = rope(zkv[:, :, sl * LANES:(sl + 1) * LANES], k_tab).astype(bf16)
        for hh in range(LANES // HEAD_DIM):
            km_scr[:, sl * (LANES // HEAD_DIM) + hh, pl.ds(slot_row, TS), :] = jnp.where(
                lane_half == hh, r, jnp.zeros_like(r))
    for b in range(batch):
        vt_scr[b, slot] = zkv[b, :, KV_WIDTH:].T.astype(bf16)

    half = ATTN_WIDTH // 2
    for c in range(2):
        ga_scr[:, c * half:(c + 1) * half] = proj(wga_ref, c * half, (c + 1) * half)
        gl_scr[:, c * half:(c + 1) * half] = proj(wlru_ref, LRU_WIDTH + c * half, LRU_WIDTH + (c + 1) * half)
        zx = proj(wlru_ref, c * half, (c + 1) * half)
        for sl in range(half // LANES):
            for b in range(batch):
                xl_scr[c * (half // LANES) + sl, b * PITCH:b * PITCH + TS, :] = (
                    zx[b * TS:(b + 1) * TS, sl * LANES:(sl + 1) * LANES])

    jj = lax.broadcasted_iota(jnp.int32, (TS, GROUP * TS), 0)
    ii = lax.broadcasted_iota(jnp.int32, (TS, GROUP * TS), 1) % TS
    blocks = []
    for sg in range(N_SLOT):
        age = lax.rem(step + (N_SLOT - sg), N_SLOT)
        valid = ((age == 0) & (jj <= ii)) | (age == 1) | ((age == 2) & (jj > ii))
        blocks.append(jnp.where(valid & (step >= age), 0.0, NEG_INF).astype(f32))
    bias_t = jnp.concatenate(blocks, axis=0)
    lane_g = lax.broadcasted_iota(jnp.int32, (1, GROUP * TS), 1) // TS
    sink_rows = []
    for k in range(N_KV_HEADS):
        row = jnp.zeros((1, GROUP * TS), f32)
        for g in range(GROUP):
            row = jnp.where(lane_g == g, sinks_ref[k * GROUP + g] * LOG2E, row)
        sink_rows.append(row)

    def attn_b(b, c):
        r0 = pl.multiple_of(b * TS, TS)
        q_pair = [jnp.concatenate(
            [q_scr[pl.ds(r0, TS), g * MXU_DIM + sl * LANES:g * MXU_DIM + (sl + 1) * LANES]
             for g in range(GROUP)], axis=0) for sl in range(KV_WIDTH // LANES)]
        o_heads = []
        for k in range(N_KV_HEADS):
            st = lax.dot_general(km_scr[b, k], q_pair[k // 2], (((1,), (1,)), ((), ())),
                                 preferred_element_type=f32)
            st = st + bias_t
            m = jnp.maximum(jnp.max(st, axis=0, keepdims=True), sink_rows[k])
            p = jnp.exp2(st - m)
            den = jnp.sum(p, axis=0, keepdims=True) + jnp.exp2(sink_rows[k] - m)
            pb = p.astype(bf16)
            o = jnp.zeros((HEAD_DIM, GROUP * TS), f32)
            for sg in range(N_SLOT):
                o = o + jnp.dot(vt_scr[b, sg, k * HEAD_DIM:(k + 1) * HEAD_DIM, :],
                                pb[sg * TS:(sg + 1) * TS], preferred_element_type=f32)
            o_heads.append(o * (1.0 / den))
        o_rows = jnp.concatenate(o_heads, axis=0).T
        ob = jnp.concatenate([o_rows[g * TS:(g + 1) * TS] for g in range(GROUP)], axis=1)
        ga = ga_scr[pl.ds(r0, TS), :]
        v = ob * (ga * _sigmoid(ga))
        y = v * _rms_scale(v) * vec(V_AG)
        ycat_scr[pl.ds(r0, TS), 0:ATTN_WIDTH] = y.astype(bf16)
        return c
    lax.fori_loop(0, batch, attn_b, 0, unroll=8)

    def conv_t(t, carry):
        x1, x2, x3 = carry
        xt = jnp.concatenate(
            [xl_scr[c, pl.ds(t, batch, stride=PITCH), :] for c in range(N_SLAB)], axis=1)
        u = cvb_scr[0] + cvb_scr[4] * xt + cvb_scr[3] * x1 + cvb_scr[2] * x2 + cvb_scr[1] * x3
        t8 = pl.multiple_of(t * batch, batch)
        for j in range(N_GRP):
            xc_scr[j, pl.ds(t8, batch), :] = u[:, j * MXU_DIM:(j + 1) * MXU_DIM]
        return xt, x1, x2
    tail = lax.fori_loop(0, TS, conv_t, (cw_scr[0], cw_scr[1], cw_scr[2]), unroll=True)
    for i in range(CONV_W - 1):
        cw_scr[i] = tail[i]

    lam = vec(V_LAM)
    softplus_neg_lam = jnp.maximum(-lam, 0.0) + jnp.log1p(jnp.exp(-jnp.abs(lam)))
    log_a_scale = -LRU_C * softplus_neg_lam
    for j in range(N_GRP):
        cs = slice(j * MXU_DIM, (j + 1) * MXU_DIM)
        xg = xc_scr[j]
        pre = jnp.dot(xg.astype(bf16), wg_ref[j], preferred_element_type=f32)
        r_gate = _sigmoid(pre[:, :MXU_DIM] + vec(V_BR)[:, cs])
        i_gate = _sigmoid(pre[:, MXU_DIM:] + vec(V_BI)[:, cs])
        log_a = r_gate * log_a_scale[:, cs]
        a_scr[j] = jnp.exp(log_a)
        th = jnp.tanh(log_a)
        u_scr[j] = jnp.sqrt(-2.0 * th / (1.0 - th)) * (i_gate * xg)

    def scan_t(t, h):
        t8 = pl.multiple_of(t * batch, batch)
        at = jnp.concatenate([a_scr[j, pl.ds(t8, batch), :] for j in range(N_GRP)], axis=1)
        ut = jnp.concatenate([u_scr[j, pl.ds(t8, batch), :] for j in range(N_GRP)], axis=1)
        h = at * h + ut
        for c in range(N_SLAB):
            hs_scr[c, pl.ds(t, batch, stride=PITCH), :] = h[:, c * LANES:(c + 1) * LANES]
        return h
    hst_scr[...] = lax.fori_loop(0, TS, scan_t, hst_scr[...], unroll=True)

    def lru_out_b(b, c):
        r0 = pl.multiple_of(b * TS, TS)
        p0 = pl.multiple_of(b * PITCH, 8)
        hb = jnp.concatenate([hs_scr[c2, pl.ds(p0, TS), :] for c2 in range(N_SLAB)], axis=1)
        gl = gl_scr[pl.ds(r0, TS), :]
        v = hb * (gl * _sigmoid(gl))
        y = v * _rms_scale(v) * vec(V_LG)
        ycat_scr[pl.ds(r0, TS), ATTN_WIDTH:] = y.astype(bf16)
        return c
    lax.fori_loop(0, batch, lru_out_b, 0, unroll=True)

    ga_scr[...] = (
        jnp.dot(ycat_scr[:, :ATTN_WIDTH], woa_ref[...], preferred_element_type=f32)
        + jnp.dot(ycat_scr[:, ATTN_WIDTH:], wol_ref[...], preferred_element_type=f32))

    def out_b(b, c):
        x2 = x_ref[b] + ga_scr[pl.ds(pl.multiple_of(b * TS, TS), TS), :]
        o_ref[b] = x2 * _rms_scale(x2) * vec(V_FG)
        return c
    lax.fori_loop(0, batch, out_b, 0, unroll=True)


def _rope_tables(seq):
    half = ROT_DIM // 2
    pos = jnp.arange(seq, dtype=jnp.float32)
    inv_freq = ROPE_THETA ** (-jnp.arange(0, ROT_DIM, 2, dtype=jnp.float32) / ROT_DIM)
    ang = pos[:, None] * jnp.tile(inv_freq, LANES // half)[None, :]
    cos, sin = jnp.cos(ang), jnp.sin(ang)
    d = np.arange(LANES) % HEAD_DIM
    lo = (d < half)[None, :]
    hi = ((d >= half) & (d < ROT_DIM))[None, :]
    return jnp.stack([jnp.where(lo | hi, cos, 1.0),
                      jnp.where(hi, sin, 0.0),
                      jnp.where(lo, -sin, 0.0)])


def _group_major(a, axis):
    shp = a.shape
    a = a.reshape(shp[:axis] + (N_KV_HEADS, GROUP, HEAD_DIM) + shp[axis + 1:])
    return jnp.swapaxes(a, axis, axis + 1).reshape(shp)


def kernel(x, ln_gain, w_in, sinks, conv_w, conv_b, w_rgate, b_rgate, w_igate, b_igate,
           lru_lambda, attn_out_gain, lru_out_gain, w_out, final_gain):
    batch, seq, d_model = x.shape
    assert d_model == D_MODEL and seq % TS == 0 and ln_gain.shape[0] == 1 and N_SLOT == 3
    f32, bf16 = jnp.float32, jnp.bfloat16
    rows = batch * TS

    w = w_in[0]
    wq = _group_major(w[:, :OFF_KV], 1).astype(bf16)
    wkv = w[:, OFF_KV:OFF_GA].astype(bf16)
    wga = _group_major(w[:, OFF_GA:OFF_XL], 1).astype(bf16)
    wlru = w[:, OFF_XL:].astype(bf16)
    woa = _group_major(w_out[0][:ATTN_WIDTH], 0).astype(bf16)
    wol = w_out[0][ATTN_WIDTH:].astype(bf16)

    def block_diag(wb):
        per = MXU_DIM // LRU_BLOCK_W
        wb = wb.reshape(N_GRP, per, LRU_BLOCK_W, 1, LRU_BLOCK_W)
        eye = np.eye(per, dtype=np.float32).reshape(1, per, 1, per, 1)
        return (wb * eye).reshape(N_GRP, MXU_DIM, MXU_DIM)
    wg = jnp.concatenate([block_diag(w_rgate[0]), block_diag(w_igate[0])], axis=2).astype(bf16)

    vec_rows = [None] * (V_CW + CONV_W)
    vec_rows[V_LN], vec_rows[V_CB], vec_rows[V_BR] = ln_gain, conv_b, b_rgate
    vec_rows[V_BI], vec_rows[V_LAM], vec_rows[V_LG] = b_igate, lru_lambda, lru_out_gain
    vec_rows[V_AG], vec_rows[V_FG] = _group_major(attn_out_gain, 1), final_gain[None]
    vec_rows[V_CW:] = [conv_w[0][tap:tap + 1] for tap in range(CONV_W)]
    vecs = jnp.concatenate(vec_rows + [jnp.zeros((V_ROWS - len(vec_rows), D_MODEL), f32)], axis=0)

    rope_tbl = _rope_tables(seq)

    const = lambda *_: (0, 0)
    single = dict(pipeline_mode=pl.Buffered(1))
    grid_spec = pl.GridSpec(
        grid=(seq // TS,),
        in_specs=[
            pl.BlockSpec(memory_space=pltpu.SMEM),
            pl.BlockSpec((batch, TS, D_MODEL), lambda s: (0, s, 0)),
            pl.BlockSpec((3, TS, LANES), lambda s: (0, s, 0)),
            pl.BlockSpec((V_ROWS, D_MODEL), const, **single),
            pl.BlockSpec((D_MODEL, ATTN_WIDTH), const, **single),
            pl.BlockSpec((D_MODEL, 2 * KV_WIDTH), const, **single),
            pl.BlockSpec((D_MODEL, ATTN_WIDTH), const, **single),
            pl.BlockSpec((D_MODEL, 2 * LRU_WIDTH), const, **single),
            pl.BlockSpec((N_GRP, MXU_DIM, 2 * MXU_DIM), lambda s: (0, 0, 0), **single),
            pl.BlockSpec((ATTN_WIDTH, D_MODEL), const, **single),
            pl.BlockSpec((LRU_WIDTH, D_MODEL), const, **single),
        ],
        out_specs=pl.BlockSpec((batch, TS, D_MODEL), lambda s: (0, s, 0)),
        scratch_shapes=[
            pltpu.VMEM((rows, D_MODEL), bf16),
            pltpu.VMEM((rows, ATTN_WIDTH), bf16),
            pltpu.VMEM((batch, N_KV_HEADS, KEYS, LANES), bf16),
            pltpu.VMEM((batch, N_SLOT, KV_WIDTH, TS), bf16),
            pltpu.VMEM((rows, ATTN_WIDTH), f32),
            pltpu.VMEM((N_SLAB, batch * PITCH, LANES), f32),
            pltpu.VMEM((rows, LRU_WIDTH), f32),
            pltpu.VMEM((N_GRP, rows, MXU_DIM), f32),
            pltpu.VMEM((N_GRP, rows, MXU_DIM), f32),
            pltpu.VMEM((N_GRP, rows, MXU_DIM), f32),
            pltpu.VMEM((N_SLAB, batch * PITCH, LANES), f32),
            pltpu.VMEM((rows, 2 * ATTN_WIDTH), bf16),
            pltpu.VMEM((CONV_W - 1, batch, LRU_WIDTH), f32),
            pltpu.VMEM((batch, LRU_WIDTH), f32),
            pltpu.VMEM((CONV_W + 1, batch, LRU_WIDTH), f32),
        ],
    )
    return pl.pallas_call(
        functools.partial(_layer_kernel, batch=batch),
        grid_spec=grid_spec,
        out_shape=jax.ShapeDtypeStruct(x.shape, x.dtype),
        compiler_params=pltpu.CompilerParams(
            dimension_semantics=("arbitrary",), vmem_limit_bytes=VMEM_LIMIT_BYTES),
        name="hymba_layer",
    )(sinks[0], x, rope_tbl, vecs, wq, wkv, wga, wlru, wg, woa, wol)
```

```python
import functools

import numpy as np
import jax
import jax.numpy as jnp
from jax import lax
from jax.experimental import pallas as pl
from jax.experimental.pallas import tpu as pltpu

D_MODEL = 1024
HEAD_DIM = 64
N_Q_HEADS = 16
N_KV_HEADS = 4
GROUP = N_Q_HEADS // N_KV_HEADS
ATTN_WIDTH = N_Q_HEADS * HEAD_DIM
KV_WIDTH = N_KV_HEADS * HEAD_DIM
WINDOW = 128
ROT_DIM = HEAD_DIM // 4
ROPE_THETA = 500000.0
NEG_INF = -1e30
LRU_WIDTH = 1024
LRU_BLOCKS = 16
LRU_BLOCK_W = LRU_WIDTH // LRU_BLOCKS
CONV_W = 4
LRU_C = 8.0
EPS = 1e-6

LANES = 128
SUBLANES = 8
MXU_DIM = 256
VMEM_LIMIT_BYTES = 58 * 1024 * 1024

TS = 64
N_SLOT = WINDOW // TS + 1
KEYS = N_SLOT * TS
LOG2E = 1.4426950408889634
Q_SCALE = HEAD_DIM ** -0.5 * LOG2E
PITCH = TS + 8
N_SLAB = LRU_WIDTH // LANES
N_GRP = LRU_WIDTH // MXU_DIM

OFF_KV, OFF_GA, OFF_XL = ATTN_WIDTH, ATTN_WIDTH + 2 * KV_WIDTH, 2 * ATTN_WIDTH + 2 * KV_WIDTH

V_LN, V_CB, V_BR, V_BI, V_LAM, V_AG, V_LG, V_FG, V_CW = 0, 1, 2, 3, 4, 5, 6, 7, 8
V_ROWS = 16


def _silu_from_half(hx):
    return hx * (jnp.tanh(hx) + 1.0)


def _rms_scale(v):
    return lax.rsqrt(jnp.mean(v * v, axis=-1, keepdims=True) + EPS)


def _layer_kernel(sinks_ref, x_ref, rope_ref, vecs_ref, wq_ref, wkv_ref, wga_ref, wlru_ref,
                  wg_ref, woa_ref, wol_ref, o_ref,
                  h_scr, q_scr, km_scr, vt_scr, ga_scr, xl_scr, gl_scr, xc_scr, a_scr, u_scr,
                  hs_scr, ycat_scr, cw_scr, hst_scr, vb_scr, *, batch):
    f32, bf16 = jnp.float32, jnp.bfloat16
    rows = batch * TS
    step = pl.program_id(0)

    @pl.when(step == 0)
    def _init_carries():
        km_scr[...] = jnp.zeros_like(km_scr)
        vt_scr[...] = jnp.zeros_like(vt_scr)
        cw_scr[...] = jnp.zeros_like(cw_scr)
        hst_scr[...] = jnp.zeros_like(hst_scr)
        for row in range(V_ROWS):
            vb_scr[row] = jnp.broadcast_to(vecs_ref[row:row + 1, :], (SUBLANES, D_MODEL))

    def by_tile(v, op, tile):
        return op(v.reshape(-1, SUBLANES, v.shape[-1]), tile[None]).reshape(v.shape)

    def scale_by(v, row):
        return by_tile(v, jnp.multiply, vb_scr[row])

    def norm_b(b, c):
        xb = x_ref[b]
        hb = scale_by(xb * _rms_scale(xb), V_LN)
        h_scr[pl.ds(pl.multiple_of(b * TS, TS), TS), :] = hb.astype(bf16)
        return c
    lax.fori_loop(0, batch, norm_b, 0, unroll=True)

    hmat = h_scr[...]

    def proj(w_ref, c0, c1):
        return jnp.dot(hmat, w_ref[:, c0:c1], preferred_element_type=f32)

    k_tab = (rope_ref[0], rope_ref[1], rope_ref[2])
    q_tab = tuple(t * Q_SCALE for t in k_tab)

    def rope(z, tab):
        cos_t, sin_hi, sin_lo = tab
        return (z * cos_t[None] + pltpu.roll(z, 8, 2) * sin_hi[None]
                + pltpu.roll(z, LANES - 8, 2) * sin_lo[None])

    for g in range(GROUP):
        zq = proj(wq_ref, g * MXU_DIM, (g + 1) * MXU_DIM).reshape(batch, TS, MXU_DIM)
        for sl in range(MXU_DIM // LANES):
            r = rope(zq[:, :, sl * LANES:(sl + 1) * LANES], q_tab)
            c0 = g * MXU_DIM + sl * LANES
            q_scr[:, c0:c0 + LANES] = r.reshape(rows, LANES).astype(bf16)

    slot = lax.rem(step, N_SLOT)
    slot_row = pl.multiple_of(slot * TS, TS)
    zkv = proj(wkv_ref, 0, 2 * KV_WIDTH).reshape(batch, TS, 2 * KV_WIDTH)
    lane_half = lax.broadcasted_iota(jnp.int32, (1, 1, LANES), 2) // HEAD_DIM
    for sl in range(KV_WIDTH // LANES):
        r = rope(zkv[:, :, sl * LANES:(sl + 1) * LANES], k_tab).astype(bf16)
        for hh in range(LANES // HEAD_DIM):
            km_scr[:, sl * (LANES // HEAD_DIM) + hh, pl.ds(slot_row, TS), :] = jnp.where(
                lane_half == hh, r, jnp.zeros_like(r))
    for b in range(batch):
        vt_scr[b, slot] = zkv[b, :, KV_WIDTH:].T.astype(bf16)

    half = ATTN_WIDTH // 2
    for c in range(2):
        ga_scr[:, c * half:(c + 1) * half] = proj(wga_ref, c * half, (c + 1) * half)
        gl_scr[:, c * half:(c + 1) * half] = proj(wlru_ref, LRU_WIDTH + c * half, LRU_WIDTH + (c + 1) * half)
        zx = proj(wlru_ref, c * half, (c + 1) * half)
        for sl in range(half // LANES):
            for b in range(batch):
                xl_scr[c * (half // LANES) + sl, b * PITCH:b * PITCH + TS, :] = (
                    zx[b * TS:(b + 1) * TS, sl * LANES:(sl + 1) * LANES])

    jj = lax.broadcasted_iota(jnp.int32, (TS, GROUP * TS), 0)
    ii = lax.broadcasted_iota(jnp.int32, (TS, GROUP * TS), 1) % TS
    blocks = []
    for sg in range(N_SLOT):
        age = lax.rem(step + (N_SLOT - sg), N_SLOT)
        valid = ((age == 0) & (jj <= ii)) | (age == 1) | ((age == 2) & (jj > ii))
        blocks.append(jnp.where(valid & (step >= age), 0.0, NEG_INF).astype(f32))
    bias_t = jnp.concatenate(blocks, axis=0)
    lane_g = lax.broadcasted_iota(jnp.int32, (1, GROUP * TS), 1) // TS
    sink_rows = []
    for k in range(N_KV_HEADS):
        row = jnp.zeros((1, GROUP * TS), f32)
        for g in range(GROUP):
            row = jnp.where(lane_g == g, sinks_ref[k * GROUP + g] * LOG2E, row)
        sink_rows.append(row)

    def attn_b(b, c):
        r0 = pl.multiple_of(b * TS, TS)
        q_pair = [jnp.concatenate(
            [q_scr[pl.ds(r0, TS), g * MXU_DIM + sl * LANES:g * MXU_DIM + (sl + 1) * LANES]
             for g in range(GROUP)], axis=0) for sl in range(KV_WIDTH // LANES)]
        o_heads = []
        for k in range(N_KV_HEADS):
            st = lax.dot_general(km_scr[b, k], q_pair[k // 2], (((1,), (1,)), ((), ())),
                                 preferred_element_type=f32)
            st = st + bias_t
            m = jnp.maximum(jnp.max(st, axis=0, keepdims=True), sink_rows[k])
            p = jnp.exp2(st - m)
            den = jnp.sum(p, axis=0, keepdims=True) + jnp.exp2(sink_rows[k] - m)
            pb = p.astype(bf16)
            o = jnp.zeros((HEAD_DIM, GROUP * TS), f32)
            for sg in range(N_SLOT):
                o = o + jnp.dot(vt_scr[b, sg, k * HEAD_DIM:(k + 1) * HEAD_DIM, :],
                                pb[sg * TS:(sg + 1) * TS], preferred_element_type=f32)
            o_heads.append(o * (1.0 / den))
        o_rows = jnp.concatenate(o_heads, axis=0).T
        ob = jnp.concatenate([o_rows[g * TS:(g + 1) * TS] for g in range(GROUP)], axis=1)
        v = ob * _silu_from_half(ga_scr[pl.ds(r0, TS), :])
        y = scale_by(v * _rms_scale(v), V_AG)
        ycat_scr[pl.ds(r0, TS), 0:ATTN_WIDTH] = y.astype(bf16)
        return c
    lax.fori_loop(0, batch, attn_b, 0, unroll=8)

    def conv_t(t, carry):
        x1, x2, x3 = carry
        xt = jnp.concatenate(
            [xl_scr[c, pl.ds(t, batch, stride=PITCH), :] for c in range(N_SLAB)], axis=1)
        u = (vb_scr[V_CB] + vb_scr[V_CW + 3] * xt + vb_scr[V_CW + 2] * x1
             + vb_scr[V_CW + 1] * x2 + vb_scr[V_CW] * x3)
        t8 = pl.multiple_of(t * batch, batch)
        for j in range(N_GRP):
            xc_scr[j, pl.ds(t8, batch), :] = u[:, j * MXU_DIM:(j + 1) * MXU_DIM]
        return xt, x1, x2
    tail = lax.fori_loop(0, TS, conv_t, (cw_scr[0], cw_scr[1], cw_scr[2]), unroll=True)
    for i in range(CONV_W - 1):
        cw_scr[i] = tail[i]

    lam = vb_scr[V_LAM]
    softplus_neg_lam = jnp.maximum(-lam, 0.0) + jnp.log1p(jnp.exp(-jnp.abs(lam)))
    half_scale = (-0.5 * LRU_C) * softplus_neg_lam
    for j in range(N_GRP):
        cs = slice(j * MXU_DIM, (j + 1) * MXU_DIM)
        xg = xc_scr[j]
        pre = jnp.dot(xg.astype(bf16), wg_ref[j], preferred_element_type=f32)
        t_r = jnp.tanh(by_tile(pre[:, :MXU_DIM], jnp.add, vb_scr[V_BR][:, cs]))
        t_i = jnp.tanh(by_tile(pre[:, MXU_DIM:], jnp.add, vb_scr[V_BI][:, cs]))
        log_a = by_tile(by_tile(t_r, jnp.multiply, half_scale[:, cs]), jnp.add, half_scale[:, cs])
        a_scr[j] = jnp.exp(log_a)
        th = jnp.tanh(log_a)
        u_scr[j] = (jnp.sqrt(-0.5 * th) * lax.rsqrt(1.0 - th)) * ((t_i + 1.0) * xg)

    def scan_t(t, h):
        t8 = pl.multiple_of(t * batch, batch)
        at = jnp.concatenate([a_scr[j, pl.ds(t8, batch), :] for j in range(N_GRP)], axis=1)
        ut = jnp.concatenate([u_scr[j, pl.ds(t8, batch), :] for j in range(N_GRP)], axis=1)
        h = at * h + ut
        for c in range(N_SLAB):
            hs_scr[c, pl.ds(t, batch, stride=PITCH), :] = h[:, c * LANES:(c + 1) * LANES]
        return h
    hst_scr[...] = lax.fori_loop(0, TS, scan_t, hst_scr[...], unroll=True)

    def lru_out_b(b, c):
        r0 = pl.multiple_of(b * TS, TS)
        p0 = pl.multiple_of(b * PITCH, 8)
        hb = jnp.concatenate([hs_scr[c2, pl.ds(p0, TS), :] for c2 in range(N_SLAB)], axis=1)
        v = hb * _silu_from_half(gl_scr[pl.ds(r0, TS), :])
        y = scale_by(v * _rms_scale(v), V_LG)
        ycat_scr[pl.ds(r0, TS), ATTN_WIDTH:] = y.astype(bf16)
        return c
    lax.fori_loop(0, batch, lru_out_b, 0, unroll=True)

    ga_scr[...] = (
        jnp.dot(ycat_scr[:, :ATTN_WIDTH], woa_ref[...], preferred_element_type=f32)
        + jnp.dot(ycat_scr[:, ATTN_WIDTH:], wol_ref[...], preferred_element_type=f32))

    def out_b(b, c):
        x2 = x_ref[b] + ga_scr[pl.ds(pl.multiple_of(b * TS, TS), TS), :]
        o_ref[b] = scale_by(x2 * _rms_scale(x2), V_FG)
        return c
    lax.fori_loop(0, batch, out_b, 0, unroll=True)


def _rope_tables(seq):
    half = ROT_DIM // 2
    pos = jnp.arange(seq, dtype=jnp.float32)
    inv_freq = ROPE_THETA ** (-jnp.arange(0, ROT_DIM, 2, dtype=jnp.float32) / ROT_DIM)
    ang = pos[:, None] * jnp.tile(inv_freq, LANES // half)[None, :]
    cos, sin = jnp.cos(ang), jnp.sin(ang)
    d = np.arange(LANES) % HEAD_DIM
    lo = (d < half)[None, :]
    hi = ((d >= half) & (d < ROT_DIM))[None, :]
    return jnp.stack([jnp.where(lo | hi, cos, 1.0),
                      jnp.where(hi, sin, 0.0),
                      jnp.where(lo, -sin, 0.0)])


def _group_major(a, axis):
    shp = a.shape
    a = a.reshape(shp[:axis] + (N_KV_HEADS, GROUP, HEAD_DIM) + shp[axis + 1:])
    return jnp.swapaxes(a, axis, axis + 1).reshape(shp)


def kernel(x, ln_gain, w_in, sinks, conv_w, conv_b, w_rgate, b_rgate, w_igate, b_igate,
           lru_lambda, attn_out_gain, lru_out_gain, w_out, final_gain):
    batch, seq, d_model = x.shape
    assert d_model == D_MODEL and seq % TS == 0 and ln_gain.shape[0] == 1 and N_SLOT == 3
    assert batch == SUBLANES
    f32, bf16 = jnp.float32, jnp.bfloat16
    rows = batch * TS

    w = w_in[0]
    wq = _group_major(w[:, :OFF_KV], 1).astype(bf16)
    wkv = w[:, OFF_KV:OFF_GA].astype(bf16)
    wga = (0.5 * _group_major(w[:, OFF_GA:OFF_XL], 1)).astype(bf16)
    gate_half = np.repeat(np.float32([1.0, 0.5]), LRU_WIDTH)[None, :]
    wlru = (w[:, OFF_XL:] * gate_half).astype(bf16)
    woa = _group_major(w_out[0][:ATTN_WIDTH], 0).astype(bf16)
    wol = w_out[0][ATTN_WIDTH:].astype(bf16)

    def block_diag(wb):
        per = MXU_DIM // LRU_BLOCK_W
        wb = wb.reshape(N_GRP, per, LRU_BLOCK_W, 1, LRU_BLOCK_W)
        eye = np.eye(per, dtype=np.float32).reshape(1, per, 1, per, 1)
        return (wb * (0.5 * eye)).reshape(N_GRP, MXU_DIM, MXU_DIM)
    wg = jnp.concatenate([block_diag(w_rgate[0]), block_diag(w_igate[0])], axis=2).astype(bf16)

    vec_rows = [None] * (V_CW + CONV_W)
    vec_rows[V_LN], vec_rows[V_CB], vec_rows[V_BR] = ln_gain, conv_b, 0.5 * b_rgate
    vec_rows[V_BI], vec_rows[V_LAM], vec_rows[V_LG] = 0.5 * b_igate, lru_lambda, lru_out_gain
    vec_rows[V_AG], vec_rows[V_FG] = _group_major(attn_out_gain, 1), final_gain[None]
    vec_rows[V_CW:] = [conv_w[0][tap:tap + 1] for tap in range(CONV_W)]
    vecs = jnp.concatenate(vec_rows + [jnp.zeros((V_ROWS - len(vec_rows), D_MODEL), f32)], axis=0)

    rope_tbl = _rope_tables(seq)

    const = lambda *_: (0, 0)
    single = dict(pipeline_mode=pl.Buffered(1))
    grid_spec = pl.GridSpec(
        grid=(seq // TS,),
        in_specs=[
            pl.BlockSpec(memory_space=pltpu.SMEM),
            pl.BlockSpec((batch, TS, D_MODEL), lambda s: (0, s, 0)),
            pl.BlockSpec((3, TS, LANES), lambda s: (0, s, 0)),
            pl.BlockSpec((V_ROWS, D_MODEL), const, **single),
            pl.BlockSpec((D_MODEL, ATTN_WIDTH), const, **single),
            pl.BlockSpec((D_MODEL, 2 * KV_WIDTH), const, **single),
            pl.BlockSpec((D_MODEL, ATTN_WIDTH), const, **single),
            pl.BlockSpec((D_MODEL, 2 * LRU_WIDTH), const, **single),
            pl.BlockSpec((N_GRP, MXU_DIM, 2 * MXU_DIM), lambda s: (0, 0, 0), **single),
            pl.BlockSpec((ATTN_WIDTH, D_MODEL), const, **single),
            pl.BlockSpec((LRU_WIDTH, D_MODEL), const, **single),
        ],
        out_specs=pl.BlockSpec((batch, TS, D_MODEL), lambda s: (0, s, 0)),
        scratch_shapes=[
            pltpu.VMEM((rows, D_MODEL), bf16),
            pltpu.VMEM((rows, ATTN_WIDTH), bf16),
            pltpu.VMEM((batch, N_KV_HEADS, KEYS, LANES), bf16),
            pltpu.VMEM((batch, N_SLOT, KV_WIDTH, TS), bf16),
            pltpu.VMEM((rows, ATTN_WIDTH), f32),
            pltpu.VMEM((N_SLAB, batch * PITCH, LANES), f32),
            pltpu.VMEM((rows, LRU_WIDTH), f32),
            pltpu.VMEM((N_GRP, rows, MXU_DIM), f32),
            pltpu.VMEM((N_GRP, rows, MXU_DIM), f32),
            pltpu.VMEM((N_GRP, rows, MXU_DIM), f32),
            pltpu.VMEM((N_SLAB, batch * PITCH, LANES), f32),
            pltpu.VMEM((rows, 2 * ATTN_WIDTH), bf16),
            pltpu.VMEM((CONV_W - 1, batch, LRU_WIDTH), f32),
            pltpu.VMEM((batch, LRU_WIDTH), f32),
            pltpu.VMEM((V_ROWS, SUBLANES, D_MODEL), f32),
        ],
    )
    return pl.pallas_call(
        functools.partial(_layer_kernel, batch=batch),
        grid_spec=grid_spec,
        out_shape=jax.ShapeDtypeStruct(x.shape, x.dtype),
        compiler_params=pltpu.CompilerParams(
            dimension_semantics=("arbitrary",), vmem_limit_bytes=VMEM_LIMIT_BYTES),
        name="hymba_layer",
    )(sinks[0], x, rope_tbl, vecs, wq, wkv, wga, wlru, wg, woa, wol)
```

```python
import functools

import numpy as np
import jax
import jax.numpy as jnp
from jax import lax
from jax.experimental import pallas as pl
from jax.experimental.pallas import tpu as pltpu

D_MODEL = 1024
HEAD_DIM = 64
N_Q_HEADS = 16
N_KV_HEADS = 4
GROUP = N_Q_HEADS // N_KV_HEADS
ATTN_WIDTH = N_Q_HEADS * HEAD_DIM
KV_WIDTH = N_KV_HEADS * HEAD_DIM
WINDOW = 128
ROT_DIM = HEAD_DIM // 4
ROPE_THETA = 500000.0
NEG_INF = -1e30
LRU_WIDTH = 1024
LRU_BLOCKS = 16
LRU_BLOCK_W = LRU_WIDTH // LRU_BLOCKS
CONV_W = 4
LRU_C = 8.0
EPS = 1e-6

LANES = 128
SUBLANES = 8
MXU_DIM = 256
VMEM_LIMIT_BYTES = 58 * 1024 * 1024

TS = 64
N_SLOT = WINDOW // TS + 1
KEYS = N_SLOT * TS
LOG2E = 1.4426950408889634
Q_SCALE = HEAD_DIM ** -0.5 * LOG2E
PITCH = TS + 8
N_SLAB = LRU_WIDTH // LANES
N_GRP = LRU_WIDTH // MXU_DIM
PROJ_COLS = 512
LRU_SEGS = 4

OFF_KV, OFF_GA, OFF_XL = ATTN_WIDTH, ATTN_WIDTH + 2 * KV_WIDTH, 2 * ATTN_WIDTH + 2 * KV_WIDTH

V_LN, V_CB, V_BR, V_BI, V_LAM, V_AG, V_LG, V_FG, V_CW = 0, 1, 2, 3, 4, 5, 6, 7, 8
V_ROWS = 16


def _silu_from_half(hx):
    return hx * (jnp.tanh(hx) + 1.0)


def _rms_scale(v):
    return lax.rsqrt(jnp.mean(v * v, axis=-1, keepdims=True) + EPS)


def _layer_kernel(sinks_ref, x_ref, rope_ref, vecs_ref, wq_ref, wkv_ref, wga_ref,
                  wxl0_ref, wxl1_ref, wgl0_ref, wgl1_ref, wg_ref, woa_ref, wol_ref, o_ref,
                  h_scr, q_scr, km_scr, vt_scr, ga_scr, xl_scr, gl_scr, xc_scr, a_scr, u_scr,
                  hs_scr, ycat_scr, cw_scr, hst_scr, vb_scr, bias_scr, *, batch):
    f32, bf16 = jnp.float32, jnp.bfloat16
    rows = batch * TS
    step = pl.program_id(0)

    @pl.when(step == 0)
    def _init_carries():
        km_scr[...] = jnp.zeros_like(km_scr)
        vt_scr[...] = jnp.zeros_like(vt_scr)
        cw_scr[...] = jnp.zeros_like(cw_scr)
        hst_scr[...] = jnp.zeros_like(hst_scr)
        for row in range(V_ROWS):
            vb_scr[row] = jnp.broadcast_to(vecs_ref[row:row + 1, :], (SUBLANES, D_MODEL))

    def by_tile(v, op, tile):
        return op(v.reshape(-1, SUBLANES, v.shape[-1]), tile[None]).reshape(v.shape)

    def scale_by(v, row):
        return by_tile(v, jnp.multiply, vb_scr[row])

    def norm_pieces():
        def piece(b):
            xb = x_ref[b]
            hb = scale_by(xb * _rms_scale(xb), V_LN)
            h_scr[b * TS:(b + 1) * TS, :] = hb.astype(bf16)
        return [functools.partial(piece, b) for b in range(batch)]

    def proj(w_ref, c0, c1):
        return jnp.dot(h_scr[...], w_ref[:, c0:c1], preferred_element_type=f32)

    def rope(z, scale):
        cos_t, sin_hi, sin_lo = (rope_ref[i] * scale if scale != 1.0 else rope_ref[i] for i in range(3))
        return (z * cos_t[None] + pltpu.roll(z, 8, 2) * sin_hi[None]
                + pltpu.roll(z, LANES - 8, 2) * sin_lo[None])

    def q_piece(g):
        zq = proj(wq_ref, g * MXU_DIM, (g + 1) * MXU_DIM).reshape(batch, TS, MXU_DIM)
        for sl in range(MXU_DIM // LANES):
            r = rope(zq[:, :, sl * LANES:(sl + 1) * LANES], Q_SCALE)
            c0 = g * MXU_DIM + sl * LANES
            q_scr[:, c0:c0 + LANES] = r.reshape(rows, LANES).astype(bf16)

    def kv_piece():
        slot = lax.rem(step, N_SLOT)
        slot_row = pl.multiple_of(slot * TS, TS)
        zkv = proj(wkv_ref, 0, 2 * KV_WIDTH).reshape(batch, TS, 2 * KV_WIDTH)
        lane_half = lax.broadcasted_iota(jnp.int32, (1, 1, LANES), 2) // HEAD_DIM
        for sl in range(KV_WIDTH // LANES):
            r = rope(zkv[:, :, sl * LANES:(sl + 1) * LANES], 1.0).astype(bf16)
            for hh in range(LANES // HEAD_DIM):
                km_scr[:, sl * (LANES // HEAD_DIM) + hh, pl.ds(slot_row, TS), :] = jnp.where(
                    lane_half == hh, r, jnp.zeros_like(r))
        for b in range(batch):
            vt_scr[b, slot] = zkv[b, :, KV_WIDTH:].T.astype(bf16)

    def ga_piece(c):
        ga_scr[:, c * PROJ_COLS:(c + 1) * PROJ_COLS] = proj(wga_ref, c * PROJ_COLS, (c + 1) * PROJ_COLS)

    def gl_piece(c):
        gl_scr[:, c * PROJ_COLS:(c + 1) * PROJ_COLS] = proj((wgl0_ref, wgl1_ref)[c], 0, PROJ_COLS)

    def xl_piece(c):
        zx = proj((wxl0_ref, wxl1_ref)[c], 0, PROJ_COLS)
        for sl in range(PROJ_COLS // LANES):
            for b in range(batch):
                xl_scr[c * (PROJ_COLS // LANES) + sl, b * PITCH:b * PITCH + TS, :] = (
                    zx[b * TS:(b + 1) * TS, sl * LANES:(sl + 1) * LANES])

    def attn_pieces():
        jj = lax.broadcasted_iota(jnp.int32, (TS, GROUP * TS), 0)
        ii = lax.broadcasted_iota(jnp.int32, (TS, GROUP * TS), 1) % TS
        blocks = []
        for sg in range(N_SLOT):
            age = lax.rem(step + (N_SLOT - sg), N_SLOT)
            valid = ((age == 0) & (jj <= ii)) | (age == 1) | ((age == 2) & (jj > ii))
            blocks.append(jnp.where(valid & (step >= age), 0.0, NEG_INF).astype(f32))
        bias_scr[...] = jnp.concatenate(blocks, axis=0)
        lane_g = lax.broadcasted_iota(jnp.int32, (1, GROUP * TS), 1) // TS
        sink_rows = []
        for k in range(N_KV_HEADS):
            row = jnp.zeros((1, GROUP * TS), f32)
            for g in range(GROUP):
                row = jnp.where(lane_g == g, sinks_ref[k * GROUP + g] * LOG2E, row)
            sink_rows.append(row)

        def piece(b):
            r0 = b * TS
            q_pair = [jnp.concatenate(
                [q_scr[r0:r0 + TS, g * MXU_DIM + sl * LANES:g * MXU_DIM + (sl + 1) * LANES]
                 for g in range(GROUP)], axis=0) for sl in range(KV_WIDTH // LANES)]
            o_heads = []
            for k in range(N_KV_HEADS):
                st = lax.dot_general(km_scr[b, k], q_pair[k // 2], (((1,), (1,)), ((), ())),
                                     preferred_element_type=f32)
                st = st + bias_scr[...]
                m = jnp.maximum(jnp.max(st, axis=0, keepdims=True), sink_rows[k])
                p = jnp.exp2(st - m)
                den = jnp.sum(p, axis=0, keepdims=True) + jnp.exp2(sink_rows[k] - m)
                pb = p.astype(bf16)
                o = jnp.zeros((HEAD_DIM, GROUP * TS), f32)
                for sg in range(N_SLOT):
                    o = o + jnp.dot(vt_scr[b, sg, k * HEAD_DIM:(k + 1) * HEAD_DIM, :],
                                    pb[sg * TS:(sg + 1) * TS], preferred_element_type=f32)
                o_heads.append(o * (1.0 / den))
            o_rows = jnp.concatenate(o_heads, axis=0).T
            ob = jnp.concatenate([o_rows[g * TS:(g + 1) * TS] for g in range(GROUP)], axis=1)
            v = ob * _silu_from_half(ga_scr[r0:r0 + TS, :])
            y = scale_by(v * _rms_scale(v), V_AG)
            ycat_scr[r0:r0 + TS, 0:ATTN_WIDTH] = y.astype(bf16)
        return [functools.partial(piece, b) for b in range(batch)]

    seg = TS // LRU_SEGS
    carry = {}

    def conv_piece(i):
        x1, x2, x3 = (cw_scr[0], cw_scr[1], cw_scr[2]) if i == 0 else carry["conv"]
        for t in range(i * seg, (i + 1) * seg):
            xt = jnp.concatenate(
                [xl_scr[c, pl.ds(t, batch, stride=PITCH), :] for c in range(N_SLAB)], axis=1)
            u = (vb_scr[V_CB] + vb_scr[V_CW + 3] * xt + vb_scr[V_CW + 2] * x1
                 + vb_scr[V_CW + 1] * x2 + vb_scr[V_CW] * x3)
            for j in range(N_GRP):
                xc_scr[j, t * batch:(t + 1) * batch, :] = u[:, j * MXU_DIM:(j + 1) * MXU_DIM]
            x1, x2, x3 = xt, x1, x2
        carry["conv"] = (x1, x2, x3)
        if i == LRU_SEGS - 1:
            cw_scr[0], cw_scr[1], cw_scr[2] = x1, x2, x3

    def gate_piece(j):
        cs = slice(j * MXU_DIM, (j + 1) * MXU_DIM)
        lam = vb_scr[V_LAM][:, cs]
        softplus_neg_lam = jnp.maximum(-lam, 0.0) + jnp.log1p(jnp.exp(-jnp.abs(lam)))
        half_scale = (-0.5 * LRU_C) * softplus_neg_lam
        xg = xc_scr[j]
        pre = jnp.dot(xg.astype(bf16), wg_ref[j], preferred_element_type=f32)
        t_r = jnp.tanh(by_tile(pre[:, :MXU_DIM], jnp.add, vb_scr[V_BR][:, cs]))
        t_i = jnp.tanh(by_tile(pre[:, MXU_DIM:], jnp.add, vb_scr[V_BI][:, cs]))
        log_a = by_tile(by_tile(t_r, jnp.multiply, half_scale), jnp.add, half_scale)
        a_scr[j] = jnp.exp(log_a)
        th = jnp.tanh(log_a)
        u_scr[j] = (jnp.sqrt(-0.5 * th) * lax.rsqrt(1.0 - th)) * ((t_i + 1.0) * xg)

    def scan_piece(i):
        h = hst_scr[...] if i == 0 else carry["h"]
        for t in range(i * seg, (i + 1) * seg):
            rs = slice(t * batch, (t + 1) * batch)
            at = jnp.concatenate([a_scr[j, rs, :] for j in range(N_GRP)], axis=1)
            ut = jnp.concatenate([u_scr[j, rs, :] for j in range(N_GRP)], axis=1)
            h = at * h + ut
            for c in range(N_SLAB):
                hs_scr[c, pl.ds(t, batch, stride=PITCH), :] = h[:, c * LANES:(c + 1) * LANES]
        carry["h"] = h
        if i == LRU_SEGS - 1:
            hst_scr[...] = h

    def lru_out_piece(b):
        r0, p0 = b * TS, b * PITCH
        hb = jnp.concatenate([hs_scr[c2, p0:p0 + TS, :] for c2 in range(N_SLAB)], axis=1)
        v = hb * _silu_from_half(gl_scr[r0:r0 + TS, :])
        y = scale_by(v * _rms_scale(v), V_LG)
        ycat_scr[r0:r0 + TS, ATTN_WIDTH:] = y.astype(bf16)

    def outproj_lru_piece(c):
        cs = slice(c * MXU_DIM, (c + 1) * MXU_DIM)
        gl_scr[:, cs] = jnp.dot(ycat_scr[:, ATTN_WIDTH:], wol_ref[:, cs], preferred_element_type=f32)

    def outproj_attn_piece(c):
        cs = slice(c * MXU_DIM, (c + 1) * MXU_DIM)
        gl_scr[:, cs] = gl_scr[:, cs] + jnp.dot(
            ycat_scr[:, :ATTN_WIDTH], woa_ref[:, cs], preferred_element_type=f32)

    def final_piece(b):
        x2 = x_ref[b] + gl_scr[b * TS:(b + 1) * TS, :]
        o_ref[b] = scale_by(x2 * _rms_scale(x2), V_FG)

    def emit(*phases):
        order = sorted(((i + 1) / len(ph), n, i) for n, ph in enumerate(phases) for i in range(len(ph)))
        for _, n, i in order:
            phases[n][i]()

    def pieces(fn, n):
        return [functools.partial(fn, i) for i in range(n)]

    n_half = LRU_WIDTH // PROJ_COLS
    n_out = D_MODEL // MXU_DIM
    attn = attn_pieces()
    emit(norm_pieces())
    emit(pieces(xl_piece, n_half))
    emit(pieces(conv_piece, LRU_SEGS), pieces(q_piece, GROUP)[:2])
    emit(pieces(gate_piece, N_GRP), pieces(q_piece, GROUP)[2:] + [kv_piece, functools.partial(ga_piece, 0)])
    emit(pieces(scan_piece, LRU_SEGS), [functools.partial(ga_piece, 1)])
    emit(attn[:batch // 2], pieces(gl_piece, n_half))
    emit(attn[batch // 2:], pieces(lru_out_piece, batch))
    emit(pieces(outproj_lru_piece, n_out))
    emit(pieces(outproj_attn_piece, n_out))
    emit(pieces(final_piece, batch))


def _rope_tables(seq):
    half = ROT_DIM // 2
    pos = jnp.arange(seq, dtype=jnp.float32)
    inv_freq = ROPE_THETA ** (-jnp.arange(0, ROT_DIM, 2, dtype=jnp.float32) / ROT_DIM)
    ang = pos[:, None] * jnp.tile(inv_freq, LANES // half)[None, :]
    cos, sin = jnp.cos(ang), jnp.sin(ang)
    d = np.arange(LANES) % HEAD_DIM
    lo = (d < half)[None, :]
    hi = ((d >= half) & (d < ROT_DIM))[None, :]
    return jnp.stack([jnp.where(lo | hi, cos, 1.0),
                      jnp.where(hi, sin, 0.0),
                      jnp.where(lo, -sin, 0.0)])


def _group_major(a, axis):
    shp = a.shape
    a = a.reshape(shp[:axis] + (N_KV_HEADS, GROUP, HEAD_DIM) + shp[axis + 1:])
    return jnp.swapaxes(a, axis, axis + 1).reshape(shp)


def kernel(x, ln_gain, w_in, sinks, conv_w, conv_b, w_rgate, b_rgate, w_igate, b_igate,
           lru_lambda, attn_out_gain, lru_out_gain, w_out, final_gain):
    batch, seq, d_model = x.shape
    assert d_model == D_MODEL and seq % TS == 0 and ln_gain.shape[0] == 1 and N_SLOT == 3
    assert batch == SUBLANES
    f32, bf16 = jnp.float32, jnp.bfloat16
    rows = batch * TS

    col_scale = np.ones((1, OFF_XL + 2 * LRU_WIDTH), np.float32)
    col_scale[:, OFF_GA:OFF_XL] = 0.5
    col_scale[:, OFF_XL + LRU_WIDTH:] = 0.5
    w_all = (w_in[0] * col_scale).astype(bf16)
    wq = _group_major(w_all[:, :OFF_KV], 1)
    wga = _group_major(w_all[:, OFF_GA:OFF_XL], 1)
    wo_all = w_out[0].astype(bf16)
    woa = _group_major(wo_all[:ATTN_WIDTH], 0)

    def block_diag(wb):
        per = MXU_DIM // LRU_BLOCK_W
        wb = wb.reshape(N_GRP, per, LRU_BLOCK_W, 1, LRU_BLOCK_W)
        eye = np.eye(per, dtype=np.float32).reshape(1, per, 1, per, 1)
        return (wb * (0.5 * eye)).reshape(N_GRP, MXU_DIM, MXU_DIM)
    wg = jnp.concatenate([block_diag(w_rgate[0]), block_diag(w_igate[0])], axis=2).astype(bf16)

    vec_rows = [None] * (V_CW + CONV_W)
    vec_rows[V_LN], vec_rows[V_CB], vec_rows[V_BR] = ln_gain, conv_b, 0.5 * b_rgate
    vec_rows[V_BI], vec_rows[V_LAM], vec_rows[V_LG] = 0.5 * b_igate, lru_lambda, lru_out_gain
    vec_rows[V_AG], vec_rows[V_FG] = _group_major(attn_out_gain, 1), final_gain[None]
    vec_rows[V_CW:] = [conv_w[0][tap:tap + 1] for tap in range(CONV_W)]
    vecs = jnp.concatenate(vec_rows + [jnp.zeros((V_ROWS - len(vec_rows), D_MODEL), f32)], axis=0)

    rope_tbl = _rope_tables(seq)

    const = lambda *_: (0, 0)
    single = dict(pipeline_mode=pl.Buffered(1))

    def col_block(off):
        return pl.BlockSpec((D_MODEL, PROJ_COLS), lambda s: (0, off // PROJ_COLS), **single)
    grid_spec = pl.GridSpec(
        grid=(seq // TS,),
        in_specs=[
            pl.BlockSpec(memory_space=pltpu.SMEM),
            pl.BlockSpec((batch, TS, D_MODEL), lambda s: (0, s, 0)),
            pl.BlockSpec((3, TS, LANES), lambda s: (0, s, 0)),
            pl.BlockSpec((V_ROWS, D_MODEL), const, **single),
            pl.BlockSpec((D_MODEL, ATTN_WIDTH), const, **single),
            col_block(OFF_KV),
            pl.BlockSpec((D_MODEL, ATTN_WIDTH), const, **single),
            col_block(OFF_XL), col_block(OFF_XL + PROJ_COLS),
            col_block(OFF_XL + LRU_WIDTH), col_block(OFF_XL + LRU_WIDTH + PROJ_COLS),
            pl.BlockSpec((N_GRP, MXU_DIM, 2 * MXU_DIM), lambda s: (0, 0, 0), **single),
            pl.BlockSpec((ATTN_WIDTH, D_MODEL), const, **single),
            pl.BlockSpec((LRU_WIDTH, D_MODEL), lambda s: (ATTN_WIDTH // LRU_WIDTH, 0), **single),
        ],
        out_specs=pl.BlockSpec((batch, TS, D_MODEL), lambda s: (0, s, 0)),
        scratch_shapes=[
            pltpu.VMEM((rows, D_MODEL), bf16),
            pltpu.VMEM((rows, ATTN_WIDTH), bf16),
            pltpu.VMEM((batch, N_KV_HEADS, KEYS, LANES), bf16),
            pltpu.VMEM((batch, N_SLOT, KV_WIDTH, TS), bf16),
            pltpu.VMEM((rows, ATTN_WIDTH), f32),
            pltpu.VMEM((N_SLAB, batch * PITCH, LANES), f32),
            pltpu.VMEM((rows, LRU_WIDTH), f32),
            pltpu.VMEM((N_GRP, rows, MXU_DIM), f32),
            pltpu.VMEM((N_GRP, rows, MXU_DIM), f32),
            pltpu.VMEM((N_GRP, rows, MXU_DIM), f32),
            pltpu.VMEM((N_SLAB, batch * PITCH, LANES), f32),
            pltpu.VMEM((rows, 2 * ATTN_WIDTH), bf16),
            pltpu.VMEM((CONV_W - 1, batch, LRU_WIDTH), f32),
            pltpu.VMEM((batch, LRU_WIDTH), f32),
            pltpu.VMEM((V_ROWS, SUBLANES, D_MODEL), f32),
            pltpu.VMEM((KEYS, GROUP * TS), f32),
        ],
    )
    return pl.pallas_call(
        functools.partial(_layer_kernel, batch=batch),
        grid_spec=grid_spec,
        out_shape=jax.ShapeDtypeStruct(x.shape, x.dtype),
        compiler_params=pltpu.CompilerParams(
            dimension_semantics=("arbitrary",), vmem_limit_bytes=VMEM_LIMIT_BYTES),
        name="hymba_layer",
    )(sinks[0], x, rope_tbl, vecs, wq, w_all, wga, w_all, w_all, w_all, w_all, wg, woa, wo_all)
```

```python
import functools

import numpy as np
import jax
import jax.numpy as jnp
from jax import lax
from jax.experimental import pallas as pl
from jax.experimental.pallas import tpu as pltpu

D_MODEL = 1024
HEAD_DIM = 64
N_Q_HEADS = 16
N_KV_HEADS = 4
GROUP = N_Q_HEADS // N_KV_HEADS
ATTN_WIDTH = N_Q_HEADS * HEAD_DIM
KV_WIDTH = N_KV_HEADS * HEAD_DIM
WINDOW = 128
ROT_DIM = HEAD_DIM // 4
ROPE_THETA = 500000.0
NEG_INF = -1e30
LRU_WIDTH = 1024
LRU_BLOCKS = 16
LRU_BLOCK_W = LRU_WIDTH // LRU_BLOCKS
CONV_W = 4
LRU_C = 8.0
EPS = 1e-6

LANES = 128
SUBLANES = 8
MXU_DIM = 256
VMEM_LIMIT_BYTES = 58 * 1024 * 1024

TS = 64
N_SLOT = WINDOW // TS + 1
KEYS = N_SLOT * TS
LOG2E = 1.4426950408889634
Q_SCALE = HEAD_DIM ** -0.5 * LOG2E
PITCH = TS + 8
N_SLAB = LRU_WIDTH // LANES
N_GRP = LRU_WIDTH // MXU_DIM
PROJ_COLS = 512
LRU_SEGS = 4

OFF_KV, OFF_GA, OFF_XL = ATTN_WIDTH, ATTN_WIDTH + 2 * KV_WIDTH, 2 * ATTN_WIDTH + 2 * KV_WIDTH

V_LN, V_CB, V_BR, V_BI, V_LAM, V_AG, V_LG, V_FG, V_CW = 0, 1, 2, 3, 4, 5, 6, 7, 8
V_ROWS = 16


def _silu_from_half(hx):
    return hx * (jnp.tanh(hx) + 1.0)


def _rms_scale(v):
    return lax.rsqrt(jnp.mean(v * v, axis=-1, keepdims=True) + EPS)


def _layer_kernel(sinks_ref, x_ref, cos_ref, sin_hi_ref, sin_lo_ref, vecs_ref, wq_ref, wkv_ref, wga_ref,
                  wxl0_ref, wxl1_ref, wgl0_ref, wgl1_ref, wg_ref, woa_ref, wol_ref, o_ref,
                  h_scr, q_scr, km_scr, vt_scr, ga_scr, xl_scr, gl_scr, xc_scr, a_scr, u_scr,
                  hs_scr, ycat_scr, cw_scr, hst_scr, vb_scr, bias_scr, *, batch):
    f32, bf16 = jnp.float32, jnp.bfloat16
    rows = batch * TS
    step = pl.program_id(0)

    @pl.when(step == 0)
    def _init_carries():
        km_scr[...] = jnp.zeros_like(km_scr)
        vt_scr[...] = jnp.zeros_like(vt_scr)
        cw_scr[...] = jnp.zeros_like(cw_scr)
        hst_scr[...] = jnp.zeros_like(hst_scr)
        for row in range(V_ROWS):
            vb_scr[row] = jnp.broadcast_to(vecs_ref[row:row + 1, :], (SUBLANES, D_MODEL))

    def by_tile(v, op, tile):
        return op(v.reshape(-1, SUBLANES, v.shape[-1]), tile[None]).reshape(v.shape)

    def scale_by(v, row):
        return by_tile(v, jnp.multiply, vb_scr[row])

    def norm_pieces():
        def piece(b):
            xb = x_ref[b]
            hb = scale_by(xb * _rms_scale(xb), V_LN)
            h_scr[b * TS:(b + 1) * TS, :] = hb.astype(bf16)
        return [functools.partial(piece, b) for b in range(batch)]

    def proj(w_ref, c0, c1):
        return jnp.dot(h_scr[...], w_ref[:, c0:c1], preferred_element_type=f32)

    def rope(z, scale):
        cos_t, sin_hi, sin_lo = (r[...] * scale if scale != 1.0 else r[...]
                                 for r in (cos_ref, sin_hi_ref, sin_lo_ref))
        return (z * cos_t[None] + pltpu.roll(z, 8, 2) * sin_hi[None]
                + pltpu.roll(z, LANES - 8, 2) * sin_lo[None])

    def q_piece(g):
        zq = proj(wq_ref, g * MXU_DIM, (g + 1) * MXU_DIM).reshape(batch, TS, MXU_DIM)
        for sl in range(MXU_DIM // LANES):
            r = rope(zq[:, :, sl * LANES:(sl + 1) * LANES], Q_SCALE)
            c0 = g * MXU_DIM + sl * LANES
            q_scr[:, c0:c0 + LANES] = r.reshape(rows, LANES).astype(bf16)

    def kv_piece():
        slot = lax.rem(step, N_SLOT)
        slot_row = pl.multiple_of(slot * TS, TS)
        zkv = proj(wkv_ref, 0, 2 * KV_WIDTH).reshape(batch, TS, 2 * KV_WIDTH)
        lane_half = lax.broadcasted_iota(jnp.int32, (1, 1, LANES), 2) // HEAD_DIM
        for sl in range(KV_WIDTH // LANES):
            r = rope(zkv[:, :, sl * LANES:(sl + 1) * LANES], 1.0).astype(bf16)
            for hh in range(LANES // HEAD_DIM):
                km_scr[:, sl * (LANES // HEAD_DIM) + hh, pl.ds(slot_row, TS), :] = jnp.where(
                    lane_half == hh, r, jnp.zeros_like(r))
        for b in range(batch):
            vt_scr[b, slot] = zkv[b, :, KV_WIDTH:].T.astype(bf16)

    def ga_piece(c):
        ga_scr[:, c * PROJ_COLS:(c + 1) * PROJ_COLS] = proj(wga_ref, c * PROJ_COLS, (c + 1) * PROJ_COLS)

    def gl_piece(c):
        gl_scr[:, c * PROJ_COLS:(c + 1) * PROJ_COLS] = proj((wgl0_ref, wgl1_ref)[c], 0, PROJ_COLS)

    def xl_piece(c):
        zx = proj((wxl0_ref, wxl1_ref)[c], 0, PROJ_COLS)
        for sl in range(PROJ_COLS // LANES):
            for b in range(batch):
                xl_scr[c * (PROJ_COLS // LANES) + sl, b * PITCH:b * PITCH + TS, :] = (
                    zx[b * TS:(b + 1) * TS, sl * LANES:(sl + 1) * LANES])

    def attn_pieces():
        jj = lax.broadcasted_iota(jnp.int32, (TS, GROUP * TS), 0)
        ii = lax.broadcasted_iota(jnp.int32, (TS, GROUP * TS), 1) % TS
        blocks = []
        for sg in range(N_SLOT):
            age = lax.rem(step + (N_SLOT - sg), N_SLOT)
            valid = ((age == 0) & (jj <= ii)) | (age == 1) | ((age == 2) & (jj > ii))
            blocks.append(jnp.where(valid & (step >= age), 0.0, NEG_INF).astype(f32))
        bias_scr[...] = jnp.concatenate(blocks, axis=0)
        lane_g = lax.broadcasted_iota(jnp.int32, (1, GROUP * TS), 1) // TS
        sink_rows = []
        for k in range(N_KV_HEADS):
            row = jnp.zeros((1, GROUP * TS), f32)
            for g in range(GROUP):
                row = jnp.where(lane_g == g, sinks_ref[k * GROUP + g] * LOG2E, row)
            sink_rows.append(row)

        def piece(b):
            r0 = b * TS
            q_pair = [jnp.concatenate(
                [q_scr[r0:r0 + TS, g * MXU_DIM + sl * LANES:g * MXU_DIM + (sl + 1) * LANES]
                 for g in range(GROUP)], axis=0) for sl in range(KV_WIDTH // LANES)]
            o_heads = []
            for k in range(N_KV_HEADS):
                st = lax.dot_general(km_scr[b, k], q_pair[k // 2], (((1,), (1,)), ((), ())),
                                     preferred_element_type=f32)
                st = st + bias_scr[...]
                m = jnp.maximum(jnp.max(st, axis=0, keepdims=True), sink_rows[k])
                p = jnp.exp2(st - m)
                den = jnp.sum(p, axis=0, keepdims=True) + jnp.exp2(sink_rows[k] - m)
                pb = p.astype(bf16)
                o = jnp.zeros((HEAD_DIM, GROUP * TS), f32)
                for sg in range(N_SLOT):
                    o = o + jnp.dot(vt_scr[b, sg, k * HEAD_DIM:(k + 1) * HEAD_DIM, :],
                                    pb[sg * TS:(sg + 1) * TS], preferred_element_type=f32)
                o_heads.append(o * (1.0 / den))
            o_rows = jnp.concatenate(o_heads, axis=0).T
            ob = jnp.concatenate([o_rows[g * TS:(g + 1) * TS] for g in range(GROUP)], axis=1)
            v = ob * _silu_from_half(ga_scr[r0:r0 + TS, :])
            y = scale_by(v * _rms_scale(v), V_AG)
            ycat_scr[r0:r0 + TS, 0:ATTN_WIDTH] = y.astype(bf16)
        return [functools.partial(piece, b) for b in range(batch)]

    seg = TS // LRU_SEGS
    carry = {}

    def conv_piece(i):
        x1, x2, x3 = (cw_scr[0], cw_scr[1], cw_scr[2]) if i == 0 else carry["conv"]
        for t in range(i * seg, (i + 1) * seg):
            xt = jnp.concatenate(
                [xl_scr[c, pl.ds(t, batch, stride=PITCH), :] for c in range(N_SLAB)], axis=1)
            u = (vb_scr[V_CB] + vb_scr[V_CW + 3] * xt + vb_scr[V_CW + 2] * x1
                 + vb_scr[V_CW + 1] * x2 + vb_scr[V_CW] * x3)
            for j in range(N_GRP):
                xc_scr[j, t * batch:(t + 1) * batch, :] = u[:, j * MXU_DIM:(j + 1) * MXU_DIM]
            x1, x2, x3 = xt, x1, x2
        carry["conv"] = (x1, x2, x3)
        if i == LRU_SEGS - 1:
            cw_scr[0], cw_scr[1], cw_scr[2] = x1, x2, x3

    def gate_piece(j):
        cs = slice(j * MXU_DIM, (j + 1) * MXU_DIM)
        lam = vb_scr[V_LAM][:, cs]
        softplus_neg_lam = jnp.maximum(-lam, 0.0) + jnp.log1p(jnp.exp(-jnp.abs(lam)))
        half_scale = (-0.5 * LRU_C) * softplus_neg_lam
        xg = xc_scr[j]
        pre = jnp.dot(xg.astype(bf16), wg_ref[j], preferred_element_type=f32)
        t_r = jnp.tanh(by_tile(pre[:, :MXU_DIM], jnp.add, vb_scr[V_BR][:, cs]))
        t_i = jnp.tanh(by_tile(pre[:, MXU_DIM:], jnp.add, vb_scr[V_BI][:, cs]))
        log_a = by_tile(by_tile(t_r, jnp.multiply, half_scale), jnp.add, half_scale)
        a_scr[j] = jnp.exp(log_a)
        th = jnp.tanh(log_a)
        u_scr[j] = (jnp.sqrt(-0.5 * th) * lax.rsqrt(1.0 - th)) * ((t_i + 1.0) * xg)

    def scan_piece(i):
        h = hst_scr[...] if i == 0 else carry["h"]
        for t in range(i * seg, (i + 1) * seg):
            rs = slice(t * batch, (t + 1) * batch)
            at = jnp.concatenate([a_scr[j, rs, :] for j in range(N_GRP)], axis=1)
            ut = jnp.concatenate([u_scr[j, rs, :] for j in range(N_GRP)], axis=1)
            h = at * h + ut
            for c in range(N_SLAB):
                hs_scr[c, pl.ds(t, batch, stride=PITCH), :] = h[:, c * LANES:(c + 1) * LANES]
        carry["h"] = h
        if i == LRU_SEGS - 1:
            hst_scr[...] = h

    def lru_out_piece(b):
        r0, p0 = b * TS, b * PITCH
        hb = jnp.concatenate([hs_scr[c2, p0:p0 + TS, :] for c2 in range(N_SLAB)], axis=1)
        v = hb * _silu_from_half(gl_scr[r0:r0 + TS, :])
        y = scale_by(v * _rms_scale(v), V_LG)
        ycat_scr[r0:r0 + TS, ATTN_WIDTH:] = y.astype(bf16)

    def outproj_lru_piece(c):
        cs = slice(c * MXU_DIM, (c + 1) * MXU_DIM)
        gl_scr[:, cs] = jnp.dot(ycat_scr[:, ATTN_WIDTH:], wol_ref[:, cs], preferred_element_type=f32)

    def outproj_attn_piece(c):
        cs = slice(c * MXU_DIM, (c + 1) * MXU_DIM)
        gl_scr[:, cs] = gl_scr[:, cs] + jnp.dot(
            ycat_scr[:, :ATTN_WIDTH], woa_ref[:, cs], preferred_element_type=f32)

    def final_piece(b):
        x2 = x_ref[b] + gl_scr[b * TS:(b + 1) * TS, :]
        o_ref[b] = scale_by(x2 * _rms_scale(x2), V_FG)

    def emit(*phases):
        order = sorted(((i + 1) / len(ph), n, i) for n, ph in enumerate(phases) for i in range(len(ph)))
        for _, n, i in order:
            phases[n][i]()

    def pieces(fn, n):
        return [functools.partial(fn, i) for i in range(n)]

    n_half = LRU_WIDTH // PROJ_COLS
    n_out = D_MODEL // MXU_DIM
    attn = attn_pieces()
    emit(norm_pieces())
    emit(pieces(xl_piece, n_half))
    emit(pieces(conv_piece, LRU_SEGS), pieces(q_piece, GROUP)[:2])
    emit(pieces(gate_piece, N_GRP), pieces(q_piece, GROUP)[2:] + [kv_piece, functools.partial(ga_piece, 0)])
    emit(pieces(scan_piece, LRU_SEGS), [functools.partial(ga_piece, 1)])
    emit(attn[:batch // 2], pieces(gl_piece, n_half))
    emit(attn[batch // 2:], pieces(lru_out_piece, batch))
    emit(pieces(outproj_lru_piece, n_out))
    emit(pieces(outproj_attn_piece, n_out))
    emit(pieces(final_piece, batch))


def _rope_tables(seq):
    half = ROT_DIM // 2
    pos = jnp.arange(seq, dtype=jnp.float32)
    inv_freq = ROPE_THETA ** (-jnp.arange(0, ROT_DIM, 2, dtype=jnp.float32) / ROT_DIM)
    ang = pos[:, None] * inv_freq[None, :]
    cos, sin = lax.optimization_barrier((jnp.cos(ang), jnp.sin(ang)))
    cos, sin = (jnp.tile(t, (1, LANES // half)) for t in (cos, sin))
    d = np.arange(LANES) % HEAD_DIM
    lo = (d < half)[None, :]
    hi = ((d >= half) & (d < ROT_DIM))[None, :]
    return (jnp.where(lo | hi, cos, 1.0),
            jnp.where(hi, sin, 0.0),
            jnp.where(lo, -sin, 0.0))


def _group_major(a, axis):
    shp = a.shape
    a = a.reshape(shp[:axis] + (N_KV_HEADS, GROUP, HEAD_DIM) + shp[axis + 1:])
    return jnp.swapaxes(a, axis, axis + 1).reshape(shp)


def kernel(x, ln_gain, w_in, sinks, conv_w, conv_b, w_rgate, b_rgate, w_igate, b_igate,
           lru_lambda, attn_out_gain, lru_out_gain, w_out, final_gain):
    batch, seq, d_model = x.shape
    assert d_model == D_MODEL and seq % TS == 0 and ln_gain.shape[0] == 1 and N_SLOT == 3
    assert batch == SUBLANES
    f32, bf16 = jnp.float32, jnp.bfloat16
    rows = batch * TS

    col_scale = np.ones((1, OFF_XL + 2 * LRU_WIDTH), np.float32)
    col_scale[:, OFF_GA:OFF_XL] = 0.5
    col_scale[:, OFF_XL + LRU_WIDTH:] = 0.5
    w_all = (w_in[0] * col_scale).astype(bf16)
    wq = _group_major(w_all[:, :OFF_KV], 1)
    wga = _group_major(w_all[:, OFF_GA:OFF_XL], 1)
    wo_all = w_out[0].astype(bf16)
    woa = _group_major(wo_all[:ATTN_WIDTH], 0)

    def block_diag(wb):
        per = MXU_DIM // LRU_BLOCK_W
        wb = wb.reshape(N_GRP, per, LRU_BLOCK_W, 1, LRU_BLOCK_W)
        eye = np.eye(per, dtype=np.float32).reshape(1, per, 1, per, 1)
        return (wb * (0.5 * eye)).reshape(N_GRP, MXU_DIM, MXU_DIM)
    wg = jnp.concatenate([block_diag(w_rgate[0]), block_diag(w_igate[0])], axis=2).astype(bf16)

    vec_rows = [None] * (V_CW + CONV_W)
    vec_rows[V_LN], vec_rows[V_CB], vec_rows[V_BR] = ln_gain, conv_b, 0.5 * b_rgate
    vec_rows[V_BI], vec_rows[V_LAM], vec_rows[V_LG] = 0.5 * b_igate, lru_lambda, lru_out_gain
    vec_rows[V_AG], vec_rows[V_FG] = _group_major(attn_out_gain, 1), final_gain[None]
    vec_rows[V_CW:] = [conv_w[0][tap:tap + 1] for tap in range(CONV_W)]
    vecs = jnp.concatenate(vec_rows + [jnp.zeros((V_ROWS - len(vec_rows), D_MODEL), f32)], axis=0)

    rope_tbls = _rope_tables(seq)

    const = lambda *_: (0, 0)
    single = dict(pipeline_mode=pl.Buffered(1))

    def col_block(off):
        return pl.BlockSpec((D_MODEL, PROJ_COLS), lambda s: (0, off // PROJ_COLS), **single)
    grid_spec = pl.GridSpec(
        grid=(seq // TS,),
        in_specs=[
            pl.BlockSpec(memory_space=pltpu.SMEM),
            pl.BlockSpec((batch, TS, D_MODEL), lambda s: (0, s, 0)),
            *[pl.BlockSpec((TS, LANES), lambda s: (s, 0))] * 3,
            pl.BlockSpec((V_ROWS, D_MODEL), const, **single),
            pl.BlockSpec((D_MODEL, ATTN_WIDTH), const, **single),
            col_block(OFF_KV),
            pl.BlockSpec((D_MODEL, ATTN_WIDTH), const, **single),
            col_block(OFF_XL), col_block(OFF_XL + PROJ_COLS),
            col_block(OFF_XL + LRU_WIDTH), col_block(OFF_XL + LRU_WIDTH + PROJ_COLS),
            pl.BlockSpec((N_GRP, MXU_DIM, 2 * MXU_DIM), lambda s: (0, 0, 0), **single),
            pl.BlockSpec((ATTN_WIDTH, D_MODEL), const, **single),
            pl.BlockSpec((LRU_WIDTH, D_MODEL), lambda s: (ATTN_WIDTH // LRU_WIDTH, 0), **single),
        ],
        out_specs=pl.BlockSpec((batch, TS, D_MODEL), lambda s: (0, s, 0)),
        scratch_shapes=[
            pltpu.VMEM((rows, D_MODEL), bf16),
            pltpu.VMEM((rows, ATTN_WIDTH), bf16),
            pltpu.VMEM((batch, N_KV_HEADS, KEYS, LANES), bf16),
            pltpu.VMEM((batch, N_SLOT, KV_WIDTH, TS), bf16),
            pltpu.VMEM((rows, ATTN_WIDTH), f32),
            pltpu.VMEM((N_SLAB, batch * PITCH, LANES), f32),
            pltpu.VMEM((rows, LRU_WIDTH), f32),
            pltpu.VMEM((N_GRP, rows, MXU_DIM), f32),
            pltpu.VMEM((N_GRP, rows, MXU_DIM), f32),
            pltpu.VMEM((N_GRP, rows, MXU_DIM), f32),
            pltpu.VMEM((N_SLAB, batch * PITCH, LANES), f32),
            pltpu.VMEM((rows, 2 * ATTN_WIDTH), bf16),
            pltpu.VMEM((CONV_W - 1, batch, LRU_WIDTH), f32),
            pltpu.VMEM((batch, LRU_WIDTH), f32),
            pltpu.VMEM((V_ROWS, SUBLANES, D_MODEL), f32),
            pltpu.VMEM((KEYS, GROUP * TS), f32),
        ],
    )
    return pl.pallas_call(
        functools.partial(_layer_kernel, batch=batch),
        grid_spec=grid_spec,
        out_shape=jax.ShapeDtypeStruct(x.shape, x.dtype),
        compiler_params=pltpu.CompilerParams(
            dimension_semantics=("arbitrary",), vmem_limit_bytes=VMEM_LIMIT_BYTES),
        name="hymba_layer",
    )(sinks[0], x, *rope_tbls, vecs, wq, w_all, wga, w_all, w_all, w_all, w_all, wg, woa, wo_all)
```

```python
import functools

import numpy as np
import jax
import jax.numpy as jnp
from jax import lax
from jax.experimental import pallas as pl
from jax.experimental.pallas import tpu as pltpu

D_MODEL = 1024
HEAD_DIM = 64
N_Q_HEADS = 16
N_KV_HEADS = 4
GROUP = N_Q_HEADS // N_KV_HEADS
ATTN_WIDTH = N_Q_HEADS * HEAD_DIM
KV_WIDTH = N_KV_HEADS * HEAD_DIM
WINDOW = 128
ROT_DIM = HEAD_DIM // 4
ROPE_THETA = 500000.0
NEG_INF = -1e30
LRU_WIDTH = 1024
LRU_BLOCKS = 16
LRU_BLOCK_W = LRU_WIDTH // LRU_BLOCKS
CONV_W = 4
LRU_C = 8.0
EPS = 1e-6

LANES = 128
SUBLANES = 8
MXU_DIM = 256
VMEM_LIMIT_BYTES = 58 * 1024 * 1024

TS = 64
N_SLOT = WINDOW // TS + 1
KEYS = N_SLOT * TS
LOG2E = 1.4426950408889634
Q_SCALE = HEAD_DIM ** -0.5 * LOG2E
PITCH = TS + 8
N_SLAB = LRU_WIDTH // LANES
N_GRP = LRU_WIDTH // MXU_DIM

OFF_KV, OFF_GA, OFF_XL = ATTN_WIDTH, ATTN_WIDTH + 2 * KV_WIDTH, 2 * ATTN_WIDTH + 2 * KV_WIDTH

V_LN, V_CB, V_BR, V_BI, V_LAM, V_AG, V_LG, V_FG, V_CW = 0, 1, 2, 3, 4, 5, 6, 7, 8
V_ROWS = 16


def _silu_from_half(hx):
    return hx * (jnp.tanh(hx) + 1.0)


def _rms_scale(v):
    return lax.rsqrt(jnp.mean(v * v, axis=-1, keepdims=True) + EPS)


def _layer_kernel(sinks_ref, x_ref, rope_ref, vecs_ref, wq_ref, wkv_ref, wga_ref, wlru_ref,
                  wg_ref, woa_ref, wol_ref, o_ref,
                  h_scr, q_scr, km_scr, vt_scr, ga_scr, xl_scr, gl_scr, xc_scr, a_scr, u_scr,
                  hs_scr, ycat_scr, cw_scr, hst_scr, vb_scr, *, batch):
    f32, bf16 = jnp.float32, jnp.bfloat16
    rows = batch * TS
    step = pl.program_id(0)

    @pl.when(step == 0)
    def _init_carries():
        km_scr[...] = jnp.zeros_like(km_scr)
        vt_scr[...] = jnp.zeros_like(vt_scr)
        cw_scr[...] = jnp.zeros_like(cw_scr)
        hst_scr[...] = jnp.zeros_like(hst_scr)
        for row in range(V_ROWS):
            vb_scr[row] = jnp.broadcast_to(vecs_ref[row:row + 1, :], (SUBLANES, D_MODEL))

    def by_tile(v, op, tile):
        return op(v.reshape(-1, SUBLANES, v.shape[-1]), tile[None]).reshape(v.shape)

    def scale_by(v, row):
        return by_tile(v, jnp.multiply, vb_scr[row])

    def norm_b(b, c):
        xb = x_ref[b]
        hb = scale_by(xb * _rms_scale(xb), V_LN)
        h_scr[pl.ds(pl.multiple_of(b * TS, TS), TS), :] = hb.astype(bf16)
        return c
    lax.fori_loop(0, batch, norm_b, 0, unroll=True)

    hmat = h_scr[...]

    def proj(w_ref, c0, c1):
        return jnp.dot(hmat, w_ref[:, c0:c1], preferred_element_type=f32)

    k_tab = (rope_ref[0], rope_ref[1], rope_ref[2])
    q_tab = tuple(t * Q_SCALE for t in k_tab)

    def rope(z, tab):
        cos_t, sin_hi, sin_lo = tab
        return (z * cos_t[None] + pltpu.roll(z, 8, 2) * sin_hi[None]
                + pltpu.roll(z, LANES - 8, 2) * sin_lo[None])

    for g in range(GROUP):
        zq = proj(wq_ref, g * MXU_DIM, (g + 1) * MXU_DIM).reshape(batch, TS, MXU_DIM)
        for sl in range(MXU_DIM // LANES):
            r = rope(zq[:, :, sl * LANES:(sl + 1) * LANES], q_tab)
            c0 = g * MXU_DIM + sl * LANES
            q_scr[:, c0:c0 + LANES] = r.reshape(rows, LANES).astype(bf16)

    slot = lax.rem(step, N_SLOT)
    slot_row = pl.multiple_of(slot * TS, TS)
    zkv = proj(wkv_ref, 0, 2 * KV_WIDTH).reshape(batch, TS, 2 * KV_WIDTH)
    lane_half = lax.broadcasted_iota(jnp.int32, (1, 1, LANES), 2) // HEAD_DIM
    for sl in range(KV_WIDTH // LANES):
        r = rope(zkv[:, :, sl * LANES:(sl + 1) * LANES], k_tab).astype(bf16)
        for hh in range(LANES // HEAD_DIM):
            km_scr[:, sl * (LANES // HEAD_DIM) + hh, pl.ds(slot_row, TS), :] = jnp.where(
                lane_half == hh, r, jnp.zeros_like(r))
    for b in range(batch):
        vt_scr[b, slot] = zkv[b, :, KV_WIDTH:].T.astype(bf16)

    half = ATTN_WIDTH // 2
    for c in range(2):
        ga_scr[:, c * half:(c + 1) * half] = proj(wga_ref, c * half, (c + 1) * half)
        gl_scr[:, c * half:(c + 1) * half] = proj(wlru_ref, LRU_WIDTH + c * half, LRU_WIDTH + (c + 1) * half)
        zx = proj(wlru_ref, c * half, (c + 1) * half)
        for sl in range(half // LANES):
            for b in range(batch):
                xl_scr[c * (half // LANES) + sl, b * PITCH:b * PITCH + TS, :] = (
                    zx[b * TS:(b + 1) * TS, sl * LANES:(sl + 1) * LANES])

    jj = lax.broadcasted_iota(jnp.int32, (TS, GROUP * TS), 0)
    ii = lax.broadcasted_iota(jnp.int32, (TS, GROUP * TS), 1) % TS
    blocks = []
    for sg in range(N_SLOT):
        age = lax.rem(step + (N_SLOT - sg), N_SLOT)
        valid = ((age == 0) & (jj <= ii)) | (age == 1) | ((age == 2) & (jj > ii))
        blocks.append(jnp.where(valid & (step >= age), 0.0, NEG_INF).astype(f32))
    bias_t = jnp.concatenate(blocks, axis=0)
    lane_g = lax.broadcasted_iota(jnp.int32, (1, GROUP * TS), 1) // TS
    sink_rows = []
    for k in range(N_KV_HEADS):
        row = jnp.zeros((1, GROUP * TS), f32)
        for g in range(GROUP):
            row = jnp.where(lane_g == g, sinks_ref[k * GROUP + g] * LOG2E, row)
        sink_rows.append(row)

    def attn_b(b, c):
        r0 = pl.multiple_of(b * TS, TS)
        q_pair = [jnp.concatenate(
            [q_scr[pl.ds(r0, TS), g * MXU_DIM + sl * LANES:g * MXU_DIM + (sl + 1) * LANES]
             for g in range(GROUP)], axis=0) for sl in range(KV_WIDTH // LANES)]
        o_heads = []
        for k in range(N_KV_HEADS):
            st = lax.dot_general(km_scr[b, k], q_pair[k // 2], (((1,), (1,)), ((), ())),
                                 preferred_element_type=f32)
            st = st + bias_t
            m = jnp.maximum(jnp.max(st, axis=0, keepdims=True), sink_rows[k])
            p = jnp.exp2(st - m)
            den = jnp.sum(p, axis=0, keepdims=True) + jnp.exp2(sink_rows[k] - m)
            pb = p.astype(bf16)
            o = jnp.zeros((HEAD_DIM, GROUP * TS), f32)
            for sg in range(N_SLOT):
                o = o + jnp.dot(vt_scr[b, sg, k * HEAD_DIM:(k + 1) * HEAD_DIM, :],
                                pb[sg * TS:(sg + 1) * TS], preferred_element_type=f32)
            o_heads.append(o * (1.0 / den))
        o_rows = jnp.concatenate(o_heads, axis=0).T
        ob = jnp.concatenate([o_rows[g * TS:(g + 1) * TS] for g in range(GROUP)], axis=1)
        v = ob * _silu_from_half(ga_scr[pl.ds(r0, TS), :])
        y = scale_by(v * _rms_scale(v), V_AG)
        ycat_scr[pl.ds(r0, TS), 0:ATTN_WIDTH] = y.astype(bf16)
        return c
    lax.fori_loop(0, batch, attn_b, 0, unroll=8)

    def conv_t(t, carry):
        x1, x2, x3 = carry
        xt = jnp.concatenate(
            [xl_scr[c, pl.ds(t, batch, stride=PITCH), :] for c in range(N_SLAB)], axis=1)
        u = (vb_scr[V_CB] + vb_scr[V_CW + 3] * xt + vb_scr[V_CW + 2] * x1
             + vb_scr[V_CW + 1] * x2 + vb_scr[V_CW] * x3)
        t8 = pl.multiple_of(t * batch, batch)
        for j in range(N_GRP):
            xc_scr[j, pl.ds(t8, batch), :] = u[:, j * MXU_DIM:(j + 1) * MXU_DIM]
        return xt, x1, x2
    tail = lax.fori_loop(0, TS, conv_t, (cw_scr[0], cw_scr[1], cw_scr[2]), unroll=True)
    for i in range(CONV_W - 1):
        cw_scr[i] = tail[i]

    lam = vb_scr[V_LAM]
    softplus_neg_lam = jnp.maximum(-lam, 0.0) + jnp.log1p(jnp.exp(-jnp.abs(lam)))
    half_scale = (-0.5 * LRU_C) * softplus_neg_lam
    for j in range(N_GRP):
        cs = slice(j * MXU_DIM, (j + 1) * MXU_DIM)
        xg = xc_scr[j]
        pre = jnp.dot(xg.astype(bf16), wg_ref[j], preferred_element_type=f32)
        t_r = jnp.tanh(by_tile(pre[:, :MXU_DIM], jnp.add, vb_scr[V_BR][:, cs]))
        t_i = jnp.tanh(by_tile(pre[:, MXU_DIM:], jnp.add, vb_scr[V_BI][:, cs]))
        log_a = by_tile(by_tile(t_r, jnp.multiply, half_scale[:, cs]), jnp.add, half_scale[:, cs])
        a_scr[j] = jnp.exp(log_a)
        th = jnp.tanh(log_a)
        u_scr[j] = (jnp.sqrt(-0.5 * th) * lax.rsqrt(1.0 - th)) * ((t_i + 1.0) * xg)

    def scan_t(t, h):
        t8 = pl.multiple_of(t * batch, batch)
        at = jnp.concatenate([a_scr[j, pl.ds(t8, batch), :] for j in range(N_GRP)], axis=1)
        ut = jnp.concatenate([u_scr[j, pl.ds(t8, batch), :] for j in range(N_GRP)], axis=1)
        h = at * h + ut
        for c in range(N_SLAB):
            hs_scr[c, pl.ds(t, batch, stride=PITCH), :] = h[:, c * LANES:(c + 1) * LANES]
        return h
    hst_scr[...] = lax.fori_loop(0, TS, scan_t, hst_scr[...], unroll=True)

    def lru_out_b(b, c):
        r0 = pl.multiple_of(b * TS, TS)
        p0 = pl.multiple_of(b * PITCH, 8)
        hb = jnp.concatenate([hs_scr[c2, pl.ds(p0, TS), :] for c2 in range(N_SLAB)], axis=1)
        v = hb * _silu_from_half(gl_scr[pl.ds(r0, TS), :])
        y = scale_by(v * _rms_scale(v), V_LG)
        ycat_scr[pl.ds(r0, TS), ATTN_WIDTH:] = y.astype(bf16)
        return c
    lax.fori_loop(0, batch, lru_out_b, 0, unroll=True)

    ga_scr[...] = (
        jnp.dot(ycat_scr[:, :ATTN_WIDTH], woa_ref[...], preferred_element_type=f32)
        + jnp.dot(ycat_scr[:, ATTN_WIDTH:], wol_ref[...], preferred_element_type=f32))

    def out_b(b, c):
        x2 = x_ref[b] + ga_scr[pl.ds(pl.multiple_of(b * TS, TS), TS), :]
        o_ref[b] = scale_by(x2 * _rms_scale(x2), V_FG)
        return c
    lax.fori_loop(0, batch, out_b, 0, unroll=True)


def _rope_tables(seq):
    half = ROT_DIM // 2
    pos = jnp.arange(seq, dtype=jnp.float32)
    inv_freq = ROPE_THETA ** (-jnp.arange(0, ROT_DIM, 2, dtype=jnp.float32) / ROT_DIM)
    ang = pos[:, None] * jnp.tile(inv_freq, LANES // half)[None, :]
    cos, sin = jnp.cos(ang), jnp.sin(ang)
    d = np.arange(LANES) % HEAD_DIM
    lo = (d < half)[None, :]
    hi = ((d >= half) & (d < ROT_DIM))[None, :]
    return jnp.stack([jnp.where(lo | hi, cos, 1.0),
                      jnp.where(hi, sin, 0.0),
                      jnp.where(lo, -sin, 0.0)])


def _group_major(a, axis):
    shp = a.shape
    a = a.reshape(shp[:axis] + (N_KV_HEADS, GROUP, HEAD_DIM) + shp[axis + 1:])
    return jnp.swapaxes(a, axis, axis + 1).reshape(shp)


def kernel(x, ln_gain, w_in, sinks, conv_w, conv_b, w_rgate, b_rgate, w_igate, b_igate,
           lru_lambda, attn_out_gain, lru_out_gain, w_out, final_gain):
    batch, seq, d_model = x.shape
    assert d_model == D_MODEL and seq % TS == 0 and ln_gain.shape[0] == 1 and N_SLOT == 3
    assert batch == SUBLANES
    f32, bf16 = jnp.float32, jnp.bfloat16
    rows = batch * TS

    col_scale = np.ones((1, OFF_XL + 2 * LRU_WIDTH), np.float32)
    col_scale[:, OFF_GA:OFF_XL] = 0.5
    col_scale[:, OFF_XL + LRU_WIDTH:] = 0.5
    w_all = (w_in[0] * col_scale).astype(bf16)
    wo_all = w_out[0].astype(bf16)
    old_col = np.arange(ATTN_WIDTH).reshape(N_KV_HEADS, GROUP, HEAD_DIM).transpose(1, 0, 2).reshape(-1)
    perm = np.zeros((ATTN_WIDTH, ATTN_WIDTH), np.float32)
    perm[old_col, np.arange(ATTN_WIDTH)] = 1.0
    perm = jnp.asarray(perm, dtype=bf16)
    wq = jnp.dot(w_all[:, :OFF_KV], perm, preferred_element_type=f32).astype(bf16)
    wga = jnp.dot(w_all[:, OFF_GA:OFF_XL], perm, preferred_element_type=f32).astype(bf16)
    woa = jnp.dot(perm.T, wo_all[:ATTN_WIDTH], preferred_element_type=f32).astype(bf16)
    wkv, wlru, wol = w_all[:, OFF_KV:OFF_GA], w_all[:, OFF_XL:], wo_all[ATTN_WIDTH:]

    def block_diag(wb):
        per = MXU_DIM // LRU_BLOCK_W
        wb = wb.reshape(N_GRP, per, LRU_BLOCK_W, 1, LRU_BLOCK_W)
        eye = np.eye(per, dtype=np.float32).reshape(1, per, 1, per, 1)
        return (wb * (0.5 * eye)).reshape(N_GRP, MXU_DIM, MXU_DIM)
    wg = jnp.concatenate([block_diag(w_rgate[0]), block_diag(w_igate[0])], axis=2).astype(bf16)

    vec_rows = [None] * (V_CW + CONV_W)
    vec_rows[V_LN], vec_rows[V_CB], vec_rows[V_BR] = ln_gain, conv_b, 0.5 * b_rgate
    vec_rows[V_BI], vec_rows[V_LAM], vec_rows[V_LG] = 0.5 * b_igate, lru_lambda, lru_out_gain
    vec_rows[V_AG], vec_rows[V_FG] = _group_major(attn_out_gain, 1), final_gain[None]
    vec_rows[V_CW:] = [conv_w[0][tap:tap + 1] for tap in range(CONV_W)]
    vecs = jnp.concatenate(vec_rows + [jnp.zeros((V_ROWS - len(vec_rows), D_MODEL), f32)], axis=0)

    rope_tbl = _rope_tables(seq)

    const = lambda *_: (0, 0)
    single = dict(pipeline_mode=pl.Buffered(1))
    grid_spec = pl.GridSpec(
        grid=(seq // TS,),
        in_specs=[
            pl.BlockSpec(memory_space=pltpu.SMEM),
            pl.BlockSpec((batch, TS, D_MODEL), lambda s: (0, s, 0)),
            pl.BlockSpec((3, TS, LANES), lambda s: (0, s, 0)),
            pl.BlockSpec((V_ROWS, D_MODEL), const, **single),
            pl.BlockSpec((D_MODEL, ATTN_WIDTH), const, **single),
            pl.BlockSpec((D_MODEL, 2 * KV_WIDTH), const, **single),
            pl.BlockSpec((D_MODEL, ATTN_WIDTH), const, **single),
            pl.BlockSpec((D_MODEL, 2 * LRU_WIDTH), const, **single),
            pl.BlockSpec((N_GRP, MXU_DIM, 2 * MXU_DIM), lambda s: (0, 0, 0), **single),
            pl.BlockSpec((ATTN_WIDTH, D_MODEL), const, **single),
            pl.BlockSpec((LRU_WIDTH, D_MODEL), const, **single),
        ],
        out_specs=pl.BlockSpec((batch, TS, D_MODEL), lambda s: (0, s, 0)),
        scratch_shapes=[
            pltpu.VMEM((rows, D_MODEL), bf16),
            pltpu.VMEM((rows, ATTN_WIDTH), bf16),
            pltpu.VMEM((batch, N_KV_HEADS, KEYS, LANES), bf16),
            pltpu.VMEM((batch, N_SLOT, KV_WIDTH, TS), bf16),
            pltpu.VMEM((rows, ATTN_WIDTH), f32),
            pltpu.VMEM((N_SLAB, batch * PITCH, LANES), f32),
            pltpu.VMEM((rows, LRU_WIDTH), f32),
            pltpu.VMEM((N_GRP, rows, MXU_DIM), f32),
            pltpu.VMEM((N_GRP, rows, MXU_DIM), f32),
            pltpu.VMEM((N_GRP, rows, MXU_DIM), f32),
            pltpu.VMEM((N_SLAB, batch * PITCH, LANES), f32),
            pltpu.VMEM((rows, 2 * ATTN_WIDTH), bf16),
            pltpu.VMEM((CONV_W - 1, batch, LRU_WIDTH), f32),
            pltpu.VMEM((batch, LRU_WIDTH), f32),
            pltpu.VMEM((V_ROWS, SUBLANES, D_MODEL), f32),
        ],
    )
    return pl.pallas_call(
        functools.partial(_layer_kernel, batch=batch),
        grid_spec=grid_spec,
        out_shape=jax.ShapeDtypeStruct(x.shape, x.dtype),
        compiler_params=pltpu.CompilerParams(
            dimension_semantics=("arbitrary",), vmem_limit_bytes=VMEM_LIMIT_BYTES),
        name="hymba_layer",
    )(sinks[0], x, rope_tbl, vecs, wq, wkv, wga, wlru, wg, woa, wol)
```

```python
import functools

import numpy as np
import jax
import jax.numpy as jnp
from jax import lax
from jax.experimental import pallas as pl
from jax.experimental.pallas import tpu as pltpu

D_MODEL = 1024
HEAD_DIM = 64
N_Q_HEADS = 16
N_KV_HEADS = 4
GROUP = N_Q_HEADS // N_KV_HEADS
ATTN_WIDTH = N_Q_HEADS * HEAD_DIM
KV_WIDTH = N_KV_HEADS * HEAD_DIM
WINDOW = 128
ROT_DIM = HEAD_DIM // 4
ROPE_THETA = 500000.0
NEG_INF = -1e30
LRU_WIDTH = 1024
LRU_BLOCKS = 16
LRU_BLOCK_W = LRU_WIDTH // LRU_BLOCKS
CONV_W = 4
LRU_C = 8.0
EPS = 1e-6

LANES = 128
SUBLANES = 8
MXU_DIM = 256
VMEM_LIMIT_BYTES = 58 * 1024 * 1024

TS = 64
CHUNKS_PER_STEP = 2
N_SLOT = WINDOW // TS + 1
KEYS = N_SLOT * TS
LOG2E = 1.4426950408889634
Q_SCALE = HEAD_DIM ** -0.5 * LOG2E
PITCH = TS + 8
N_SLAB = LRU_WIDTH // LANES
N_GRP = LRU_WIDTH // MXU_DIM

OFF_KV, OFF_GA, OFF_XL = ATTN_WIDTH, ATTN_WIDTH + 2 * KV_WIDTH, 2 * ATTN_WIDTH + 2 * KV_WIDTH

V_LN, V_CB, V_BR, V_BI, V_LAM, V_AG, V_LG, V_FG, V_CW = 0, 1, 2, 3, 4, 5, 6, 7, 8
V_ROWS = 16


def _silu_from_half(hx):
    return hx * (jnp.tanh(hx) + 1.0)


def _rms_scale(v):
    return lax.rsqrt(jnp.mean(v * v, axis=-1, keepdims=True) + EPS)


def _chunk_body(ci, sinks_ref, x_ref, rope_ref, vecs_ref, wq_ref, wkv_ref, wga_ref, wlru_ref,
                  wg_ref, woa_ref, wol_ref, o_ref,
                  h_scr, q_scr, km_scr, vt_scr, ga_scr, xl_scr, gl_scr, xc_scr, a_scr, u_scr,
                  hs_scr, ycat_scr, cw_scr, hst_scr, vb_scr, *, batch):
    f32, bf16 = jnp.float32, jnp.bfloat16
    rows = batch * TS
    chunk = pl.program_id(0) * CHUNKS_PER_STEP + ci
    pos = slice(ci * TS, (ci + 1) * TS)

    def _init_carries():
        km_scr[...] = jnp.zeros_like(km_scr)
        vt_scr[...] = jnp.zeros_like(vt_scr)
        cw_scr[...] = jnp.zeros_like(cw_scr)
        hst_scr[...] = jnp.zeros_like(hst_scr)
        for row in range(V_ROWS):
            vb_scr[row] = jnp.broadcast_to(vecs_ref[row:row + 1, :], (SUBLANES, D_MODEL))
    if ci == 0:
        pl.when(chunk == 0)(_init_carries)

    def by_tile(v, op, tile):
        return op(v.reshape(-1, SUBLANES, v.shape[-1]), tile[None]).reshape(v.shape)

    def scale_by(v, row):
        return by_tile(v, jnp.multiply, vb_scr[row])

    def norm_b(b, c):
        xb = x_ref[b, pos, :]
        hb = scale_by(xb * _rms_scale(xb), V_LN)
        h_scr[pl.ds(pl.multiple_of(b * TS, TS), TS), :] = hb.astype(bf16)
        return c
    lax.fori_loop(0, batch, norm_b, 0, unroll=True)

    hmat = h_scr[...]

    def proj(w_ref, c0, c1):
        return jnp.dot(hmat, w_ref[:, c0:c1], preferred_element_type=f32)

    k_tab = (rope_ref[0, pos, :], rope_ref[1, pos, :], rope_ref[2, pos, :])
    q_tab = tuple(t * Q_SCALE for t in k_tab)

    def rope(z, tab):
        cos_t, sin_hi, sin_lo = tab
        return (z * cos_t[None] + pltpu.roll(z, 8, 2) * sin_hi[None]
                + pltpu.roll(z, LANES - 8, 2) * sin_lo[None])

    for g in range(GROUP):
        zq = proj(wq_ref, g * MXU_DIM, (g + 1) * MXU_DIM).reshape(batch, TS, MXU_DIM)
        for sl in range(MXU_DIM // LANES):
            r = rope(zq[:, :, sl * LANES:(sl + 1) * LANES], q_tab)
            c0 = g * MXU_DIM + sl * LANES
            q_scr[:, c0:c0 + LANES] = r.reshape(rows, LANES).astype(bf16)

    slot = lax.rem(chunk, N_SLOT)
    slot_row = pl.multiple_of(slot * TS, TS)
    zkv = proj(wkv_ref, 0, 2 * KV_WIDTH).reshape(batch, TS, 2 * KV_WIDTH)
    lane_half = lax.broadcasted_iota(jnp.int32, (1, 1, LANES), 2) // HEAD_DIM
    for sl in range(KV_WIDTH // LANES):
        r = rope(zkv[:, :, sl * LANES:(sl + 1) * LANES], k_tab).astype(bf16)
        for hh in range(LANES // HEAD_DIM):
            km_scr[:, sl * (LANES // HEAD_DIM) + hh, pl.ds(slot_row, TS), :] = jnp.where(
                lane_half == hh, r, jnp.zeros_like(r))
    for b in range(batch):
        vt_scr[b, slot] = zkv[b, :, KV_WIDTH:].T.astype(bf16)

    half = ATTN_WIDTH // 2
    for c in range(2):
        ga_scr[:, c * half:(c + 1) * half] = proj(wga_ref, c * half, (c + 1) * half)
        gl_scr[:, c * half:(c + 1) * half] = proj(wlru_ref, LRU_WIDTH + c * half, LRU_WIDTH + (c + 1) * half)
        zx = proj(wlru_ref, c * half, (c + 1) * half)
        for sl in range(half // LANES):
            for b in range(batch):
                xl_scr[c * (half // LANES) + sl, b * PITCH:b * PITCH + TS, :] = (
                    zx[b * TS:(b + 1) * TS, sl * LANES:(sl + 1) * LANES])

    jj = lax.broadcasted_iota(jnp.int32, (TS, GROUP * TS), 0)
    ii = lax.broadcasted_iota(jnp.int32, (TS, GROUP * TS), 1) % TS
    blocks = []
    for sg in range(N_SLOT):
        age = lax.rem(chunk + (N_SLOT - sg), N_SLOT)
        valid = ((age == 0) & (jj <= ii)) | (age == 1) | ((age == 2) & (jj > ii))
        blocks.append(jnp.where(valid & (chunk >= age), 0.0, NEG_INF).astype(f32))
    bias_t = jnp.concatenate(blocks, axis=0)
    lane_g = lax.broadcasted_iota(jnp.int32, (1, GROUP * TS), 1) // TS
    sink_rows = []
    for k in range(N_KV_HEADS):
        row = jnp.zeros((1, GROUP * TS), f32)
        for g in range(GROUP):
            row = jnp.where(lane_g == g, sinks_ref[k * GROUP + g] * LOG2E, row)
        sink_rows.append(row)

    def attn_b(b, c):
        r0 = pl.multiple_of(b * TS, TS)
        q_pair = [jnp.concatenate(
            [q_scr[pl.ds(r0, TS), g * MXU_DIM + sl * LANES:g * MXU_DIM + (sl + 1) * LANES]
             for g in range(GROUP)], axis=0) for sl in range(KV_WIDTH // LANES)]
        o_heads = []
        for k in range(N_KV_HEADS):
            st = lax.dot_general(km_scr[b, k], q_pair[k // 2], (((1,), (1,)), ((), ())),
                                 preferred_element_type=f32)
            st = st + bias_t
            m = jnp.maximum(jnp.max(st, axis=0, keepdims=True), sink_rows[k])
            p = jnp.exp2(st - m)
            den = jnp.sum(p, axis=0, keepdims=True) + jnp.exp2(sink_rows[k] - m)
            pb = p.astype(bf16)
            o = jnp.zeros((HEAD_DIM, GROUP * TS), f32)
            for sg in range(N_SLOT):
                o = o + jnp.dot(vt_scr[b, sg, k * HEAD_DIM:(k + 1) * HEAD_DIM, :],
                                pb[sg * TS:(sg + 1) * TS], preferred_element_type=f32)
            o_heads.append(o * (1.0 / den))
        o_rows = jnp.concatenate(o_heads, axis=0).T
        ob = jnp.concatenate([o_rows[g * TS:(g + 1) * TS] for g in range(GROUP)], axis=1)
        v = ob * _silu_from_half(ga_scr[pl.ds(r0, TS), :])
        y = scale_by(v * _rms_scale(v), V_AG)
        ycat_scr[pl.ds(r0, TS), 0:ATTN_WIDTH] = y.astype(bf16)
        return c
    lax.fori_loop(0, batch, attn_b, 0, unroll=8)

    def conv_t(t, carry):
        x1, x2, x3 = carry
        xt = jnp.concatenate(
            [xl_scr[c, pl.ds(t, batch, stride=PITCH), :] for c in range(N_SLAB)], axis=1)
        u = (vb_scr[V_CB] + vb_scr[V_CW + 3] * xt + vb_scr[V_CW + 2] * x1
             + vb_scr[V_CW + 1] * x2 + vb_scr[V_CW] * x3)
        t8 = pl.multiple_of(t * batch, batch)
        for j in range(N_GRP):
            xc_scr[j, pl.ds(t8, batch), :] = u[:, j * MXU_DIM:(j + 1) * MXU_DIM]
        return xt, x1, x2
    tail = lax.fori_loop(0, TS, conv_t, (cw_scr[0], cw_scr[1], cw_scr[2]), unroll=True)
    for i in range(CONV_W - 1):
        cw_scr[i] = tail[i]

    lam = vb_scr[V_LAM]
    softplus_neg_lam = jnp.maximum(-lam, 0.0) + jnp.log1p(jnp.exp(-jnp.abs(lam)))
    half_scale = (-0.5 * LRU_C) * softplus_neg_lam
    for j in range(N_GRP):
        cs = slice(j * MXU_DIM, (j + 1) * MXU_DIM)
        xg = xc_scr[j]
        pre = jnp.dot(xg.astype(bf16), wg_ref[j], preferred_element_type=f32)
        t_r = jnp.tanh(by_tile(pre[:, :MXU_DIM], jnp.add, vb_scr[V_BR][:, cs]))
        t_i = jnp.tanh(by_tile(pre[:, MXU_DIM:], jnp.add, vb_scr[V_BI][:, cs]))
        log_a = by_tile(by_tile(t_r, jnp.multiply, half_scale[:, cs]), jnp.add, half_scale[:, cs])
        a_scr[j] = jnp.exp(log_a)
        th = jnp.tanh(log_a)
        u_scr[j] = (jnp.sqrt(-0.5 * th) * lax.rsqrt(1.0 - th)) * ((t_i + 1.0) * xg)

    def scan_t(t, h):
        t8 = pl.multiple_of(t * batch, batch)
        at = jnp.concatenate([a_scr[j, pl.ds(t8, batch), :] for j in range(N_GRP)], axis=1)
        ut = jnp.concatenate([u_scr[j, pl.ds(t8, batch), :] for j in range(N_GRP)], axis=1)
        h = at * h + ut
        for c in range(N_SLAB):
            hs_scr[c, pl.ds(t, batch, stride=PITCH), :] = h[:, c * LANES:(c + 1) * LANES]
        return h
    hst_scr[...] = lax.fori_loop(0, TS, scan_t, hst_scr[...], unroll=True)

    def lru_out_b(b, c):
        r0 = pl.multiple_of(b * TS, TS)
        p0 = pl.multiple_of(b * PITCH, 8)
        hb = jnp.concatenate([hs_scr[c2, pl.ds(p0, TS), :] for c2 in range(N_SLAB)], axis=1)
        v = hb * _silu_from_half(gl_scr[pl.ds(r0, TS), :])
        y = scale_by(v * _rms_scale(v), V_LG)
        ycat_scr[pl.ds(r0, TS), ATTN_WIDTH:] = y.astype(bf16)
        return c
    lax.fori_loop(0, batch, lru_out_b, 0, unroll=True)

    ga_scr[...] = (
        jnp.dot(ycat_scr[:, :ATTN_WIDTH], woa_ref[...], preferred_element_type=f32)
        + jnp.dot(ycat_scr[:, ATTN_WIDTH:], wol_ref[...], preferred_element_type=f32))

    def out_b(b, c):
        x2 = x_ref[b, pos, :] + ga_scr[pl.ds(pl.multiple_of(b * TS, TS), TS), :]
        o_ref[b, pos, :] = scale_by(x2 * _rms_scale(x2), V_FG)
        return c
    lax.fori_loop(0, batch, out_b, 0, unroll=True)


def _layer_kernel(*refs, batch):
    for ci in range(CHUNKS_PER_STEP):
        _chunk_body(ci, *refs, batch=batch)


def _rope_tables(seq):
    half = ROT_DIM // 2
    pos = jnp.arange(seq, dtype=jnp.float32)
    inv_freq = ROPE_THETA ** (-jnp.arange(0, ROT_DIM, 2, dtype=jnp.float32) / ROT_DIM)
    ang = pos[:, None] * jnp.tile(inv_freq, LANES // half)[None, :]
    cos, sin = jnp.cos(ang), jnp.sin(ang)
    d = np.arange(LANES) % HEAD_DIM
    lo = (d < half)[None, :]
    hi = ((d >= half) & (d < ROT_DIM))[None, :]
    return jnp.stack([jnp.where(lo | hi, cos, 1.0),
                      jnp.where(hi, sin, 0.0),
                      jnp.where(lo, -sin, 0.0)])


def _group_major(a, axis):
    shp = a.shape
    a = a.reshape(shp[:axis] + (N_KV_HEADS, GROUP, HEAD_DIM) + shp[axis + 1:])
    return jnp.swapaxes(a, axis, axis + 1).reshape(shp)


def kernel(x, ln_gain, w_in, sinks, conv_w, conv_b, w_rgate, b_rgate, w_igate, b_igate,
           lru_lambda, attn_out_gain, lru_out_gain, w_out, final_gain):
    batch, seq, d_model = x.shape
    assert d_model == D_MODEL and seq % (TS * CHUNKS_PER_STEP) == 0 and ln_gain.shape[0] == 1 and N_SLOT == 3
    assert batch == SUBLANES
    f32, bf16 = jnp.float32, jnp.bfloat16
    rows = batch * TS

    col_scale = np.ones((1, OFF_XL + 2 * LRU_WIDTH), np.float32)
    col_scale[:, OFF_GA:OFF_XL] = 0.5
    col_scale[:, OFF_XL + LRU_WIDTH:] = 0.5
    w_all = (w_in[0] * col_scale).astype(bf16)
    wo_all = w_out[0].astype(bf16)
    old_col = np.arange(ATTN_WIDTH).reshape(N_KV_HEADS, GROUP, HEAD_DIM).transpose(1, 0, 2).reshape(-1)
    perm = np.zeros((ATTN_WIDTH, ATTN_WIDTH), np.float32)
    perm[old_col, np.arange(ATTN_WIDTH)] = 1.0
    perm = jnp.asarray(perm, dtype=bf16)
    wq = jnp.dot(w_all[:, :OFF_KV], perm, preferred_element_type=f32).astype(bf16)
    wga = jnp.dot(w_all[:, OFF_GA:OFF_XL], perm, preferred_element_type=f32).astype(bf16)
    woa = jnp.dot(perm.T, wo_all[:ATTN_WIDTH], preferred_element_type=f32).astype(bf16)
    wkv, wlru, wol = w_all[:, OFF_KV:OFF_GA], w_all[:, OFF_XL:], wo_all[ATTN_WIDTH:]

    def block_diag(wb):
        per = MXU_DIM // LRU_BLOCK_W
        wb = wb.reshape(N_GRP, per, LRU_BLOCK_W, 1, LRU_BLOCK_W)
        eye = np.eye(per, dtype=np.float32).reshape(1, per, 1, per, 1)
        return (wb * (0.5 * eye)).reshape(N_GRP, MXU_DIM, MXU_DIM)
    wg = jnp.concatenate([block_diag(w_rgate[0]), block_diag(w_igate[0])], axis=2).astype(bf16)

    vec_rows = [None] * (V_CW + CONV_W)
    vec_rows[V_LN], vec_rows[V_CB], vec_rows[V_BR] = ln_gain, conv_b, 0.5 * b_rgate
    vec_rows[V_BI], vec_rows[V_LAM], vec_rows[V_LG] = 0.5 * b_igate, lru_lambda, lru_out_gain
    vec_rows[V_AG], vec_rows[V_FG] = _group_major(attn_out_gain, 1), final_gain[None]
    vec_rows[V_CW:] = [conv_w[0][tap:tap + 1] for tap in range(CONV_W)]
    vecs = jnp.concatenate(vec_rows + [jnp.zeros((V_ROWS - len(vec_rows), D_MODEL), f32)], axis=0)

    rope_tbl = _rope_tables(seq)

    const = lambda *_: (0, 0)
    single = dict(pipeline_mode=pl.Buffered(1))
    grid_spec = pl.GridSpec(
        grid=(seq // (TS * CHUNKS_PER_STEP),),
        in_specs=[
            pl.BlockSpec(memory_space=pltpu.SMEM),
            pl.BlockSpec((batch, TS * CHUNKS_PER_STEP, D_MODEL), lambda s: (0, s, 0)),
            pl.BlockSpec((3, TS * CHUNKS_PER_STEP, LANES), lambda s: (0, s, 0)),
            pl.BlockSpec((V_ROWS, D_MODEL), const, **single),
            pl.BlockSpec((D_MODEL, ATTN_WIDTH), const, **single),
            pl.BlockSpec((D_MODEL, 2 * KV_WIDTH), const, **single),
            pl.BlockSpec((D_MODEL, ATTN_WIDTH), const, **single),
            pl.BlockSpec((D_MODEL, 2 * LRU_WIDTH), const, **single),
            pl.BlockSpec((N_GRP, MXU_DIM, 2 * MXU_DIM), lambda s: (0, 0, 0), **single),
            pl.BlockSpec((ATTN_WIDTH, D_MODEL), const, **single),
            pl.BlockSpec((LRU_WIDTH, D_MODEL), const, **single),
        ],
        out_specs=pl.BlockSpec((batch, TS * CHUNKS_PER_STEP, D_MODEL), lambda s: (0, s, 0)),
        scratch_shapes=[
            pltpu.VMEM((rows, D_MODEL), bf16),
            pltpu.VMEM((rows, ATTN_WIDTH), bf16),
            pltpu.VMEM((batch, N_KV_HEADS, KEYS, LANES), bf16),
            pltpu.VMEM((batch, N_SLOT, KV_WIDTH, TS), bf16),
            pltpu.VMEM((rows, ATTN_WIDTH), f32),
            pltpu.VMEM((N_SLAB, batch * PITCH, LANES), f32),
            pltpu.VMEM((rows, LRU_WIDTH), f32),
            pltpu.VMEM((N_GRP, rows, MXU_DIM), f32),
            pltpu.VMEM((N_GRP, rows, MXU_DIM), f32),
            pltpu.VMEM((N_GRP, rows, MXU_DIM), f32),
            pltpu.VMEM((N_SLAB, batch * PITCH, LANES), f32),
            pltpu.VMEM((rows, 2 * ATTN_WIDTH), bf16),
            pltpu.VMEM((CONV_W - 1, batch, LRU_WIDTH), f32),
            pltpu.VMEM((batch, LRU_WIDTH), f32),
            pltpu.VMEM((V_ROWS, SUBLANES, D_MODEL), f32),
        ],
    )
    return pl.pallas_call(
        functools.partial(_layer_kernel, batch=batch),
        grid_spec=grid_spec,
        out_shape=jax.ShapeDtypeStruct(x.shape, x.dtype),
        compiler_params=pltpu.CompilerParams(
            dimension_semantics=("arbitrary",), vmem_limit_bytes=VMEM_LIMIT_BYTES),
        name="hymba_layer",
    )(sinks[0], x, rope_tbl, vecs, wq, wkv, wga, wlru, wg, woa, wol)
```

```python
import functools

import numpy as np
import jax
import jax.numpy as jnp
from jax import lax
from jax.experimental import pallas as pl
from jax.experimental.pallas import tpu as pltpu

D_MODEL = 1024
HEAD_DIM = 64
N_Q_HEADS = 16
N_KV_HEADS = 4
GROUP = N_Q_HEADS // N_KV_HEADS
ATTN_WIDTH = N_Q_HEADS * HEAD_DIM
KV_WIDTH = N_KV_HEADS * HEAD_DIM
WINDOW = 128
ROT_DIM = HEAD_DIM // 4
ROPE_THETA = 500000.0
NEG_INF = -1e30
LRU_WIDTH = 1024
LRU_BLOCKS = 16
LRU_BLOCK_W = LRU_WIDTH // LRU_BLOCKS
CONV_W = 4
LRU_C = 8.0
EPS = 1e-6

LANES = 128
SUBLANES = 8
MXU_DIM = 256
VMEM_LIMIT_BYTES = 58 * 1024 * 1024

TS = 64
N_SLOT = WINDOW // TS + 1
KEYS = N_SLOT * TS
LOG2E = 1.4426950408889634
Q_SCALE = HEAD_DIM ** -0.5 * LOG2E
PITCH = TS + 8
N_SLAB = LRU_WIDTH // LANES
N_GRP = LRU_WIDTH // MXU_DIM

OFF_KV, OFF_GA, OFF_XL = ATTN_WIDTH, ATTN_WIDTH + 2 * KV_WIDTH, 2 * ATTN_WIDTH + 2 * KV_WIDTH

V_LN, V_CB, V_BR, V_BI, V_LAM, V_AG, V_LG, V_FG, V_CW = 0, 1, 2, 3, 4, 5, 6, 7, 8
V_ROWS = 16


def _silu_from_half(hx):
    return hx * (jnp.tanh(hx) + 1.0)


def _rms_scale(v):
    return lax.rsqrt(jnp.mean(v * v, axis=-1, keepdims=True) + EPS)


def _layer_kernel(sinks_ref, x_ref, rope_ref, vecs_ref, wq_ref, wkv_ref, wga_ref, wlru_ref,
                  wg_ref, woa_ref, wol_ref, o_ref,
                  h_scr, q_scr, km_scr, vt_scr, ga_scr, xl_scr, gl_scr, xc_scr, a_scr, u_scr,
                  hs_scr, ycat_scr, cw_scr, hst_scr, vb_scr, *, batch):
    f32, bf16 = jnp.float32, jnp.bfloat16
    rows = batch * TS
    step = pl.program_id(0)

    @pl.when(step == 0)
    def _init_carries():
        km_scr[...] = jnp.zeros_like(km_scr)
        vt_scr[...] = jnp.zeros_like(vt_scr)
        cw_scr[...] = jnp.zeros_like(cw_scr)
        hst_scr[...] = jnp.zeros_like(hst_scr)
        for row in range(V_ROWS):
            vb_scr[row] = jnp.broadcast_to(vecs_ref[row:row + 1, :], (SUBLANES, D_MODEL))

    def by_tile(v, op, tile):
        return op(v.reshape(-1, SUBLANES, v.shape[-1]), tile[None]).reshape(v.shape)

    def scale_by(v, row):
        return by_tile(v, jnp.multiply, vb_scr[row])

    def norm_b(b, c):
        xb = x_ref[b]
        hb = scale_by(xb * _rms_scale(xb), V_LN)
        h_scr[pl.ds(pl.multiple_of(b * TS, TS), TS), :] = hb.astype(bf16)
        return c
    lax.fori_loop(0, batch, norm_b, 0, unroll=True)

    hmat = h_scr[...]

    def proj(w_ref, c0, c1):
        return jnp.dot(hmat, w_ref[:, c0:c1], preferred_element_type=f32)

    k_tab = (rope_ref[0], rope_ref[1], rope_ref[2])
    q_tab = tuple(t * Q_SCALE for t in k_tab)

    def rope(z, tab):
        cos_t, sin_hi, sin_lo = tab
        return (z * cos_t[None] + pltpu.roll(z, 8, 2) * sin_hi[None]
                + pltpu.roll(z, LANES - 8, 2) * sin_lo[None])

    for g in range(GROUP):
        zq = proj(wq_ref, g * MXU_DIM, (g + 1) * MXU_DIM).reshape(batch, TS, MXU_DIM)
        for sl in range(MXU_DIM // LANES):
            r = rope(zq[:, :, sl * LANES:(sl + 1) * LANES], q_tab)
            c0 = g * MXU_DIM + sl * LANES
            q_scr[:, c0:c0 + LANES] = r.reshape(rows, LANES).astype(bf16)

    slot = lax.rem(step, N_SLOT)
    slot_row = pl.multiple_of(slot * TS, TS)
    zkv = proj(wkv_ref, 0, 2 * KV_WIDTH).reshape(batch, TS, 2 * KV_WIDTH)
    lane_half = lax.broadcasted_iota(jnp.int32, (1, 1, LANES), 2) // HEAD_DIM
    for sl in range(KV_WIDTH // LANES):
        r = rope(zkv[:, :, sl * LANES:(sl + 1) * LANES], k_tab).astype(bf16)
        for hh in range(LANES // HEAD_DIM):
            km_scr[:, sl * (LANES // HEAD_DIM) + hh, pl.ds(slot_row, TS), :] = jnp.where(
                lane_half == hh, r, jnp.zeros_like(r))
    for b in range(batch):
        vt_scr[b, slot] = zkv[b, :, KV_WIDTH:].T.astype(bf16)

    half = ATTN_WIDTH // 2
    for c in range(2):
        ga_scr[:, c * half:(c + 1) * half] = proj(wga_ref, c * half, (c + 1) * half)
        gl_scr[:, c * half:(c + 1) * half] = proj(wlru_ref, LRU_WIDTH + c * half, LRU_WIDTH + (c + 1) * half)
        zx = proj(wlru_ref, c * half, (c + 1) * half)
        for sl in range(half // LANES):
            for b in range(batch):
                xl_scr[c * (half // LANES) + sl, b * PITCH:b * PITCH + TS, :] = (
                    zx[b * TS:(b + 1) * TS, sl * LANES:(sl + 1) * LANES])

    jj = lax.broadcasted_iota(jnp.int32, (TS, GROUP * TS), 0)
    ii = lax.broadcasted_iota(jnp.int32, (TS, GROUP * TS), 1) % TS
    blocks = []
    for sg in range(N_SLOT):
        age = lax.rem(step + (N_SLOT - sg), N_SLOT)
        valid = ((age == 0) & (jj <= ii)) | (age == 1) | ((age == 2) & (jj > ii))
        blocks.append(jnp.where(valid & (step >= age), 0.0, NEG_INF).astype(f32))
    bias_t = jnp.concatenate(blocks, axis=0)
    lane_g = lax.broadcasted_iota(jnp.int32, (1, GROUP * TS), 1) // TS
    sink_rows = []
    for k in range(N_KV_HEADS):
        row = jnp.zeros((1, GROUP * TS), f32)
        for g in range(GROUP):
            row = jnp.where(lane_g == g, sinks_ref[k * GROUP + g] * LOG2E, row)
        sink_rows.append(row)

    def attn_b(b, c):
        r0 = pl.multiple_of(b * TS, TS)
        q_pair = [jnp.concatenate(
            [q_scr[pl.ds(r0, TS), g * MXU_DIM + sl * LANES:g * MXU_DIM + (sl + 1) * LANES]
             for g in range(GROUP)], axis=0) for sl in range(KV_WIDTH // LANES)]
        o_heads = []
        for k in range(N_KV_HEADS):
            st = lax.dot_general(km_scr[b, k], q_pair[k // 2], (((1,), (1,)), ((), ())),
                                 preferred_element_type=f32)
            st = st + bias_t
            m = jnp.maximum(jnp.max(st, axis=0, keepdims=True), sink_rows[k])
            p = jnp.exp2(st - m)
            den = jnp.sum(p, axis=0, keepdims=True) + jnp.exp2(sink_rows[k] - m)
            pb = p.astype(bf16)
            o = jnp.zeros((HEAD_DIM, GROUP * TS), f32)
            for sg in range(N_SLOT):
                o = o + jnp.dot(vt_scr[b, sg, k * HEAD_DIM:(k + 1) * HEAD_DIM, :],
                                pb[sg * TS:(sg + 1) * TS], preferred_element_type=f32)
            o_heads.append(o * (1.0 / den))
        o_rows = jnp.concatenate(o_heads, axis=0).T
        ob = jnp.concatenate([o_rows[g * TS:(g + 1) * TS] for g in range(GROUP)], axis=1)
        v = ob * _silu_from_half(ga_scr[pl.ds(r0, TS), :])
        y = scale_by(v * _rms_scale(v), V_AG)
        ycat_scr[pl.ds(r0, TS), 0:ATTN_WIDTH] = y.astype(bf16)
        return c
    lax.fori_loop(0, batch, attn_b, 0, unroll=8)

    def conv_t(t, carry):
        x1, x2, x3 = carry
        xt = jnp.concatenate(
            [xl_scr[c, pl.ds(t, batch, stride=PITCH), :] for c in range(N_SLAB)], axis=1)
        u = (vb_scr[V_CB] + vb_scr[V_CW + 3] * xt + vb_scr[V_CW + 2] * x1
             + vb_scr[V_CW + 1] * x2 + vb_scr[V_CW] * x3)
        t8 = pl.multiple_of(t * batch, batch)
        for j in range(N_GRP):
            xc_scr[j, pl.ds(t8, batch), :] = u[:, j * MXU_DIM:(j + 1) * MXU_DIM]
        return xt, x1, x2
    tail = lax.fori_loop(0, TS, conv_t, (cw_scr[0], cw_scr[1], cw_scr[2]), unroll=True)
    for i in range(CONV_W - 1):
        cw_scr[i] = tail[i]

    lam = vb_scr[V_LAM]
    softplus_neg_lam = jnp.maximum(-lam, 0.0) + jnp.log1p(jnp.exp(-jnp.abs(lam)))
    half_scale = (-0.5 * LRU_C) * softplus_neg_lam
    for j in range(N_GRP):
        cs = slice(j * MXU_DIM, (j + 1) * MXU_DIM)
        xg = xc_scr[j]
        pre = jnp.dot(xg.astype(bf16), wg_ref[j], preferred_element_type=f32)
        t_r = jnp.tanh(by_tile(pre[:, :MXU_DIM], jnp.add, vb_scr[V_BR][:, cs]))
        t_i = jnp.tanh(by_tile(pre[:, MXU_DIM:], jnp.add, vb_scr[V_BI][:, cs]))
        log_a = by_tile(by_tile(t_r, jnp.multiply, half_scale[:, cs]), jnp.add, half_scale[:, cs])
        a_scr[j] = jnp.exp(log_a)
        th = jnp.tanh(log_a)
        a_half = -0.5 * th
        gate = jnp.where(a_half > 0.0, a_half * lax.rsqrt(a_half * (1.0 - th)), 0.0)
        u_scr[j] = gate * ((t_i + 1.0) * xg)

    def scan_t(t, h):
        t8 = pl.multiple_of(t * batch, batch)
        at = jnp.concatenate([a_scr[j, pl.ds(t8, batch), :] for j in range(N_GRP)], axis=1)
        ut = jnp.concatenate([u_scr[j, pl.ds(t8, batch), :] for j in range(N_GRP)], axis=1)
        h = at * h + ut
        for c in range(N_SLAB):
            hs_scr[c, pl.ds(t, batch, stride=PITCH), :] = h[:, c * LANES:(c + 1) * LANES]
        return h
    hst_scr[...] = lax.fori_loop(0, TS, scan_t, hst_scr[...], unroll=True)

    def lru_out_b(b, c):
        r0 = pl.multiple_of(b * TS, TS)
        p0 = pl.multiple_of(b * PITCH, 8)
        hb = jnp.concatenate([hs_scr[c2, pl.ds(p0, TS), :] for c2 in range(N_SLAB)], axis=1)
        v = hb * _silu_from_half(gl_scr[pl.ds(r0, TS), :])
        y = scale_by(v * _rms_scale(v), V_LG)
        ycat_scr[pl.ds(r0, TS), ATTN_WIDTH:] = y.astype(bf16)
        return c
    lax.fori_loop(0, batch, lru_out_b, 0, unroll=True)

    ga_scr[...] = (
        jnp.dot(ycat_scr[:, :ATTN_WIDTH], woa_ref[...], preferred_element_type=f32)
        + jnp.dot(ycat_scr[:, ATTN_WIDTH:], wol_ref[...], preferred_element_type=f32))

    def out_b(b, c):
        x2 = x_ref[b] + ga_scr[pl.ds(pl.multiple_of(b * TS, TS), TS), :]
        o_ref[b] = scale_by(x2 * _rms_scale(x2), V_FG)
        return c
    lax.fori_loop(0, batch, out_b, 0, unroll=True)


def _rope_tables(seq):
    half = ROT_DIM // 2
    pos = jnp.arange(seq, dtype=jnp.float32)
    inv_freq = ROPE_THETA ** (-jnp.arange(0, ROT_DIM, 2, dtype=jnp.float32) / ROT_DIM)
    ang = pos[:, None] * jnp.tile(inv_freq, LANES // half)[None, :]
    cos, sin = jnp.cos(ang), jnp.sin(ang)
    d = np.arange(LANES) % HEAD_DIM
    lo = (d < half)[None, :]
    hi = ((d >= half) & (d < ROT_DIM))[None, :]
    return jnp.stack([jnp.where(lo | hi, cos, 1.0),
                      jnp.where(hi, sin, 0.0),
                      jnp.where(lo, -sin, 0.0)])


def _group_major(a, axis):
    shp = a.shape
    a = a.reshape(shp[:axis] + (N_KV_HEADS, GROUP, HEAD_DIM) + shp[axis + 1:])
    return jnp.swapaxes(a, axis, axis + 1).reshape(shp)


def kernel(x, ln_gain, w_in, sinks, conv_w, conv_b, w_rgate, b_rgate, w_igate, b_igate,
           lru_lambda, attn_out_gain, lru_out_gain, w_out, final_gain):
    batch, seq, d_model = x.shape
    assert d_model == D_MODEL and seq % TS == 0 and ln_gain.shape[0] == 1 and N_SLOT == 3
    assert batch == SUBLANES
    f32, bf16 = jnp.float32, jnp.bfloat16
    rows = batch * TS

    col_scale = np.ones((1, OFF_XL + 2 * LRU_WIDTH), np.float32)
    col_scale[:, OFF_GA:OFF_XL] = 0.5
    col_scale[:, OFF_XL + LRU_WIDTH:] = 0.5
    w_all = (w_in[0] * col_scale).astype(bf16)
    wo_all = w_out[0].astype(bf16)
    old_col = np.arange(ATTN_WIDTH).reshape(N_KV_HEADS, GROUP, HEAD_DIM).transpose(1, 0, 2).reshape(-1)
    perm = np.zeros((ATTN_WIDTH, ATTN_WIDTH), np.float32)
    perm[old_col, np.arange(ATTN_WIDTH)] = 1.0
    perm = jnp.asarray(perm, dtype=bf16)
    wq = jnp.dot(w_all[:, :OFF_KV], perm, preferred_element_type=f32).astype(bf16)
    wga = jnp.dot(w_all[:, OFF_GA:OFF_XL], perm, preferred_element_type=f32).astype(bf16)
    woa = jnp.dot(perm.T, wo_all[:ATTN_WIDTH], preferred_element_type=f32).astype(bf16)
    wkv, wlru, wol = w_all[:, OFF_KV:OFF_GA], w_all[:, OFF_XL:], wo_all[ATTN_WIDTH:]

    def block_diag(wb):
        per = MXU_DIM // LRU_BLOCK_W
        wb = wb.reshape(N_GRP, per, LRU_BLOCK_W, 1, LRU_BLOCK_W)
        eye = np.eye(per, dtype=np.float32).reshape(1, per, 1, per, 1)
        return (wb * (0.5 * eye)).reshape(N_GRP, MXU_DIM, MXU_DIM)
    wg = jnp.concatenate([block_diag(w_rgate[0]), block_diag(w_igate[0])], axis=2).astype(bf16)

    vec_rows = [None] * (V_CW + CONV_W)
    vec_rows[V_LN], vec_rows[V_CB], vec_rows[V_BR] = ln_gain, conv_b, 0.5 * b_rgate
    vec_rows[V_BI], vec_rows[V_LAM], vec_rows[V_LG] = 0.5 * b_igate, lru_lambda, lru_out_gain
    vec_rows[V_AG], vec_rows[V_FG] = _group_major(attn_out_gain, 1), final_gain[None]
    vec_rows[V_CW:] = [conv_w[0][tap:tap + 1] for tap in range(CONV_W)]
    vecs = jnp.concatenate(vec_rows + [jnp.zeros((V_ROWS - len(vec_rows), D_MODEL), f32)], axis=0)

    rope_tbl = _rope_tables(seq)

    const = lambda *_: (0, 0)
    single = dict(pipeline_mode=pl.Buffered(1))
    grid_spec = pl.GridSpec(
        grid=(seq // TS,),
        in_specs=[
            pl.BlockSpec(memory_space=pltpu.SMEM),
            pl.BlockSpec((batch, TS, D_MODEL), lambda s: (0, s, 0)),
            pl.BlockSpec((3, TS, LANES), lambda s: (0, s, 0)),
            pl.BlockSpec((V_ROWS, D_MODEL), const, **single),
            pl.BlockSpec((D_MODEL, ATTN_WIDTH), const, **single),
            pl.BlockSpec((D_MODEL, 2 * KV_WIDTH), const, **single),
            pl.BlockSpec((D_MODEL, ATTN_WIDTH), const, **single),
            pl.BlockSpec((D_MODEL, 2 * LRU_WIDTH), const, **single),
            pl.BlockSpec((N_GRP, MXU_DIM, 2 * MXU_DIM), lambda s: (0, 0, 0), **single),
            pl.BlockSpec((ATTN_WIDTH, D_MODEL), const, **single),
            pl.BlockSpec((LRU_WIDTH, D_MODEL), const, **single),
        ],
        out_specs=pl.BlockSpec((batch, TS, D_MODEL), lambda s: (0, s, 0)),
        scratch_shapes=[
            pltpu.VMEM((rows, D_MODEL), bf16),
            pltpu.VMEM((rows, ATTN_WIDTH), bf16),
            pltpu.VMEM((batch, N_KV_HEADS, KEYS, LANES), bf16),
            pltpu.VMEM((batch, N_SLOT, KV_WIDTH, TS), bf16),
            pltpu.VMEM((rows, ATTN_WIDTH), f32),
            pltpu.VMEM((N_SLAB, batch * PITCH, LANES), f32),
            pltpu.VMEM((rows, LRU_WIDTH), f32),
            pltpu.VMEM((N_GRP, rows, MXU_DIM), f32),
            pltpu.VMEM((N_GRP, rows, MXU_DIM), f32),
            pltpu.VMEM((N_GRP, rows, MXU_DIM), f32),
            pltpu.VMEM((N_SLAB, batch * PITCH, LANES), f32),
            pltpu.VMEM((rows, 2 * ATTN_WIDTH), bf16),
            pltpu.VMEM((CONV_W - 1, batch, LRU_WIDTH), f32),
            pltpu.VMEM((batch, LRU_WIDTH), f32),
            pltpu.VMEM((V_ROWS, SUBLANES, D_MODEL), f32),
        ],
    )
    return pl.pallas_call(
        functools.partial(_layer_kernel, batch=batch),
        grid_spec=grid_spec,
        out_shape=jax.ShapeDtypeStruct(x.shape, x.dtype),
        compiler_params=pltpu.CompilerParams(
            dimension_semantics=("arbitrary",), vmem_limit_bytes=VMEM_LIMIT_BYTES),
        name="hymba_layer",
    )(sinks[0], x, rope_tbl, vecs, wq, wkv, wga, wlru, wg, woa, wol)
```

```python
import functools

import numpy as np
import jax
import jax.numpy as jnp
from jax import lax
from jax.experimental import pallas as pl
from jax.experimental.pallas import tpu as pltpu

D_MODEL = 1024
HEAD_DIM = 64
N_Q_HEADS = 16
N_KV_HEADS = 4
GROUP = N_Q_HEADS // N_KV_HEADS
ATTN_WIDTH = N_Q_HEADS * HEAD_DIM
KV_WIDTH = N_KV_HEADS * HEAD_DIM
WINDOW = 128
ROT_DIM = HEAD_DIM // 4
ROPE_THETA = 500000.0
NEG_INF = -1e30
LRU_WIDTH = 1024
LRU_BLOCKS = 16
LRU_BLOCK_W = LRU_WIDTH // LRU_BLOCKS
CONV_W = 4
LRU_C = 8.0
EPS = 1e-6

LANES = 128
SUBLANES = 8
MXU_DIM = 256
VMEM_LIMIT_BYTES = 58 * 1024 * 1024

TS = 64
N_SLOT = WINDOW // TS + 1
KEYS = N_SLOT * TS
LOG2E = 1.4426950408889634
Q_SCALE = HEAD_DIM ** -0.5 * LOG2E
PITCH = TS + 8
N_SLAB = LRU_WIDTH // LANES
N_GRP = LRU_WIDTH // MXU_DIM

OFF_KV, OFF_GA, OFF_XL = ATTN_WIDTH, ATTN_WIDTH + 2 * KV_WIDTH, 2 * ATTN_WIDTH + 2 * KV_WIDTH

V_CB, V_BR, V_BI, V_LAM, V_FG, V_CW = 0, 1, 2, 3, 4, 5
V_ROWS = 16


def _silu_from_half(hx):
    return hx * (jnp.tanh(hx) + 1.0)


def _rms_scale(v):
    return lax.rsqrt(jnp.mean(v * v, axis=-1, keepdims=True) + EPS)


def _layer_kernel(sinks_ref, x_ref, rope_ref, vecs_ref, wq_ref, wkv_ref, wga_ref, wlru_ref,
                  wg_ref, woa_ref, wol_ref, o_ref,
                  h_scr, q_scr, km_scr, vt_scr, ga_scr, xl_scr, gl_scr, xc_scr, a_scr, u_scr,
                  hs_scr, ycat_scr, cw_scr, hst_scr, vb_scr, *, batch):
    f32, bf16 = jnp.float32, jnp.bfloat16
    rows = batch * TS
    step = pl.program_id(0)

    @pl.when(step == 0)
    def _init_carries():
        km_scr[...] = jnp.zeros_like(km_scr)
        vt_scr[...] = jnp.zeros_like(vt_scr)
        cw_scr[...] = jnp.zeros_like(cw_scr)
        hst_scr[...] = jnp.zeros_like(hst_scr)
        for row in range(V_ROWS):
            vb_scr[row] = jnp.broadcast_to(vecs_ref[row:row + 1, :], (SUBLANES, D_MODEL))

    def by_tile(v, op, tile):
        return op(v.reshape(-1, SUBLANES, v.shape[-1]), tile[None]).reshape(v.shape)

    def scale_by(v, row):
        return by_tile(v, jnp.multiply, vb_scr[row])

    def norm_b(b, c):
        xb = x_ref[b]
        hb = xb * _rms_scale(xb)
        h_scr[pl.ds(pl.multiple_of(b * TS, TS), TS), :] = hb.astype(bf16)
        return c
    lax.fori_loop(0, batch, norm_b, 0, unroll=True)

    hmat = h_scr[...]

    def proj(w_ref, c0, c1):
        return jnp.dot(hmat, w_ref[:, c0:c1], preferred_element_type=f32)

    k_tab = (rope_ref[0], rope_ref[1], rope_ref[2])
    q_tab = tuple(t * Q_SCALE for t in k_tab)

    def rope(z, tab):
        cos_t, sin_hi, sin_lo = tab
        return (z * cos_t[None] + pltpu.roll(z, 8, 2) * sin_hi[None]
                + pltpu.roll(z, LANES - 8, 2) * sin_lo[None])

    for g in range(GROUP):
        zq = proj(wq_ref, g * MXU_DIM, (g + 1) * MXU_DIM).reshape(batch, TS, MXU_DIM)
        for sl in range(MXU_DIM // LANES):
            r = rope(zq[:, :, sl * LANES:(sl + 1) * LANES], q_tab)
            c0 = g * MXU_DIM + sl * LANES
            q_scr[:, c0:c0 + LANES] = r.reshape(rows, LANES).astype(bf16)

    slot = lax.rem(step, N_SLOT)
    slot_row = pl.multiple_of(slot * TS, TS)
    zkv = proj(wkv_ref, 0, 2 * KV_WIDTH).reshape(batch, TS, 2 * KV_WIDTH)
    lane_half = lax.broadcasted_iota(jnp.int32, (1, 1, LANES), 2) // HEAD_DIM
    for sl in range(KV_WIDTH // LANES):
        r = rope(zkv[:, :, sl * LANES:(sl + 1) * LANES], k_tab).astype(bf16)
        for hh in range(LANES // HEAD_DIM):
            km_scr[:, sl * (LANES // HEAD_DIM) + hh, pl.ds(slot_row, TS), :] = jnp.where(
                lane_half == hh, r, jnp.zeros_like(r))
    for b in range(batch):
        vt_scr[b, slot] = zkv[b, :, KV_WIDTH:].T.astype(bf16)

    half = ATTN_WIDTH // 2
    for c in range(2):
        ga_scr[:, c * half:(c + 1) * half] = proj(wga_ref, c * half, (c + 1) * half)
        gl_scr[:, c * half:(c + 1) * half] = proj(wlru_ref, LRU_WIDTH + c * half, LRU_WIDTH + (c + 1) * half)
        zx = proj(wlru_ref, c * half, (c + 1) * half)
        for sl in range(half // LANES):
            for b in range(batch):
                xl_scr[c * (half // LANES) + sl, b * PITCH:b * PITCH + TS, :] = (
                    zx[b * TS:(b + 1) * TS, sl * LANES:(sl + 1) * LANES])

    jj = lax.broadcasted_iota(jnp.int32, (TS, GROUP * TS), 0)
    ii = lax.broadcasted_iota(jnp.int32, (TS, GROUP * TS), 1) % TS
    blocks = []
    for sg in range(N_SLOT):
        age = lax.rem(step + (N_SLOT - sg), N_SLOT)
        valid = ((age == 0) & (jj <= ii)) | (age == 1) | ((age == 2) & (jj > ii))
        blocks.append(jnp.where(valid & (step >= age), 0.0, NEG_INF).astype(f32))
    bias_t = jnp.concatenate(blocks, axis=0)
    lane_g = lax.broadcasted_iota(jnp.int32, (1, GROUP * TS), 1) // TS
    sink_rows = []
    for k in range(N_KV_HEADS):
        row = jnp.zeros((1, GROUP * TS), f32)
        for g in range(GROUP):
            row = jnp.where(lane_g == g, sinks_ref[k * GROUP + g] * LOG2E, row)
        sink_rows.append(row)

    def attn_b(b, c):
        r0 = pl.multiple_of(b * TS, TS)
        q_pair = [jnp.concatenate(
            [q_scr[pl.ds(r0, TS), g * MXU_DIM + sl * LANES:g * MXU_DIM + (sl + 1) * LANES]
             for g in range(GROUP)], axis=0) for sl in range(KV_WIDTH // LANES)]
        o_heads = []
        for k in range(N_KV_HEADS):
            st = lax.dot_general(km_scr[b, k], q_pair[k // 2], (((1,), (1,)), ((), ())),
                                 preferred_element_type=f32)
            st = st + bias_t
            m = jnp.maximum(jnp.max(st, axis=0, keepdims=True), sink_rows[k])
            p = jnp.exp2(st - m)
            den = jnp.sum(p, axis=0, keepdims=True) + jnp.exp2(sink_rows[k] - m)
            pb = p.astype(bf16)
            o = jnp.zeros((HEAD_DIM, GROUP * TS), f32)
            for sg in range(N_SLOT):
                o = o + jnp.dot(vt_scr[b, sg, k * HEAD_DIM:(k + 1) * HEAD_DIM, :],
                                pb[sg * TS:(sg + 1) * TS], preferred_element_type=f32)
            o_heads.append(o * (1.0 / den))
        o_rows = jnp.concatenate(o_heads, axis=0).T
        ob = jnp.concatenate([o_rows[g * TS:(g + 1) * TS] for g in range(GROUP)], axis=1)
        v = ob * _silu_from_half(ga_scr[pl.ds(r0, TS), :])
        y = v * _rms_scale(v)
        ycat_scr[pl.ds(r0, TS), 0:ATTN_WIDTH] = y.astype(bf16)
        return c
    lax.fori_loop(0, batch, attn_b, 0, unroll=8)

    def conv_t(t, carry):
        x1, x2, x3 = carry
        xt = jnp.concatenate(
            [xl_scr[c, pl.ds(t, batch, stride=PITCH), :] for c in range(N_SLAB)], axis=1)
        u = (vb_scr[V_CB] + vb_scr[V_CW + 3] * xt + vb_scr[V_CW + 2] * x1
             + vb_scr[V_CW + 1] * x2 + vb_scr[V_CW] * x3)
        t8 = pl.multiple_of(t * batch, batch)
        for j in range(N_GRP):
            xc_scr[j, pl.ds(t8, batch), :] = u[:, j * MXU_DIM:(j + 1) * MXU_DIM]
        return xt, x1, x2
    tail = lax.fori_loop(0, TS, conv_t, (cw_scr[0], cw_scr[1], cw_scr[2]), unroll=True)
    for i in range(CONV_W - 1):
        cw_scr[i] = tail[i]

    lam = vb_scr[V_LAM]
    softplus_neg_lam = jnp.maximum(-lam, 0.0) + jnp.log1p(jnp.exp(-jnp.abs(lam)))
    half_scale = (-0.5 * LRU_C) * softplus_neg_lam
    for j in range(N_GRP):
        cs = slice(j * MXU_DIM, (j + 1) * MXU_DIM)
        xg = xc_scr[j]
        pre = jnp.dot(xg.astype(bf16), wg_ref[j], preferred_element_type=f32)
        t_r = jnp.tanh(by_tile(pre[:, :MXU_DIM], jnp.add, vb_scr[V_BR][:, cs]))
        t_i = jnp.tanh(by_tile(pre[:, MXU_DIM:], jnp.add, vb_scr[V_BI][:, cs]))
        log_a = by_tile(by_tile(t_r, jnp.multiply, half_scale[:, cs]), jnp.add, half_scale[:, cs])
        a_scr[j] = jnp.exp(log_a)
        th = jnp.tanh(log_a)
        a_half = -0.5 * th
        gate = jnp.where(a_half > 0.0, a_half * lax.rsqrt(a_half * (1.0 - th)), 0.0)
        u_scr[j] = gate * ((t_i + 1.0) * xg)

    def scan_t(t, h):
        t8 = pl.multiple_of(t * batch, batch)
        at = jnp.concatenate([a_scr[j, pl.ds(t8, batch), :] for j in range(N_GRP)], axis=1)
        ut = jnp.concatenate([u_scr[j, pl.ds(t8, batch), :] for j in range(N_GRP)], axis=1)
        h = at * h + ut
        for c in range(N_SLAB):
            hs_scr[c, pl.ds(t, batch, stride=PITCH), :] = h[:, c * LANES:(c + 1) * LANES]
        return h
    hst_scr[...] = lax.fori_loop(0, TS, scan_t, hst_scr[...], unroll=True)

    def lru_out_b(b, c):
        r0 = pl.multiple_of(b * TS, TS)
        p0 = pl.multiple_of(b * PITCH, 8)
        hb = jnp.concatenate([hs_scr[c2, pl.ds(p0, TS), :] for c2 in range(N_SLAB)], axis=1)
        v = hb * _silu_from_half(gl_scr[pl.ds(r0, TS), :])
        y = v * _rms_scale(v)
        ycat_scr[pl.ds(r0, TS), ATTN_WIDTH:] = y.astype(bf16)
        return c
    lax.fori_loop(0, batch, lru_out_b, 0, unroll=True)

    ga_scr[...] = (
        jnp.dot(ycat_scr[:, :ATTN_WIDTH], woa_ref[...], preferred_element_type=f32)
        + jnp.dot(ycat_scr[:, ATTN_WIDTH:], wol_ref[...], preferred_element_type=f32))

    def out_b(b, c):
        x2 = x_ref[b] + ga_scr[pl.ds(pl.multiple_of(b * TS, TS), TS), :]
        o_ref[b] = scale_by(x2 * _rms_scale(x2), V_FG)
        return c
    lax.fori_loop(0, batch, out_b, 0, unroll=True)


def _rope_tables(seq):
    half = ROT_DIM // 2
    pos = jnp.arange(seq, dtype=jnp.float32)
    inv_freq = ROPE_THETA ** (-jnp.arange(0, ROT_DIM, 2, dtype=jnp.float32) / ROT_DIM)
    ang = pos[:, None] * jnp.tile(inv_freq, LANES // half)[None, :]
    cos, sin = jnp.cos(ang), jnp.sin(ang)
    d = np.arange(LANES) % HEAD_DIM
    lo = (d < half)[None, :]
    hi = ((d >= half) & (d < ROT_DIM))[None, :]
    return jnp.stack([jnp.where(lo | hi, cos, 1.0),
                      jnp.where(hi, sin, 0.0),
                      jnp.where(lo, -sin, 0.0)])


def kernel(x, ln_gain, w_in, sinks, conv_w, conv_b, w_rgate, b_rgate, w_igate, b_igate,
           lru_lambda, attn_out_gain, lru_out_gain, w_out, final_gain):
    batch, seq, d_model = x.shape
    assert d_model == D_MODEL and seq % TS == 0 and ln_gain.shape[0] == 1 and N_SLOT == 3
    assert batch == SUBLANES
    f32, bf16 = jnp.float32, jnp.bfloat16
    rows = batch * TS

    col_scale = np.ones((1, OFF_XL + 2 * LRU_WIDTH), np.float32)
    col_scale[:, OFF_GA:OFF_XL] = 0.5
    col_scale[:, OFF_XL + LRU_WIDTH:] = 0.5
    w_all = (w_in[0] * col_scale * ln_gain[0][:, None]).astype(bf16)
    mix_gain = jnp.concatenate([attn_out_gain[0], lru_out_gain[0]])
    wo_all = (w_out[0] * mix_gain[:, None]).astype(bf16)
    old_col = np.arange(ATTN_WIDTH).reshape(N_KV_HEADS, GROUP, HEAD_DIM).transpose(1, 0, 2).reshape(-1)
    perm = np.zeros((ATTN_WIDTH, ATTN_WIDTH), np.float32)
    perm[old_col, np.arange(ATTN_WIDTH)] = 1.0
    perm = jnp.asarray(perm, dtype=bf16)
    wq = jnp.dot(w_all[:, :OFF_KV], perm, preferred_element_type=f32).astype(bf16)
    wga = jnp.dot(w_all[:, OFF_GA:OFF_XL], perm, preferred_element_type=f32).astype(bf16)
    woa = jnp.dot(perm.T, wo_all[:ATTN_WIDTH], preferred_element_type=f32).astype(bf16)
    wkv, wlru, wol = w_all[:, OFF_KV:OFF_GA], w_all[:, OFF_XL:], wo_all[ATTN_WIDTH:]

    def block_diag(wb):
        per = MXU_DIM // LRU_BLOCK_W
        wb = wb.reshape(N_GRP, per, LRU_BLOCK_W, 1, LRU_BLOCK_W)
        eye = np.eye(per, dtype=np.float32).reshape(1, per, 1, per, 1)
        return (wb * (0.5 * eye)).reshape(N_GRP, MXU_DIM, MXU_DIM)
    wg = jnp.concatenate([block_diag(w_rgate[0]), block_diag(w_igate[0])], axis=2).astype(bf16)

    vec_rows = [None] * (V_CW + CONV_W)
    vec_rows[V_CB], vec_rows[V_BR], vec_rows[V_BI] = conv_b, 0.5 * b_rgate, 0.5 * b_igate
    vec_rows[V_LAM], vec_rows[V_FG] = lru_lambda, final_gain[None]
    vec_rows[V_CW:] = [conv_w[0][tap:tap + 1] for tap in range(CONV_W)]
    vecs = jnp.concatenate(vec_rows + [jnp.zeros((V_ROWS - len(vec_rows), D_MODEL), f32)], axis=0)

    rope_tbl = _rope_tables(seq)

    const = lambda *_: (0, 0)
    single = dict(pipeline_mode=pl.Buffered(1))
    grid_spec = pl.GridSpec(
        grid=(seq // TS,),
        in_specs=[
            pl.BlockSpec(memory_space=pltpu.SMEM),
            pl.BlockSpec((batch, TS, D_MODEL), lambda s: (0, s, 0)),
            pl.BlockSpec((3, TS, LANES), lambda s: (0, s, 0)),
            pl.BlockSpec((V_ROWS, D_MODEL), const, **single),
            pl.BlockSpec((D_MODEL, ATTN_WIDTH), const, **single),
            pl.BlockSpec((D_MODEL, 2 * KV_WIDTH), const, **single),
            pl.BlockSpec((D_MODEL, ATTN_WIDTH), const, **single),
            pl.BlockSpec((D_MODEL, 2 * LRU_WIDTH), const, **single),
            pl.BlockSpec((N_GRP, MXU_DIM, 2 * MXU_DIM), lambda s: (0, 0, 0), **single),
            pl.BlockSpec((ATTN_WIDTH, D_MODEL), const, **single),
            pl.BlockSpec((LRU_WIDTH, D_MODEL), const, **single),
        ],
        out_specs=pl.BlockSpec((batch, TS, D_MODEL), lambda s: (0, s, 0)),
        scratch_shapes=[
            pltpu.VMEM((rows, D_MODEL), bf16),
            pltpu.VMEM((rows, ATTN_WIDTH), bf16),
            pltpu.VMEM((batch, N_KV_HEADS, KEYS, LANES), bf16),
            pltpu.VMEM((batch, N_SLOT, KV_WIDTH, TS), bf16),
            pltpu.VMEM((rows, ATTN_WIDTH), f32),
            pltpu.VMEM((N_SLAB, batch * PITCH, LANES), f32),
            pltpu.VMEM((rows, LRU_WIDTH), f32),
            pltpu.VMEM((N_GRP, rows, MXU_DIM), f32),
            pltpu.VMEM((N_GRP, rows, MXU_DIM), f32),
            pltpu.VMEM((N_GRP, rows, MXU_DIM), f32),
            pltpu.VMEM((N_SLAB, batch * PITCH, LANES), f32),
            pltpu.VMEM((rows, 2 * ATTN_WIDTH), bf16),
            pltpu.VMEM((CONV_W - 1, batch, LRU_WIDTH), f32),
            pltpu.VMEM((batch, LRU_WIDTH), f32),
            pltpu.VMEM((V_ROWS, SUBLANES, D_MODEL), f32),
        ],
    )
    return pl.pallas_call(
        functools.partial(_layer_kernel, batch=batch),
        grid_spec=grid_spec,
        out_shape=jax.ShapeDtypeStruct(x.shape, x.dtype),
        compiler_params=pltpu.CompilerParams(
            dimension_semantics=("arbitrary",), vmem_limit_bytes=VMEM_LIMIT_BYTES),
        name="hymba_layer",
    )(sinks[0], x, rope_tbl, vecs, wq, wkv, wga, wlru, wg, woa, wol)
```

```python
import functools

import numpy as np
import jax
import jax.numpy as jnp
from jax import lax
from jax.experimental import pallas as pl
from jax.experimental.pallas import tpu as pltpu

D_MODEL = 1024
HEAD_DIM = 64
N_Q_HEADS = 16
N_KV_HEADS = 4
GROUP = N_Q_HEADS // N_KV_HEADS
ATTN_WIDTH = N_Q_HEADS * HEAD_DIM
KV_WIDTH = N_KV_HEADS * HEAD_DIM
WINDOW = 128
ROT_DIM = HEAD_DIM // 4
ROPE_THETA = 500000.0
NEG_INF = -1e30
LRU_WIDTH = 1024
LRU_BLOCKS = 16
LRU_BLOCK_W = LRU_WIDTH // LRU_BLOCKS
CONV_W = 4
LRU_C = 8.0
EPS = 1e-6

LANES = 128
SUBLANES = 8
MXU_DIM = 256
VMEM_LIMIT_BYTES = 58 * 1024 * 1024

TS = 64
N_SLOT = WINDOW // TS + 1
KEYS = N_SLOT * TS
LOG2E = 1.4426950408889634
Q_SCALE = HEAD_DIM ** -0.5 * LOG2E
PITCH = TS + 8
N_SLAB = LRU_WIDTH // LANES
N_GRP = LRU_WIDTH // MXU_DIM

OFF_KV, OFF_GA, OFF_XL = ATTN_WIDTH, ATTN_WIDTH + 2 * KV_WIDTH, 2 * ATTN_WIDTH + 2 * KV_WIDTH

V_LN, V_CB, V_BR, V_BI, V_LAM, V_AG, V_LG, V_FG, V_CW = 0, 1, 2, 3, 4, 5, 6, 7, 8
V_HS = V_CW + CONV_W
V_ROWS = 16


def _silu_from_half(hx):
    return hx * (jnp.tanh(hx) + 1.0)


def _rms_scale(v):
    return lax.rsqrt(jnp.mean(v * v, axis=-1, keepdims=True) + EPS)


def _layer_kernel(sinks_ref, x_ref, rope_ref, vecs_ref, wq_ref, wkv_ref, wga_ref, wlru_ref,
                  wg_ref, woa_ref, wol_ref, o_ref,
                  h_scr, q_scr, km_scr, vt_scr, ga_scr, xl_scr, gl_scr, xc_scr, a_scr, u_scr,
                  hs_scr, ycat_scr, cw_scr, hst_scr, vb_scr, *, batch):
    f32, bf16 = jnp.float32, jnp.bfloat16
    rows = batch * TS
    step = pl.program_id(0)

    @pl.when(step == 0)
    def _init_carries():
        km_scr[...] = jnp.zeros_like(km_scr)
        vt_scr[...] = jnp.zeros_like(vt_scr)
        cw_scr[...] = jnp.zeros_like(cw_scr)
        hst_scr[...] = jnp.zeros_like(hst_scr)
        for row in range(V_ROWS):
            vb_scr[row] = jnp.broadcast_to(vecs_ref[row:row + 1, :], (SUBLANES, D_MODEL))
        lam = vb_scr[V_LAM]
        softplus_neg_lam = jnp.maximum(-lam, 0.0) + jnp.log1p(jnp.exp(-jnp.abs(lam)))
        vb_scr[V_HS] = (-0.5 * LRU_C) * softplus_neg_lam

    def by_tile(v, op, tile):
        return op(v.reshape(-1, SUBLANES, v.shape[-1]), tile[None]).reshape(v.shape)

    def scale_by(v, row):
        return by_tile(v, jnp.multiply, vb_scr[row])

    def norm_b(b, c):
        xb = x_ref[b]
        hb = scale_by(xb * _rms_scale(xb), V_LN)
        h_scr[pl.ds(pl.multiple_of(b * TS, TS), TS), :] = hb.astype(bf16)
        return c
    lax.fori_loop(0, batch, norm_b, 0, unroll=True)

    hmat = h_scr[...]

    def proj(w_ref, c0, c1):
        return jnp.dot(hmat, w_ref[:, c0:c1], preferred_element_type=f32)

    k_tab = (rope_ref[0], rope_ref[1], rope_ref[2])
    q_tab = tuple(t * Q_SCALE for t in k_tab)

    def rope(z, tab):
        cos_t, sin_hi, sin_lo = tab
        return (z * cos_t[None] + pltpu.roll(z, 8, 2) * sin_hi[None]
                + pltpu.roll(z, LANES - 8, 2) * sin_lo[None])

    for g in range(GROUP):
        zq = proj(wq_ref, g * MXU_DIM, (g + 1) * MXU_DIM).reshape(batch, TS, MXU_DIM)
        for sl in range(MXU_DIM // LANES):
            r = rope(zq[:, :, sl * LANES:(sl + 1) * LANES], q_tab)
            c0 = g * MXU_DIM + sl * LANES
            q_scr[:, c0:c0 + LANES] = r.reshape(rows, LANES).astype(bf16)

    slot = lax.rem(step, N_SLOT)
    slot_row = pl.multiple_of(slot * TS, TS)
    zkv = proj(wkv_ref, 0, 2 * KV_WIDTH).reshape(batch, TS, 2 * KV_WIDTH)
    lane_half = lax.broadcasted_iota(jnp.int32, (1, 1, LANES), 2) // HEAD_DIM
    for sl in range(KV_WIDTH // LANES):
        r = rope(zkv[:, :, sl * LANES:(sl + 1) * LANES], k_tab).astype(bf16)
        for hh in range(LANES // HEAD_DIM):
            km_scr[:, sl * (LANES // HEAD_DIM) + hh, pl.ds(slot_row, TS), :] = jnp.where(
                lane_half == hh, r, jnp.zeros_like(r))
    for b in range(batch):
        vt_scr[b, slot] = zkv[b, :, KV_WIDTH:].T.astype(bf16)

    half = ATTN_WIDTH // 2
    for c in range(2):
        ga_scr[:, c * half:(c + 1) * half] = proj(wga_ref, c * half, (c + 1) * half)
        gl_scr[:, c * half:(c + 1) * half] = proj(wlru_ref, LRU_WIDTH + c * half, LRU_WIDTH + (c + 1) * half)
        zx = proj(wlru_ref, c * half, (c + 1) * half)
        for sl in range(half // LANES):
            for b in range(batch):
                xl_scr[c * (half // LANES) + sl, b * PITCH:b * PITCH + TS, :] = (
                    zx[b * TS:(b + 1) * TS, sl * LANES:(sl + 1) * LANES])

    jj = lax.broadcasted_iota(jnp.int32, (TS, GROUP * TS), 0)
    ii = lax.broadcasted_iota(jnp.int32, (TS, GROUP * TS), 1) % TS
    blocks = []
    for sg in range(N_SLOT):
        age = lax.rem(step + (N_SLOT - sg), N_SLOT)
        valid = ((age == 0) & (jj <= ii)) | (age == 1) | ((age == 2) & (jj > ii))
        blocks.append(jnp.where(valid & (step >= age), 0.0, NEG_INF).astype(f32))
    bias_t = jnp.concatenate(blocks, axis=0)
    lane_g = lax.broadcasted_iota(jnp.int32, (1, GROUP * TS), 1) // TS
    sink_rows = []
    for k in range(N_KV_HEADS):
        row = jnp.zeros((1, GROUP * TS), f32)
        for g in range(GROUP):
            row = jnp.where(lane_g == g, sinks_ref[k * GROUP + g] * LOG2E, row)
        sink_rows.append(row)

    def attn_b(b, c):
        r0 = pl.multiple_of(b * TS, TS)
        q_pair = [jnp.concatenate(
            [q_scr[pl.ds(r0, TS), g * MXU_DIM + sl * LANES:g * MXU_DIM + (sl + 1) * LANES]
             for g in range(GROUP)], axis=0) for sl in range(KV_WIDTH // LANES)]
        o_heads = []
        for k in range(N_KV_HEADS):
            st = lax.dot_general(km_scr[b, k], q_pair[k // 2], (((1,), (1,)), ((), ())),
                                 preferred_element_type=f32)
            st = st + bias_t
            m = jnp.maximum(jnp.max(st, axis=0, keepdims=True), sink_rows[k])
            p = jnp.exp2(st - m)
            den = jnp.sum(p, axis=0, keepdims=True) + jnp.exp2(sink_rows[k] - m)
            pb = p.astype(bf16)
            o = jnp.zeros((HEAD_DIM, GROUP * TS), f32)
            for sg in range(N_SLOT):
                o = o + jnp.dot(vt_scr[b, sg, k * HEAD_DIM:(k + 1) * HEAD_DIM, :],
                                pb[sg * TS:(sg + 1) * TS], preferred_element_type=f32)
            o_heads.append(o * (1.0 / den))
        o_rows = jnp.concatenate(o_heads, axis=0).T
        ob = jnp.concatenate([o_rows[g * TS:(g + 1) * TS] for g in range(GROUP)], axis=1)
        v = ob * _silu_from_half(ga_scr[pl.ds(r0, TS), :])
        y = scale_by(v * _rms_scale(v), V_AG)
        ycat_scr[pl.ds(r0, TS), 0:ATTN_WIDTH] = y.astype(bf16)
        return c
    lax.fori_loop(0, batch, attn_b, 0, unroll=8)

    def conv_t(t, carry):
        x1, x2, x3 = carry
        xt = jnp.concatenate(
            [xl_scr[c, pl.ds(t, batch, stride=PITCH), :] for c in range(N_SLAB)], axis=1)
        u = (vb_scr[V_CB] + vb_scr[V_CW + 3] * xt + vb_scr[V_CW + 2] * x1
             + vb_scr[V_CW + 1] * x2 + vb_scr[V_CW] * x3)
        t8 = pl.multiple_of(t * batch, batch)
        for j in range(N_GRP):
            xc_scr[j, pl.ds(t8, batch), :] = u[:, j * MXU_DIM:(j + 1) * MXU_DIM]
        return xt, x1, x2
    tail = lax.fori_loop(0, TS, conv_t, (cw_scr[0], cw_scr[1], cw_scr[2]), unroll=True)
    for i in range(CONV_W - 1):
        cw_scr[i] = tail[i]

    half_scale = vb_scr[V_HS]
    for j in range(N_GRP):
        cs = slice(j * MXU_DIM, (j + 1) * MXU_DIM)
        xg = xc_scr[j]
        pre = jnp.dot(xg.astype(bf16), wg_ref[j], preferred_element_type=f32)
        t_r = jnp.tanh(by_tile(pre[:, :MXU_DIM], jnp.add, vb_scr[V_BR][:, cs]))
        t_i = jnp.tanh(by_tile(pre[:, MXU_DIM:], jnp.add, vb_scr[V_BI][:, cs]))
        log_a = by_tile(by_tile(t_r, jnp.multiply, half_scale[:, cs]), jnp.add, half_scale[:, cs])
        a_scr[j] = jnp.exp(log_a)
        th = jnp.tanh(log_a)
        a_half = -0.5 * th
        gate = jnp.where(a_half > 0.0, a_half * lax.rsqrt(a_half * (1.0 - th)), 0.0)
        u_scr[j] = gate * ((t_i + 1.0) * xg)

    def scan_t(t, h):
        t8 = pl.multiple_of(t * batch, batch)
        at = jnp.concatenate([a_scr[j, pl.ds(t8, batch), :] for j in range(N_GRP)], axis=1)
        ut = jnp.concatenate([u_scr[j, pl.ds(t8, batch), :] for j in range(N_GRP)], axis=1)
        h = at * h + ut
        for c in range(N_SLAB):
            hs_scr[c, pl.ds(t, batch, stride=PITCH), :] = h[:, c * LANES:(c + 1) * LANES]
        return h
    hst_scr[...] = lax.fori_loop(0, TS, scan_t, hst_scr[...], unroll=True)

    def lru_out_b(b, c):
        r0 = pl.multiple_of(b * TS, TS)
        p0 = pl.multiple_of(b * PITCH, 8)
        hb = jnp.concatenate([hs_scr[c2, pl.ds(p0, TS), :] for c2 in range(N_SLAB)], axis=1)
        v = hb * _silu_from_half(gl_scr[pl.ds(r0, TS), :])
        y = scale_by(v * _rms_scale(v), V_LG)
        ycat_scr[pl.ds(r0, TS), ATTN_WIDTH:] = y.astype(bf16)
        return c
    lax.fori_loop(0, batch, lru_out_b, 0, unroll=True)

    ga_scr[...] = (
        jnp.dot(ycat_scr[:, :ATTN_WIDTH], woa_ref[...], preferred_element_type=f32)
        + jnp.dot(ycat_scr[:, ATTN_WIDTH:], wol_ref[...], preferred_element_type=f32))

    def out_b(b, c):
        x2 = x_ref[b] + ga_scr[pl.ds(pl.multiple_of(b * TS, TS), TS), :]
        o_ref[b] = scale_by(x2 * _rms_scale(x2), V_FG)
        return c
    lax.fori_loop(0, batch, out_b, 0, unroll=True)


def _rope_tables(seq):
    half = ROT_DIM // 2
    pos = jnp.arange(seq, dtype=jnp.float32)
    inv_freq = ROPE_THETA ** (-jnp.arange(0, ROT_DIM, 2, dtype=jnp.float32) / ROT_DIM)
    ang = pos[:, None] * jnp.tile(inv_freq, LANES // half)[None, :]
    cos, sin = jnp.cos(ang), jnp.sin(ang)
    d = np.arange(LANES) % HEAD_DIM
    lo = (d < half)[None, :]
    hi = ((d >= half) & (d < ROT_DIM))[None, :]
    return jnp.stack([jnp.where(lo | hi, cos, 1.0),
                      jnp.where(hi, sin, 0.0),
                      jnp.where(lo, -sin, 0.0)])


def _group_major(a, axis):
    shp = a.shape
    a = a.reshape(shp[:axis] + (N_KV_HEADS, GROUP, HEAD_DIM) + shp[axis + 1:])
    return jnp.swapaxes(a, axis, axis + 1).reshape(shp)


def kernel(x, ln_gain, w_in, sinks, conv_w, conv_b, w_rgate, b_rgate, w_igate, b_igate,
           lru_lambda, attn_out_gain, lru_out_gain, w_out, final_gain):
    batch, seq, d_model = x.shape
    assert d_model == D_MODEL and seq % TS == 0 and ln_gain.shape[0] == 1 and N_SLOT == 3
    assert batch == SUBLANES
    f32, bf16 = jnp.float32, jnp.bfloat16
    rows = batch * TS

    col_scale = np.ones((1, OFF_XL + 2 * LRU_WIDTH), np.float32)
    col_scale[:, OFF_GA:OFF_XL] = 0.5
    col_scale[:, OFF_XL + LRU_WIDTH:] = 0.5
    w_all = (w_in[0] * col_scale).astype(bf16)
    wo_all = w_out[0].astype(bf16)
    old_col = np.arange(ATTN_WIDTH).reshape(N_KV_HEADS, GROUP, HEAD_DIM).transpose(1, 0, 2).reshape(-1)
    perm = np.zeros((ATTN_WIDTH, ATTN_WIDTH), np.float32)
    perm[old_col, np.arange(ATTN_WIDTH)] = 1.0
    perm = jnp.asarray(perm, dtype=bf16)
    wq = jnp.dot(w_all[:, :OFF_KV], perm, preferred_element_type=f32).astype(bf16)
    wga = jnp.dot(w_all[:, OFF_GA:OFF_XL], perm, preferred_element_type=f32).astype(bf16)
    woa = jnp.dot(perm.T, wo_all[:ATTN_WIDTH], preferred_element_type=f32).astype(bf16)
    wkv, wlru, wol = w_all[:, OFF_KV:OFF_GA], w_all[:, OFF_XL:], wo_all[ATTN_WIDTH:]

    def block_diag(wb):
        per = MXU_DIM // LRU_BLOCK_W
        wb = wb.reshape(N_GRP, per, LRU_BLOCK_W, 1, LRU_BLOCK_W)
        eye = np.eye(per, dtype=np.float32).reshape(1, per, 1, per, 1)
        return (wb * (0.5 * eye)).reshape(N_GRP, MXU_DIM, MXU_DIM)
    wg = jnp.concatenate([block_diag(w_rgate[0]), block_diag(w_igate[0])], axis=2).astype(bf16)

    vec_rows = [None] * (V_CW + CONV_W)
    vec_rows[V_LN], vec_rows[V_CB], vec_rows[V_BR] = ln_gain, conv_b, 0.5 * b_rgate
    vec_rows[V_BI], vec_rows[V_LAM], vec_rows[V_LG] = 0.5 * b_igate, lru_lambda, lru_out_gain
    vec_rows[V_AG], vec_rows[V_FG] = _group_major(attn_out_gain, 1), final_gain[None]
    vec_rows[V_CW:] = [conv_w[0][tap:tap + 1] for tap in range(CONV_W)]
    vecs = jnp.concatenate(vec_rows + [jnp.zeros((V_ROWS - len(vec_rows), D_MODEL), f32)], axis=0)

    rope_tbl = _rope_tables(seq)

    const = lambda *_: (0, 0)
    single = dict(pipeline_mode=pl.Buffered(1))
    grid_spec = pl.GridSpec(
        grid=(seq // TS,),
        in_specs=[
            pl.BlockSpec(memory_space=pltpu.SMEM),
            pl.BlockSpec((batch, TS, D_MODEL), lambda s: (0, s, 0)),
            pl.BlockSpec((3, TS, LANES), lambda s: (0, s, 0)),
            pl.BlockSpec((V_ROWS, D_MODEL), const, **single),
            pl.BlockSpec((D_MODEL, ATTN_WIDTH), const, **single),
            pl.BlockSpec((D_MODEL, 2 * KV_WIDTH), const, **single),
            pl.BlockSpec((D_MODEL, ATTN_WIDTH), const, **single),
            pl.BlockSpec((D_MODEL, 2 * LRU_WIDTH), const, **single),
            pl.BlockSpec((N_GRP, MXU_DIM, 2 * MXU_DIM), lambda s: (0, 0, 0), **single),
            pl.BlockSpec((ATTN_WIDTH, D_MODEL), const, **single),
            pl.BlockSpec((LRU_WIDTH, D_MODEL), const, **single),
        ],
        out_specs=pl.BlockSpec((batch, TS, D_MODEL), lambda s: (0, s, 0)),
        scratch_shapes=[
            pltpu.VMEM((rows, D_MODEL), bf16),
            pltpu.VMEM((rows, ATTN_WIDTH), bf16),
            pltpu.VMEM((batch, N_KV_HEADS, KEYS, LANES), bf16),
            pltpu.VMEM((batch, N_SLOT, KV_WIDTH, TS), bf16),
            pltpu.VMEM((rows, ATTN_WIDTH), f32),
            pltpu.VMEM((N_SLAB, batch * PITCH, LANES), f32),
            pltpu.VMEM((rows, LRU_WIDTH), f32),
            pltpu.VMEM((N_GRP, rows, MXU_DIM), f32),
            pltpu.VMEM((N_GRP, rows, MXU_DIM), f32),
            pltpu.VMEM((N_GRP, rows, MXU_DIM), f32),
            pltpu.VMEM((N_SLAB, batch * PITCH, LANES), f32),
            pltpu.VMEM((rows, 2 * ATTN_WIDTH), bf16),
            pltpu.VMEM((CONV_W - 1, batch, LRU_WIDTH), f32),
            pltpu.VMEM((batch, LRU_WIDTH), f32),
            pltpu.VMEM((V_ROWS, SUBLANES, D_MODEL), f32),
        ],
    )
    return pl.pallas_call(
        functools.partial(_layer_kernel, batch=batch),
        grid_spec=grid_spec,
        out_shape=jax.ShapeDtypeStruct(x.shape, x.dtype),
        compiler_params=pltpu.CompilerParams(
            dimension_semantics=("arbitrary",), vmem_limit_bytes=VMEM_LIMIT_BYTES),
        name="hymba_layer",
    )(sinks[0], x, rope_tbl, vecs, wq, wkv, wga, wlru, wg, woa, wol)
```

```python
import functools

import numpy as np
import jax
import jax.numpy as jnp
from jax import lax
from jax.experimental import pallas as pl
from jax.experimental.pallas import tpu as pltpu

D_MODEL = 1024
HEAD_DIM = 64
N_Q_HEADS = 16
N_KV_HEADS = 4
GROUP = N_Q_HEADS // N_KV_HEADS
ATTN_WIDTH = N_Q_HEADS * HEAD_DIM
KV_WIDTH = N_KV_HEADS * HEAD_DIM
WINDOW = 128
ROT_DIM = HEAD_DIM // 4
ROPE_THETA = 500000.0
NEG_INF = -1e30
LRU_WIDTH = 1024
LRU_BLOCKS = 16
LRU_BLOCK_W = LRU_WIDTH // LRU_BLOCKS
CONV_W = 4
LRU_C = 8.0
EPS = 1e-6

LANES = 128
SUBLANES = 8
MXU_DIM = 256
VMEM_LIMIT_BYTES = 58 * 1024 * 1024

TS = 64
N_SLOT = WINDOW // TS + 1
KEYS = N_SLOT * TS
LOG2E = 1.4426950408889634
Q_SCALE = HEAD_DIM ** -0.5 * LOG2E
PITCH = TS + 8
N_SLAB = LRU_WIDTH // LANES
N_GRP = LRU_WIDTH // MXU_DIM

OFF_KV, OFF_GA, OFF_XL = ATTN_WIDTH, ATTN_WIDTH + 2 * KV_WIDTH, 2 * ATTN_WIDTH + 2 * KV_WIDTH

V_LN, V_CB, V_BR, V_BI, V_LAM, V_AG, V_LG, V_FG, V_CW = 0, 1, 2, 3, 4, 5, 6, 7, 8
V_HS = V_CW + CONV_W
V_ROWS = 16


def _silu_from_half(hx):
    return hx * (jnp.tanh(hx) + 1.0)


def _rms_scale(v):
    return lax.rsqrt(jnp.mean(v * v, axis=-1, keepdims=True) + EPS)


def _layer_kernel(sinks_ref, x_ref, rope_ref, vecs_ref, wq_ref, wkv_ref, wga_ref, wlru_ref,
                  wg_ref, woa_ref, wol_ref, o_ref,
                  h_scr, q_scr, km_scr, vt_scr, ga_scr, xl_scr, gl_scr, xc_scr, a_scr, u_scr,
                  hs_scr, ycat_scr, cw_scr, hst_scr, vb_scr, *, batch):
    f32, bf16 = jnp.float32, jnp.bfloat16
    rows = batch * TS
    step = pl.program_id(0)

    @pl.when(step == 0)
    def _init_carries():
        km_scr[...] = jnp.zeros_like(km_scr)
        vt_scr[...] = jnp.zeros_like(vt_scr)
        cw_scr[...] = jnp.zeros_like(cw_scr)
        hst_scr[...] = jnp.zeros_like(hst_scr)
        for row in range(V_ROWS):
            vb_scr[row] = jnp.broadcast_to(vecs_ref[row:row + 1, :], (SUBLANES, D_MODEL))
        lam = vb_scr[V_LAM]
        softplus_neg_lam = jnp.maximum(-lam, 0.0) + jnp.log1p(jnp.exp(-jnp.abs(lam)))
        vb_scr[V_HS] = (-0.5 * LRU_C) * softplus_neg_lam

    def by_tile(v, op, tile):
        return op(v.reshape(-1, SUBLANES, v.shape[-1]), tile[None]).reshape(v.shape)

    def scale_by(v, row):
        return by_tile(v, jnp.multiply, vb_scr[row])

    def norm_b(b, c):
        xb = x_ref[b]
        hb = scale_by(xb * _rms_scale(xb), V_LN)
        h_scr[pl.ds(pl.multiple_of(b * TS, TS), TS), :] = hb.astype(bf16)
        return c
    lax.fori_loop(0, batch, norm_b, 0, unroll=True)

    hmat = h_scr[...]

    def proj(w_ref, c0, c1):
        return jnp.dot(hmat, w_ref[:, c0:c1], preferred_element_type=f32)

    k_tab = (rope_ref[0], rope_ref[1], rope_ref[2])
    q_tab = tuple(t * Q_SCALE for t in k_tab)

    def rope(z, tab):
        cos_t, sin_hi, sin_lo = tab
        return (z * cos_t[None] + pltpu.roll(z, 8, 2) * sin_hi[None]
                + pltpu.roll(z, LANES - 8, 2) * sin_lo[None])

    half = ATTN_WIDTH // 2
    for c in range(2):
        zx = proj(wlru_ref, c * half, (c + 1) * half)
        for sl in range(half // LANES):
            for b in range(batch):
                xl_scr[c * (half // LANES) + sl, b * PITCH:b * PITCH + TS, :] = (
                    zx[b * TS:(b + 1) * TS, sl * LANES:(sl + 1) * LANES])

    def conv_t(t, carry):
        x1, x2, x3 = carry
        xt = jnp.concatenate(
            [xl_scr[c, pl.ds(t, batch, stride=PITCH), :] for c in range(N_SLAB)], axis=1)
        u = (vb_scr[V_CB] + vb_scr[V_CW + 3] * xt + vb_scr[V_CW + 2] * x1
             + vb_scr[V_CW + 1] * x2 + vb_scr[V_CW] * x3)
        t8 = pl.multiple_of(t * batch, batch)
        for j in range(N_GRP):
            xc_scr[j, pl.ds(t8, batch), :] = u[:, j * MXU_DIM:(j + 1) * MXU_DIM]
        return xt, x1, x2
    tail = lax.fori_loop(0, TS, conv_t, (cw_scr[0], cw_scr[1], cw_scr[2]), unroll=True)
    for i in range(CONV_W - 1):
        cw_scr[i] = tail[i]

    for g in range(GROUP):
        zq = proj(wq_ref, g * MXU_DIM, (g + 1) * MXU_DIM).reshape(batch, TS, MXU_DIM)
        for sl in range(MXU_DIM // LANES):
            r = rope(zq[:, :, sl * LANES:(sl + 1) * LANES], q_tab)
            c0 = g * MXU_DIM + sl * LANES
            q_scr[:, c0:c0 + LANES] = r.reshape(rows, LANES).astype(bf16)

    slot = lax.rem(step, N_SLOT)
    slot_row = pl.multiple_of(slot * TS, TS)
    zkv = proj(wkv_ref, 0, 2 * KV_WIDTH).reshape(batch, TS, 2 * KV_WIDTH)
    lane_half = lax.broadcasted_iota(jnp.int32, (1, 1, LANES), 2) // HEAD_DIM
    for sl in range(KV_WIDTH // LANES):
        r = rope(zkv[:, :, sl * LANES:(sl + 1) * LANES], k_tab).astype(bf16)
        for hh in range(LANES // HEAD_DIM):
            km_scr[:, sl * (LANES // HEAD_DIM) + hh, pl.ds(slot_row, TS), :] = jnp.where(
                lane_half == hh, r, jnp.zeros_like(r))
    for b in range(batch):
        vt_scr[b, slot] = zkv[b, :, KV_WIDTH:].T.astype(bf16)

    half_scale = vb_scr[V_HS]
    for j in range(N_GRP):
        cs = slice(j * MXU_DIM, (j + 1) * MXU_DIM)
        xg = xc_scr[j]
        pre = jnp.dot(xg.astype(bf16), wg_ref[j], preferred_element_type=f32)
        t_r = jnp.tanh(by_tile(pre[:, :MXU_DIM], jnp.add, vb_scr[V_BR][:, cs]))
        t_i = jnp.tanh(by_tile(pre[:, MXU_DIM:], jnp.add, vb_scr[V_BI][:, cs]))
        log_a = by_tile(by_tile(t_r, jnp.multiply, half_scale[:, cs]), jnp.add, half_scale[:, cs])
        a_scr[j] = jnp.exp(log_a)
        th = jnp.tanh(log_a)
        a_half = -0.5 * th
        gate = jnp.where(a_half > 0.0, a_half * lax.rsqrt(a_half * (1.0 - th)), 0.0)
        u_scr[j] = gate * ((t_i + 1.0) * xg)

    for c in range(2):
        ga_scr[:, c * half:(c + 1) * half] = proj(wga_ref, c * half, (c + 1) * half)
        gl_scr[:, c * half:(c + 1) * half] = proj(wlru_ref, LRU_WIDTH + c * half, LRU_WIDTH + (c + 1) * half)

    def scan_t(t, h):
        t8 = pl.multiple_of(t * batch, batch)
        at = jnp.concatenate([a_scr[j, pl.ds(t8, batch), :] for j in range(N_GRP)], axis=1)
        ut = jnp.concatenate([u_scr[j, pl.ds(t8, batch), :] for j in range(N_GRP)], axis=1)
        h = at * h + ut
        for c in range(N_SLAB):
            hs_scr[c, pl.ds(t, batch, stride=PITCH), :] = h[:, c * LANES:(c + 1) * LANES]
        return h
    hst_scr[...] = lax.fori_loop(0, TS, scan_t, hst_scr[...], unroll=True)

    jj = lax.broadcasted_iota(jnp.int32, (TS, GROUP * TS), 0)
    ii = lax.broadcasted_iota(jnp.int32, (TS, GROUP * TS), 1) % TS
    blocks = []
    for sg in range(N_SLOT):
        age = lax.rem(step + (N_SLOT - sg), N_SLOT)
        valid = ((age == 0) & (jj <= ii)) | (age == 1) | ((age == 2) & (jj > ii))
        blocks.append(jnp.where(valid & (step >= age), 0.0, NEG_INF).astype(f32))
    bias_t = jnp.concatenate(blocks, axis=0)
    lane_g = lax.broadcasted_iota(jnp.int32, (1, GROUP * TS), 1) // TS
    sink_rows = []
    for k in range(N_KV_HEADS):
        row = jnp.zeros((1, GROUP * TS), f32)
        for g in range(GROUP):
            row = jnp.where(lane_g == g, sinks_ref[k * GROUP + g] * LOG2E, row)
        sink_rows.append(row)

    def attn_b(b, c):
        r0 = pl.multiple_of(b * TS, TS)
        q_pair = [jnp.concatenate(
            [q_scr[pl.ds(r0, TS), g * MXU_DIM + sl * LANES:g * MXU_DIM + (sl + 1) * LANES]
             for g in range(GROUP)], axis=0) for sl in range(KV_WIDTH // LANES)]
        o_heads = []
        for k in range(N_KV_HEADS):
            st = lax.dot_general(km_scr[b, k], q_pair[k // 2], (((1,), (1,)), ((), ())),
                                 preferred_element_type=f32)
            st = st + bias_t
            m = jnp.maximum(jnp.max(st, axis=0, keepdims=True), sink_rows[k])
            p = jnp.exp2(st - m)
            den = jnp.sum(p, axis=0, keepdims=True) + jnp.exp2(sink_rows[k] - m)
            pb = p.astype(bf16)
            o = jnp.zeros((HEAD_DIM, GROUP * TS), f32)
            for sg in range(N_SLOT):
                o = o + jnp.dot(vt_scr[b, sg, k * HEAD_DIM:(k + 1) * HEAD_DIM, :],
                                pb[sg * TS:(sg + 1) * TS], preferred_element_type=f32)
            o_heads.append(o * (1.0 / den))
        o_rows = jnp.concatenate(o_heads, axis=0).T
        ob = jnp.concatenate([o_rows[g * TS:(g + 1) * TS] for g in range(GROUP)], axis=1)
        v = ob * _silu_from_half(ga_scr[pl.ds(r0, TS), :])
        y = scale_by(v * _rms_scale(v), V_AG)
        ycat_scr[pl.ds(r0, TS), 0:ATTN_WIDTH] = y.astype(bf16)
        return c
    lax.fori_loop(0, batch, attn_b, 0, unroll=8)

    def lru_out_b(b, c):
        r0 = pl.multiple_of(b * TS, TS)
        p0 = pl.multiple_of(b * PITCH, 8)
        hb = jnp.concatenate([hs_scr[c2, pl.ds(p0, TS), :] for c2 in range(N_SLAB)], axis=1)
        v = hb * _silu_from_half(gl_scr[pl.ds(r0, TS), :])
        y = scale_by(v * _rms_scale(v), V_LG)
        ycat_scr[pl.ds(r0, TS), ATTN_WIDTH:] = y.astype(bf16)
        return c
    lax.fori_loop(0, batch, lru_out_b, 0, unroll=True)

    ga_scr[...] = (
        jnp.dot(ycat_scr[:, :ATTN_WIDTH], woa_ref[...], preferred_element_type=f32)
        + jnp.dot(ycat_scr[:, ATTN_WIDTH:], wol_ref[...], preferred_element_type=f32))

    def out_b(b, c):
        x2 = x_ref[b] + ga_scr[pl.ds(pl.multiple_of(b * TS, TS), TS), :]
        o_ref[b] = scale_by(x2 * _rms_scale(x2), V_FG)
        return c
    lax.fori_loop(0, batch, out_b, 0, unroll=True)


def _rope_tables(seq):
    half = ROT_DIM // 2
    pos = jnp.arange(seq, dtype=jnp.float32)
    inv_freq = ROPE_THETA ** (-jnp.arange(0, ROT_DIM, 2, dtype=jnp.float32) / ROT_DIM)
    ang = pos[:, None] * jnp.tile(inv_freq, LANES // half)[None, :]
    cos, sin = jnp.cos(ang), jnp.sin(ang)
    d = np.arange(LANES) % HEAD_DIM
    lo = (d < half)[None, :]
    hi = ((d >= half) & (d < ROT_DIM))[None, :]
    return jnp.stack([jnp.where(lo | hi, cos, 1.0),
                      jnp.where(hi, sin, 0.0),
                      jnp.where(lo, -sin, 0.0)])


def _group_major(a, axis):
    shp = a.shape
    a = a.reshape(shp[:axis] + (N_KV_HEADS, GROUP, HEAD_DIM) + shp[axis + 1:])
    return jnp.swapaxes(a, axis, axis + 1).reshape(shp)


def kernel(x, ln_gain, w_in, sinks, conv_w, conv_b, w_rgate, b_rgate, w_igate, b_igate,
           lru_lambda, attn_out_gain, lru_out_gain, w_out, final_gain):
    batch, seq, d_model = x.shape
    assert d_model == D_MODEL and seq % TS == 0 and ln_gain.shape[0] == 1 and N_SLOT == 3
    assert batch == SUBLANES
    f32, bf16 = jnp.float32, jnp.bfloat16
    rows = batch * TS

    col_scale = np.ones((1, OFF_XL + 2 * LRU_WIDTH), np.float32)
    col_scale[:, OFF_GA:OFF_XL] = 0.5
    col_scale[:, OFF_XL + LRU_WIDTH:] = 0.5
    w_all = (w_in[0] * col_scale).astype(bf16)
    wo_all = w_out[0].astype(bf16)
    old_col = np.arange(ATTN_WIDTH).reshape(N_KV_HEADS, GROUP, HEAD_DIM).transpose(1, 0, 2).reshape(-1)
    perm = np.zeros((ATTN_WIDTH, ATTN_WIDTH), np.float32)
    perm[old_col, np.arange(ATTN_WIDTH)] = 1.0
    perm = jnp.asarray(perm, dtype=bf16)
    wq = jnp.dot(w_all[:, :OFF_KV], perm, preferred_element_type=f32).astype(bf16)
    wga = jnp.dot(w_all[:, OFF_GA:OFF_XL], perm, preferred_element_type=f32).astype(bf16)
    woa = jnp.dot(perm.T, wo_all[:ATTN_WIDTH], preferred_element_type=f32).astype(bf16)
    wkv, wlru, wol = w_all[:, OFF_KV:OFF_GA], w_all[:, OFF_XL:], wo_all[ATTN_WIDTH:]

    def block_diag(wb):
        per = MXU_DIM // LRU_BLOCK_W
        wb = wb.reshape(N_GRP, per, LRU_BLOCK_W, 1, LRU_BLOCK_W)
        eye = np.eye(per, dtype=np.float32).reshape(1, per, 1, per, 1)
        return (wb * (0.5 * eye)).reshape(N_GRP, MXU_DIM, MXU_DIM)
    wg = jnp.concatenate([block_diag(w_rgate[0]), block_diag(w_igate[0])], axis=2).astype(bf16)

    vec_rows = [None] * (V_CW + CONV_W)
    vec_rows[V_LN], vec_rows[V_CB], vec_rows[V_BR] = ln_gain, conv_b, 0.5 * b_rgate
    vec_rows[V_BI], vec_rows[V_LAM], vec_rows[V_LG] = 0.5 * b_igate, lru_lambda, lru_out_gain
    vec_rows[V_AG], vec_rows[V_FG] = _group_major(attn_out_gain, 1), final_gain[None]
    vec_rows[V_CW:] = [conv_w[0][tap:tap + 1] for tap in range(CONV_W)]
    vecs = jnp.concatenate(vec_rows + [jnp.zeros((V_ROWS - len(vec_rows), D_MODEL), f32)], axis=0)

    rope_tbl = _rope_tables(seq)

    const = lambda *_: (0, 0)
    single = dict(pipeline_mode=pl.Buffered(1))
    grid_spec = pl.GridSpec(
        grid=(seq // TS,),
        in_specs=[
            pl.BlockSpec(memory_space=pltpu.SMEM),
            pl.BlockSpec((batch, TS, D_MODEL), lambda s: (0, s, 0)),
            pl.BlockSpec((3, TS, LANES), lambda s: (0, s, 0)),
            pl.BlockSpec((V_ROWS, D_MODEL), const, **single),
            pl.BlockSpec((D_MODEL, ATTN_WIDTH), const, **single),
            pl.BlockSpec((D_MODEL, 2 * KV_WIDTH), const, **single),
            pl.BlockSpec((D_MODEL, ATTN_WIDTH), const, **single),
            pl.BlockSpec((D_MODEL, 2 * LRU_WIDTH), const, **single),
            pl.BlockSpec((N_GRP, MXU_DIM, 2 * MXU_DIM), lambda s: (0, 0, 0), **single),
            pl.BlockSpec((ATTN_WIDTH, D_MODEL), const, **single),
            pl.BlockSpec((LRU_WIDTH, D_MODEL), const, **single),
        ],
        out_specs=pl.BlockSpec((batch, TS, D_MODEL), lambda s: (0, s, 0)),
        scratch_shapes=[
            pltpu.VMEM((rows, D_MODEL), bf16),
            pltpu.VMEM((rows, ATTN_WIDTH), bf16),
            pltpu.VMEM((batch, N_KV_HEADS, KEYS, LANES), bf16),
            pltpu.VMEM((batch, N_SLOT, KV_WIDTH, TS), bf16),
            pltpu.VMEM((rows, ATTN_WIDTH), f32),
            pltpu.VMEM((N_SLAB, batch * PITCH, LANES), f32),
            pltpu.VMEM((rows, LRU_WIDTH), f32),
            pltpu.VMEM((N_GRP, rows, MXU_DIM), f32),
            pltpu.VMEM((N_GRP, rows, MXU_DIM), f32),
            pltpu.VMEM((N_GRP, rows, MXU_DIM), f32),
            pltpu.VMEM((N_SLAB, batch * PITCH, LANES), f32),
            pltpu.VMEM((rows, 2 * ATTN_WIDTH), bf16),
            pltpu.VMEM((CONV_W - 1, batch, LRU_WIDTH), f32),
            pltpu.VMEM((batch, LRU_WIDTH), f32),
            pltpu.VMEM((V_ROWS, SUBLANES, D_MODEL), f32),
        ],
    )
    return pl.pallas_call(
        functools.partial(_layer_kernel, batch=batch),
        grid_spec=grid_spec,
        out_shape=jax.ShapeDtypeStruct(x.shape, x.dtype),
        compiler_params=pltpu.CompilerParams(
            dimension_semantics=("arbitrary",), vmem_limit_bytes=VMEM_LIMIT_BYTES),
        name="hymba_layer",
    )(sinks[0], x, rope_tbl, vecs, wq, wkv, wga, wlru, wg, woa, wol)
```

```python
import functools

import numpy as np
import jax
import jax.numpy as jnp
from jax import lax
from jax.experimental import pallas as pl
from jax.experimental.pallas import tpu as pltpu

D_MODEL = 1024
HEAD_DIM = 64
N_Q_HEADS = 16
N_KV_HEADS = 4
GROUP = N_Q_HEADS // N_KV_HEADS
ATTN_WIDTH = N_Q_HEADS * HEAD_DIM
KV_WIDTH = N_KV_HEADS * HEAD_DIM
WINDOW = 128
ROT_DIM = HEAD_DIM // 4
ROPE_THETA = 500000.0
NEG_INF = -1e30
LRU_WIDTH = 1024
LRU_BLOCKS = 16
LRU_BLOCK_W = LRU_WIDTH // LRU_BLOCKS
CONV_W = 4
LRU_C = 8.0
EPS = 1e-6

LANES = 128
SUBLANES = 8
MXU_DIM = 256
VMEM_LIMIT_BYTES = 58 * 1024 * 1024

TS = 64
N_SLOT = WINDOW // TS + 1
KEYS = N_SLOT * TS
LOG2E = 1.4426950408889634
Q_SCALE = HEAD_DIM ** -0.5 * LOG2E
PITCH = TS + 8
N_SLAB = LRU_WIDTH // LANES
N_GRP = LRU_WIDTH // MXU_DIM

OFF_KV, OFF_GA, OFF_XL = ATTN_WIDTH, ATTN_WIDTH + 2 * KV_WIDTH, 2 * ATTN_WIDTH + 2 * KV_WIDTH

V_LN, V_CB, V_BR, V_BI, V_LAM, V_AG, V_LG, V_FG, V_CW = 0, 1, 2, 3, 4, 5, 6, 7, 8
V_HS = V_CW + CONV_W
V_ROWS = 16


def _silu_from_half(hx):
    return hx * (jnp.tanh(hx) + 1.0)


def _rms_scale(v):
    return lax.rsqrt(jnp.mean(v * v, axis=-1, keepdims=True) + EPS)


def _layer_kernel(sinks_ref, x_ref, rope_ref, vecs_ref, wq_ref, wkv_ref, wga_ref, wlru_ref,
                  wg_ref, woa_ref, wol_ref, o_ref,
                  h_scr, q_scr, km_scr, vt_scr, ga_scr, xl_scr, gl_scr, xc_scr, a_scr, u_scr,
                  hs_scr, ycat_scr, cw_scr, hst_scr, vb_scr, *, batch):
    f32, bf16 = jnp.float32, jnp.bfloat16
    rows = batch * TS
    step = pl.program_id(0)

    @pl.when(step == 0)
    def _init_carries():
        km_scr[...] = jnp.zeros_like(km_scr)
        vt_scr[...] = jnp.zeros_like(vt_scr)
        cw_scr[...] = jnp.zeros_like(cw_scr)
        hst_scr[...] = jnp.zeros_like(hst_scr)
        for row in range(V_ROWS):
            vb_scr[row] = jnp.broadcast_to(vecs_ref[row:row + 1, :], (SUBLANES, D_MODEL))
        lam = vb_scr[V_LAM]
        softplus_neg_lam = jnp.maximum(-lam, 0.0) + jnp.log1p(jnp.exp(-jnp.abs(lam)))
        vb_scr[V_HS] = (-0.5 * LRU_C) * softplus_neg_lam

    def by_tile(v, op, tile):
        return op(v.reshape(-1, SUBLANES, v.shape[-1]), tile[None]).reshape(v.shape)

    def scale_by(v, row):
        return by_tile(v, jnp.multiply, vb_scr[row])

    def norm_b(b, c):
        xb = x_ref[b]
        hb = scale_by(xb * _rms_scale(xb), V_LN)
        h_scr[pl.ds(pl.multiple_of(b * TS, TS), TS), :] = hb.astype(bf16)
        return c
    lax.fori_loop(0, batch, norm_b, 0, unroll=True)

    hmat = h_scr[...]

    def proj(w_ref, c0, c1):
        return jnp.dot(hmat, w_ref[:, c0:c1], preferred_element_type=f32)

    k_tab = (rope_ref[0], rope_ref[1], rope_ref[2])
    q_tab = tuple(t * Q_SCALE for t in k_tab)

    def rope(z, tab):
        cos_t, sin_hi, sin_lo = tab
        return (z * cos_t[None] + pltpu.roll(z, 8, 2) * sin_hi[None]
                + pltpu.roll(z, LANES - 8, 2) * sin_lo[None])

    for g in range(GROUP):
        zq = proj(wq_ref, g * MXU_DIM, (g + 1) * MXU_DIM).reshape(batch, TS, MXU_DIM)
        for sl in range(MXU_DIM // LANES):
            r = rope(zq[:, :, sl * LANES:(sl + 1) * LANES], q_tab)
            c0 = g * MXU_DIM + sl * LANES
            q_scr[:, c0:c0 + LANES] = r.reshape(rows, LANES).astype(bf16)

    slot = lax.rem(step, N_SLOT)
    slot_row = pl.multiple_of(slot * TS, TS)
    zkv = proj(wkv_ref, 0, 2 * KV_WIDTH).reshape(batch, TS, 2 * KV_WIDTH)
    lane_half = lax.broadcasted_iota(jnp.int32, (1, 1, LANES), 2) // HEAD_DIM
    for sl in range(KV_WIDTH // LANES):
        r = rope(zkv[:, :, sl * LANES:(sl + 1) * LANES], k_tab).astype(bf16)
        for hh in range(LANES // HEAD_DIM):
            km_scr[:, sl * (LANES // HEAD_DIM) + hh, pl.ds(slot_row, TS), :] = jnp.where(
                lane_half == hh, r, jnp.zeros_like(r))
    for b in range(batch):
        vt_scr[b, slot] = zkv[b, :, KV_WIDTH:].T.astype(bf16)

    half = ATTN_WIDTH // 2
    for c in range(2):
        ga_scr[:, c * half:(c + 1) * half] = proj(wga_ref, c * half, (c + 1) * half)
        gl_scr[:, c * half:(c + 1) * half] = proj(wlru_ref, LRU_WIDTH + c * half, LRU_WIDTH + (c + 1) * half)
        zx = proj(wlru_ref, c * half, (c + 1) * half)
        for sl in range(half // LANES):
            for b in range(batch):
                xl_scr[c * (half // LANES) + sl, b * PITCH:b * PITCH + TS, :] = (
                    zx[b * TS:(b + 1) * TS, sl * LANES:(sl + 1) * LANES])

    jj = lax.broadcasted_iota(jnp.int32, (TS, GROUP * TS), 0)
    ii = lax.broadcasted_iota(jnp.int32, (TS, GROUP * TS), 1) % TS
    blocks = []
    for sg in range(N_SLOT):
        age = lax.rem(step + (N_SLOT - sg), N_SLOT)
        valid = ((age == 0) & (jj <= ii)) | (age == 1) | ((age == 2) & (jj > ii))
        blocks.append(jnp.where(valid & (step >= age), 0.0, NEG_INF).astype(f32))
    bias_t = jnp.concatenate(blocks, axis=0)
    lane_g = lax.broadcasted_iota(jnp.int32, (1, GROUP * TS), 1) // TS
    sink_rows = []
    for k in range(N_KV_HEADS):
        row = jnp.zeros((1, GROUP * TS), f32)
        for g in range(GROUP):
            row = jnp.where(lane_g == g, sinks_ref[k * GROUP + g] * LOG2E, row)
        sink_rows.append(row)

    def attn_b(b, c):
        r0 = pl.multiple_of(b * TS, TS)
        q_pair = [jnp.concatenate(
            [q_scr[pl.ds(r0, TS), g * MXU_DIM + sl * LANES:g * MXU_DIM + (sl + 1) * LANES]
             for g in range(GROUP)], axis=0) for sl in range(KV_WIDTH // LANES)]
        o_heads = []
        for k in range(N_KV_HEADS):
            st = lax.dot_general(km_scr[b, k], q_pair[k // 2], (((1,), (1,)), ((), ())),
                                 preferred_element_type=f32)
            st = st + bias_t
            m = jnp.maximum(jnp.max(st, axis=0, keepdims=True), sink_rows[k])
            p = jnp.exp2(st - m)
            den = jnp.sum(p, axis=0, keepdims=True) + jnp.exp2(sink_rows[k] - m)
            pb = p.astype(bf16)
            o = jnp.zeros((HEAD_DIM, GROUP * TS), f32)
            for sg in range(N_SLOT):
                o = o + jnp.dot(vt_scr[b, sg, k * HEAD_DIM:(k + 1) * HEAD_DIM, :],
                                pb[sg * TS:(sg + 1) * TS], preferred_element_type=f32)
            o_heads.append(o * (1.0 / den))
        o_rows = jnp.concatenate(o_heads, axis=0).T
        ob = jnp.concatenate([o_rows[g * TS:(g + 1) * TS] for g in range(GROUP)], axis=1)
        v = ob * _silu_from_half(ga_scr[pl.ds(r0, TS), :])
        y = scale_by(v * _rms_scale(v), V_AG)
        ycat_scr[pl.ds(r0, TS), 0:ATTN_WIDTH] = y.astype(bf16)
        return c
    lax.fori_loop(0, batch, attn_b, 0, unroll=8)

    ga_scr[...] = jnp.dot(ycat_scr[:, :ATTN_WIDTH], woa_ref[...], preferred_element_type=f32)

    def conv_t(t, carry):
        x1, x2, x3 = carry
        xt = jnp.concatenate(
            [xl_scr[c, pl.ds(t, batch, stride=PITCH), :] for c in range(N_SLAB)], axis=1)
        u = (vb_scr[V_CB] + vb_scr[V_CW + 3] * xt + vb_scr[V_CW + 2] * x1
             + vb_scr[V_CW + 1] * x2 + vb_scr[V_CW] * x3)
        t8 = pl.multiple_of(t * batch, batch)
        for j in range(N_GRP):
            xc_scr[j, pl.ds(t8, batch), :] = u[:, j * MXU_DIM:(j + 1) * MXU_DIM]
        return xt, x1, x2
    tail = lax.fori_loop(0, TS, conv_t, (cw_scr[0], cw_scr[1], cw_scr[2]), unroll=True)
    for i in range(CONV_W - 1):
        cw_scr[i] = tail[i]

    half_scale = vb_scr[V_HS]
    for j in range(N_GRP):
        cs = slice(j * MXU_DIM, (j + 1) * MXU_DIM)
        xg = xc_scr[j]
        pre = jnp.dot(xg.astype(bf16), wg_ref[j], preferred_element_type=f32)
        t_r = jnp.tanh(by_tile(pre[:, :MXU_DIM], jnp.add, vb_scr[V_BR][:, cs]))
        t_i = jnp.tanh(by_tile(pre[:, MXU_DIM:], jnp.add, vb_scr[V_BI][:, cs]))
        log_a = by_tile(by_tile(t_r, jnp.multiply, half_scale[:, cs]), jnp.add, half_scale[:, cs])
        a_scr[j] = jnp.exp(log_a)
        th = jnp.tanh(log_a)
        a_half = -0.5 * th
        gate = jnp.where(a_half > 0.0, a_half * lax.rsqrt(a_half * (1.0 - th)), 0.0)
        u_scr[j] = gate * ((t_i + 1.0) * xg)

    def scan_t(t, h):
        t8 = pl.multiple_of(t * batch, batch)
        at = jnp.concatenate([a_scr[j, pl.ds(t8, batch), :] for j in range(N_GRP)], axis=1)
        ut = jnp.concatenate([u_scr[j, pl.ds(t8, batch), :] for j in range(N_GRP)], axis=1)
        h = at * h + ut
        for c in range(N_SLAB):
            hs_scr[c, pl.ds(t, batch, stride=PITCH), :] = h[:, c * LANES:(c + 1) * LANES]
        return h
    hst_scr[...] = lax.fori_loop(0, TS, scan_t, hst_scr[...], unroll=True)

    def lru_out_b(b, c):
        r0 = pl.multiple_of(b * TS, TS)
        p0 = pl.multiple_of(b * PITCH, 8)
        hb = jnp.concatenate([hs_scr[c2, pl.ds(p0, TS), :] for c2 in range(N_SLAB)], axis=1)
        v = hb * _silu_from_half(gl_scr[pl.ds(r0, TS), :])
        y = scale_by(v * _rms_scale(v), V_LG)
        ycat_scr[pl.ds(r0, TS), ATTN_WIDTH:] = y.astype(bf16)
        return c
    lax.fori_loop(0, batch, lru_out_b, 0, unroll=True)

    ga_scr[...] = ga_scr[...] + jnp.dot(ycat_scr[:, ATTN_WIDTH:], wol_ref[...], preferred_element_type=f32)

    def out_b(b, c):
        x2 = x_ref[b] + ga_scr[pl.ds(pl.multiple_of(b * TS, TS), TS), :]
        o_ref[b] = scale_by(x2 * _rms_scale(x2), V_FG)
        return c
    lax.fori_loop(0, batch, out_b, 0, unroll=True)


def _rope_tables(seq):
    half = ROT_DIM // 2
    pos = jnp.arange(seq, dtype=jnp.float32)
    inv_freq = ROPE_THETA ** (-jnp.arange(0, ROT_DIM, 2, dtype=jnp.float32) / ROT_DIM)
    ang = pos[:, None] * jnp.tile(inv_freq, LANES // half)[None, :]
    cos, sin = jnp.cos(ang), jnp.sin(ang)
    d = np.arange(LANES) % HEAD_DIM
    lo = (d < half)[None, :]
    hi = ((d >= half) & (d < ROT_DIM))[None, :]
    return jnp.stack([jnp.where(lo | hi, cos, 1.0),
                      jnp.where(hi, sin, 0.0),
                      jnp.where(lo, -sin, 0.0)])


def _group_major(a, axis):
    shp = a.shape
    a = a.reshape(shp[:axis] + (N_KV_HEADS, GROUP, HEAD_DIM) + shp[axis + 1:])
    return jnp.swapaxes(a, axis, axis + 1).reshape(shp)


def kernel(x, ln_gain, w_in, sinks, conv_w, conv_b, w_rgate, b_rgate, w_igate, b_igate,
           lru_lambda, attn_out_gain, lru_out_gain, w_out, final_gain):
    batch, seq, d_model = x.shape
    assert d_model == D_MODEL and seq % TS == 0 and ln_gain.shape[0] == 1 and N_SLOT == 3
    assert batch == SUBLANES
    f32, bf16 = jnp.float32, jnp.bfloat16
    rows = batch * TS

    col_scale = np.ones((1, OFF_XL + 2 * LRU_WIDTH), np.float32)
    col_scale[:, OFF_GA:OFF_XL] = 0.5
    col_scale[:, OFF_XL + LRU_WIDTH:] = 0.5
    w_all = (w_in[0] * col_scale).astype(bf16)
    wo_all = w_out[0].astype(bf16)
    old_col = np.arange(ATTN_WIDTH).reshape(N_KV_HEADS, GROUP, HEAD_DIM).transpose(1, 0, 2).reshape(-1)
    perm = np.zeros((ATTN_WIDTH, ATTN_WIDTH), np.float32)
    perm[old_col, np.arange(ATTN_WIDTH)] = 1.0
    perm = jnp.asarray(perm, dtype=bf16)
    wq = jnp.dot(w_all[:, :OFF_KV], perm, preferred_element_type=f32).astype(bf16)
    wga = jnp.dot(w_all[:, OFF_GA:OFF_XL], perm, preferred_element_type=f32).astype(bf16)
    woa = jnp.dot(perm.T, wo_all[:ATTN_WIDTH], preferred_element_type=f32).astype(bf16)
    wkv, wlru, wol = w_all[:, OFF_KV:OFF_GA], w_all[:, OFF_XL:], wo_all[ATTN_WIDTH:]

    def block_diag(wb):
        per = MXU_DIM // LRU_BLOCK_W
        wb = wb.reshape(N_GRP, per, LRU_BLOCK_W, 1, LRU_BLOCK_W)
        eye = np.eye(per, dtype=np.float32).reshape(1, per, 1, per, 1)
        return (wb * (0.5 * eye)).reshape(N_GRP, MXU_DIM, MXU_DIM)
    wg = jnp.concatenate([block_diag(w_rgate[0]), block_diag(w_igate[0])], axis=2).astype(bf16)

    vec_rows = [None] * (V_CW + CONV_W)
    vec_rows[V_LN], vec_rows[V_CB], vec_rows[V_BR] = ln_gain, conv_b, 0.5 * b_rgate
    vec_rows[V_BI], vec_rows[V_LAM], vec_rows[V_LG] = 0.5 * b_igate, lru_lambda, lru_out_gain
    vec_rows[V_AG], vec_rows[V_FG] = _group_major(attn_out_gain, 1), final_gain[None]
    vec_rows[V_CW:] = [conv_w[0][tap:tap + 1] for tap in range(CONV_W)]
    vecs = jnp.concatenate(vec_rows + [jnp.zeros((V_ROWS - len(vec_rows), D_MODEL), f32)], axis=0)

    rope_tbl = _rope_tables(seq)

    const = lambda *_: (0, 0)
    single = dict(pipeline_mode=pl.Buffered(1))
    grid_spec = pl.GridSpec(
        grid=(seq // TS,),
        in_specs=[
            pl.BlockSpec(memory_space=pltpu.SMEM),
            pl.BlockSpec((batch, TS, D_MODEL), lambda s: (0, s, 0)),
            pl.BlockSpec((3, TS, LANES), lambda s: (0, s, 0)),
            pl.BlockSpec((V_ROWS, D_MODEL), const, **single),
            pl.BlockSpec((D_MODEL, ATTN_WIDTH), const, **single),
            pl.BlockSpec((D_MODEL, 2 * KV_WIDTH), const, **single),
            pl.BlockSpec((D_MODEL, ATTN_WIDTH), const, **single),
            pl.BlockSpec((D_MODEL, 2 * LRU_WIDTH), const, **single),
            pl.BlockSpec((N_GRP, MXU_DIM, 2 * MXU_DIM), lambda s: (0, 0, 0), **single),
            pl.BlockSpec((ATTN_WIDTH, D_MODEL), const, **single),
            pl.BlockSpec((LRU_WIDTH, D_MODEL), const, **single),
        ],
        out_specs=pl.BlockSpec((batch, TS, D_MODEL), lambda s: (0, s, 0)),
        scratch_shapes=[
            pltpu.VMEM((rows, D_MODEL), bf16),
            pltpu.VMEM((rows, ATTN_WIDTH), bf16),
            pltpu.VMEM((batch, N_KV_HEADS, KEYS, LANES), bf16),
            pltpu.VMEM((batch, N_SLOT, KV_WIDTH, TS), bf16),
            pltpu.VMEM((rows, ATTN_WIDTH), f32),
            pltpu.VMEM((N_SLAB, batch * PITCH, LANES), f32),
            pltpu.VMEM((rows, LRU_WIDTH), f32),
            pltpu.VMEM((N_GRP, rows, MXU_DIM), f32),
            pltpu.VMEM((N_GRP, rows, MXU_DIM), f32),
            pltpu.VMEM((N_GRP, rows, MXU_DIM), f32),
            pltpu.VMEM((N_SLAB, batch * PITCH, LANES), f32),
            pltpu.VMEM((rows, 2 * ATTN_WIDTH), bf16),
            pltpu.VMEM((CONV_W - 1, batch, LRU_WIDTH), f32),
            pltpu.VMEM((batch, LRU_WIDTH), f32),
            pltpu.VMEM((V_ROWS, SUBLANES, D_MODEL), f32),
        ],
    )
    return pl.pallas_call(
        functools.partial(_layer_kernel, batch=batch),
        grid_spec=grid_spec,
        out_shape=jax.ShapeDtypeStruct(x.shape, x.dtype),
        compiler_params=pltpu.CompilerParams(
            dimension_semantics=("arbitrary",), vmem_limit_bytes=VMEM_LIMIT_BYTES),
        name="hymba_layer",
    )(sinks[0], x, rope_tbl, vecs, wq, wkv, wga, wlru, wg, woa, wol)
```

```python
import functools

import numpy as np
import jax
import jax.numpy as jnp
from jax import lax
from jax.experimental import pallas as pl
from jax.experimental.pallas import tpu as pltpu

D_MODEL = 1024
HEAD_DIM = 64
N_Q_HEADS = 16
N_KV_HEADS = 4
GROUP = N_Q_HEADS // N_KV_HEADS
ATTN_WIDTH = N_Q_HEADS * HEAD_DIM
KV_WIDTH = N_KV_HEADS * HEAD_DIM
WINDOW = 128
ROT_DIM = HEAD_DIM // 4
ROPE_THETA = 500000.0
NEG_INF = -1e30
LRU_WIDTH = 1024
LRU_BLOCKS = 16
LRU_BLOCK_W = LRU_WIDTH // LRU_BLOCKS
CONV_W = 4
LRU_C = 8.0
EPS = 1e-6

LANES = 128
SUBLANES = 8
MXU_DIM = 256
VMEM_LIMIT_BYTES = 58 * 1024 * 1024

TS = 64
N_SLOT = WINDOW // TS + 1
KEYS = N_SLOT * TS
LOG2E = 1.4426950408889634
Q_SCALE = HEAD_DIM ** -0.5 * LOG2E
PITCH = TS + 8
N_SLAB = LRU_WIDTH // LANES
N_GRP = LRU_WIDTH // MXU_DIM

OFF_KV, OFF_GA, OFF_XL = ATTN_WIDTH, ATTN_WIDTH + 2 * KV_WIDTH, 2 * ATTN_WIDTH + 2 * KV_WIDTH

V_LN, V_CB, V_BR, V_BI, V_LAM, V_AG, V_LG, V_FG, V_CW = 0, 1, 2, 3, 4, 5, 6, 7, 8
V_HS = V_CW + CONV_W
V_ROWS = 16


def _silu_from_half(hx):
    return hx * (jnp.tanh(hx) + 1.0)


def _rms_scale(v):
    return lax.rsqrt(jnp.mean(v * v, axis=-1, keepdims=True) + EPS)


def _layer_kernel(sinks_ref, x_ref, rope_ref, vecs_ref, wq_ref, wkv_ref, wga_ref, wlru_ref,
                  wg_ref, woa_ref, wol_ref, o_ref,
                  h_scr, q_scr, km_scr, vt_scr, ga_scr, xl_scr, gl_scr, xc_scr, a_scr, u_scr,
                  hs_scr, ycat_scr, cw_scr, hst_scr, vb_scr, *, batch):
    f32, bf16 = jnp.float32, jnp.bfloat16
    rows = batch * TS
    step = pl.program_id(0)

    @pl.when(step == 0)
    def _init_carries():
        km_scr[...] = jnp.zeros_like(km_scr)
        vt_scr[...] = jnp.zeros_like(vt_scr)
        cw_scr[...] = jnp.zeros_like(cw_scr)
        hst_scr[...] = jnp.zeros_like(hst_scr)
        for row in range(V_ROWS):
            vb_scr[row] = jnp.broadcast_to(vecs_ref[row:row + 1, :], (SUBLANES, D_MODEL))
        lam = vb_scr[V_LAM]
        softplus_neg_lam = jnp.maximum(-lam, 0.0) + jnp.log1p(jnp.exp(-jnp.abs(lam)))
        vb_scr[V_HS] = (-0.5 * LRU_C) * softplus_neg_lam

    def by_tile(v, op, tile):
        return op(v.reshape(-1, SUBLANES, v.shape[-1]), tile[None]).reshape(v.shape)

    def scale_by(v, row):
        return by_tile(v, jnp.multiply, vb_scr[row])

    def norm_b(b, c):
        xb = x_ref[b]
        hb = scale_by(xb * _rms_scale(xb), V_LN)
        h_scr[pl.ds(pl.multiple_of(b * TS, TS), TS), :] = hb.astype(bf16)
        return c
    lax.fori_loop(0, batch, norm_b, 0, unroll=True)

    hmat = h_scr[...]

    def proj(w_ref, c0, c1):
        return jnp.dot(hmat, w_ref[:, c0:c1], preferred_element_type=f32)

    k_tab = (rope_ref[0], rope_ref[1], rope_ref[2])
    q_tab = tuple(t * Q_SCALE for t in k_tab)

    def rope(z, tab):
        cos_t, sin_hi, sin_lo = tab
        return (z * cos_t[None] + pltpu.roll(z, 8, 2) * sin_hi[None]
                + pltpu.roll(z, LANES - 8, 2) * sin_lo[None])

    for g in range(GROUP):
        zq = proj(wq_ref, g * MXU_DIM, (g + 1) * MXU_DIM).reshape(batch, TS, MXU_DIM)
        for sl in range(MXU_DIM // LANES):
            r = rope(zq[:, :, sl * LANES:(sl + 1) * LANES], q_tab)
            c0 = g * MXU_DIM + sl * LANES
            q_scr[:, c0:c0 + LANES] = r.reshape(rows, LANES).astype(bf16)

    slot = lax.rem(step, N_SLOT)
    slot_row = pl.multiple_of(slot * TS, TS)
    zkv = proj(wkv_ref, 0, 2 * KV_WIDTH).reshape(batch, TS, 2 * KV_WIDTH)
    lane_half = lax.broadcasted_iota(jnp.int32, (1, 1, LANES), 2) // HEAD_DIM
    for sl in range(KV_WIDTH // LANES):
        r = rope(zkv[:, :, sl * LANES:(sl + 1) * LANES], k_tab).astype(bf16)
        for hh in range(LANES // HEAD_DIM):
            km_scr[:, sl * (LANES // HEAD_DIM) + hh, pl.ds(slot_row, TS), :] = jnp.where(
                lane_half == hh, r, jnp.zeros_like(r))
    for b in range(batch):
        vt_scr[b, slot] = zkv[b, :, KV_WIDTH:].T.astype(bf16)

    half = ATTN_WIDTH // 2
    for c in range(2):
        ga_scr[:, c * half:(c + 1) * half] = proj(wga_ref, c * half, (c + 1) * half)
        gl_scr[:, c * half:(c + 1) * half] = proj(wlru_ref, LRU_WIDTH + c * half, LRU_WIDTH + (c + 1) * half)
        zx = proj(wlru_ref, c * half, (c + 1) * half)
        for sl in range(half // LANES):
            for b in range(batch):
                xl_scr[c * (half // LANES) + sl, b * PITCH:b * PITCH + TS, :] = (
                    zx[b * TS:(b + 1) * TS, sl * LANES:(sl + 1) * LANES])

    jj = lax.broadcasted_iota(jnp.int32, (TS, GROUP * TS), 0)
    ii = lax.broadcasted_iota(jnp.int32, (TS, GROUP * TS), 1) % TS
    blocks = []
    for sg in range(N_SLOT):
        age = lax.rem(step + (N_SLOT - sg), N_SLOT)
        valid = ((age == 0) & (jj <= ii)) | (age == 1) | ((age == 2) & (jj > ii))
        blocks.append(jnp.where(valid & (step >= age), 0.0, NEG_INF).astype(f32))
    bias_t = jnp.concatenate(blocks, axis=0)
    lane_g = lax.broadcasted_iota(jnp.int32, (1, GROUP * TS), 1) // TS
    sink_rows = []
    for k in range(N_KV_HEADS):
        row = jnp.zeros((1, GROUP * TS), f32)
        for g in range(GROUP):
            row = jnp.where(lane_g == g, sinks_ref[k * GROUP + g] * LOG2E, row)
        sink_rows.append(row)

    def attn_b(b, c):
        r0 = pl.multiple_of(b * TS, TS)
        q_pair = [jnp.concatenate(
            [q_scr[pl.ds(r0, TS), g * MXU_DIM + sl * LANES:g * MXU_DIM + (sl + 1) * LANES]
             for g in range(GROUP)], axis=0) for sl in range(KV_WIDTH // LANES)]
        o_heads = []
        for k in range(N_KV_HEADS):
            st = lax.dot_general(km_scr[b, k], q_pair[k // 2], (((1,), (1,)), ((), ())),
                                 preferred_element_type=f32)
            st = st + bias_t
            m = jnp.maximum(jnp.max(st, axis=0, keepdims=True), sink_rows[k])
            p = jnp.exp2(st - m)
            den = jnp.sum(p, axis=0, keepdims=True) + jnp.exp2(sink_rows[k] - m)
            pb = p.astype(bf16)
            o = jnp.zeros((HEAD_DIM, GROUP * TS), f32)
            for sg in range(N_SLOT):
                o = o + jnp.dot(vt_scr[b, sg, k * HEAD_DIM:(k + 1) * HEAD_DIM, :],
                                pb[sg * TS:(sg + 1) * TS], preferred_element_type=f32)
            inv_root = lax.rsqrt(den)
            o_heads.append(o * (inv_root * inv_root))
        o_rows = jnp.concatenate(o_heads, axis=0).T
        ob = jnp.concatenate([o_rows[g * TS:(g + 1) * TS] for g in range(GROUP)], axis=1)
        v = ob * _silu_from_half(ga_scr[pl.ds(r0, TS), :])
        y = scale_by(v * _rms_scale(v), V_AG)
        ycat_scr[pl.ds(r0, TS), 0:ATTN_WIDTH] = y.astype(bf16)
        return c
    lax.fori_loop(0, batch, attn_b, 0, unroll=8)

    def conv_t(t, carry):
        x1, x2, x3 = carry
        xt = jnp.concatenate(
            [xl_scr[c, pl.ds(t, batch, stride=PITCH), :] for c in range(N_SLAB)], axis=1)
        u = (vb_scr[V_CB] + vb_scr[V_CW + 3] * xt + vb_scr[V_CW + 2] * x1
             + vb_scr[V_CW + 1] * x2 + vb_scr[V_CW] * x3)
        t8 = pl.multiple_of(t * batch, batch)
        for j in range(N_GRP):
            xc_scr[j, pl.ds(t8, batch), :] = u[:, j * MXU_DIM:(j + 1) * MXU_DIM]
        return xt, x1, x2
    tail = lax.fori_loop(0, TS, conv_t, (cw_scr[0], cw_scr[1], cw_scr[2]), unroll=True)
    for i in range(CONV_W - 1):
        cw_scr[i] = tail[i]

    half_scale = vb_scr[V_HS]
    for j in range(N_GRP):
        cs = slice(j * MXU_DIM, (j + 1) * MXU_DIM)
        xg = xc_scr[j]
        pre = jnp.dot(xg.astype(bf16), wg_ref[j], preferred_element_type=f32)
        t_r = jnp.tanh(by_tile(pre[:, :MXU_DIM], jnp.add, vb_scr[V_BR][:, cs]))
        t_i = jnp.tanh(by_tile(pre[:, MXU_DIM:], jnp.add, vb_scr[V_BI][:, cs]))
        log_a = by_tile(by_tile(t_r, jnp.multiply, half_scale[:, cs]), jnp.add, half_scale[:, cs])
        a_scr[j] = jnp.exp(log_a)
        th = jnp.tanh(log_a)
        a_half = -0.5 * th
        gate = jnp.where(a_half > 0.0, a_half * lax.rsqrt(a_half * (1.0 - th)), 0.0)
        u_scr[j] = gate * ((t_i + 1.0) * xg)

    def scan_t(t, h):
        t8 = pl.multiple_of(t * batch, batch)
        at = jnp.concatenate([a_scr[j, pl.ds(t8, batch), :] for j in range(N_GRP)], axis=1)
        ut = jnp.concatenate([u_scr[j, pl.ds(t8, batch), :] for j in range(N_GRP)], axis=1)
        h = at * h + ut
        for c in range(N_SLAB):
            hs_scr[c, pl.ds(t, batch, stride=PITCH), :] = h[:, c * LANES:(c + 1) * LANES]
        return h
    hst_scr[...] = lax.fori_loop(0, TS, scan_t, hst_scr[...], unroll=True)

    def lru_out_b(b, c):
        r0 = pl.multiple_of(b * TS, TS)
        p0 = pl.multiple_of(b * PITCH, 8)
        hb = jnp.concatenate([hs_scr[c2, pl.ds(p0, TS), :] for c2 in range(N_SLAB)], axis=1)
        v = hb * _silu_from_half(gl_scr[pl.ds(r0, TS), :])
        y = scale_by(v * _rms_scale(v), V_LG)
        ycat_scr[pl.ds(r0, TS), ATTN_WIDTH:] = y.astype(bf16)
        return c
    lax.fori_loop(0, batch, lru_out_b, 0, unroll=True)

    ga_scr[...] = (
        jnp.dot(ycat_scr[:, :ATTN_WIDTH], woa_ref[...], preferred_element_type=f32)
        + jnp.dot(ycat_scr[:, ATTN_WIDTH:], wol_ref[...], preferred_element_type=f32))

    def out_b(b, c):
        x2 = x_ref[b] + ga_scr[pl.ds(pl.multiple_of(b * TS, TS), TS), :]
        o_ref[b] = scale_by(x2 * _rms_scale(x2), V_FG)
        return c
    lax.fori_loop(0, batch, out_b, 0, unroll=True)


def _rope_tables(seq):
    half = ROT_DIM // 2
    pos = jnp.arange(seq, dtype=jnp.float32)
    inv_freq = ROPE_THETA ** (-jnp.arange(0, ROT_DIM, 2, dtype=jnp.float32) / ROT_DIM)
    ang = pos[:, None] * jnp.tile(inv_freq, LANES // half)[None, :]
    cos, sin = jnp.cos(ang), jnp.sin(ang)
    d = np.arange(LANES) % HEAD_DIM
    lo = (d < half)[None, :]
    hi = ((d >= half) & (d < ROT_DIM))[None, :]
    return jnp.stack([jnp.where(lo | hi, cos, 1.0),
                      jnp.where(hi, sin, 0.0),
                      jnp.where(lo, -sin, 0.0)])


def _group_major(a, axis):
    shp = a.shape
    a = a.reshape(shp[:axis] + (N_KV_HEADS, GROUP, HEAD_DIM) + shp[axis + 1:])
    return jnp.swapaxes(a, axis, axis + 1).reshape(shp)


def kernel(x, ln_gain, w_in, sinks, conv_w, conv_b, w_rgate, b_rgate, w_igate, b_igate,
           lru_lambda, attn_out_gain, lru_out_gain, w_out, final_gain):
    batch, seq, d_model = x.shape
    assert d_model == D_MODEL and seq % TS == 0 and ln_gain.shape[0] == 1 and N_SLOT == 3
    assert batch == SUBLANES
    f32, bf16 = jnp.float32, jnp.bfloat16
    rows = batch * TS

    col_scale = np.ones((1, OFF_XL + 2 * LRU_WIDTH), np.float32)
    col_scale[:, OFF_GA:OFF_XL] = 0.5
    col_scale[:, OFF_XL + LRU_WIDTH:] = 0.5
    w_all = (w_in[0] * col_scale).astype(bf16)
    wo_all = w_out[0].astype(bf16)
    old_col = np.arange(ATTN_WIDTH).reshape(N_KV_HEADS, GROUP, HEAD_DIM).transpose(1, 0, 2).reshape(-1)
    perm = np.zeros((ATTN_WIDTH, ATTN_WIDTH), np.float32)
    perm[old_col, np.arange(ATTN_WIDTH)] = 1.0
    perm = jnp.asarray(perm, dtype=bf16)
    wq = jnp.dot(w_all[:, :OFF_KV], perm, preferred_element_type=f32).astype(bf16)
    wga = jnp.dot(w_all[:, OFF_GA:OFF_XL], perm, preferred_element_type=f32).astype(bf16)
    woa = jnp.dot(perm.T, wo_all[:ATTN_WIDTH], preferred_element_type=f32).astype(bf16)
    wkv, wlru, wol = w_all[:, OFF_KV:OFF_GA], w_all[:, OFF_XL:], wo_all[ATTN_WIDTH:]

    def block_diag(wb):
        per = MXU_DIM // LRU_BLOCK_W
        wb = wb.reshape(N_GRP, per, LRU_BLOCK_W, 1, LRU_BLOCK_W)
        eye = np.eye(per, dtype=np.float32).reshape(1, per, 1, per, 1)
        return (wb * (0.5 * eye)).reshape(N_GRP, MXU_DIM, MXU_DIM)
    wg = jnp.concatenate([block_diag(w_rgate[0]), block_diag(w_igate[0])], axis=2).astype(bf16)

    vec_rows = [None] * (V_CW + CONV_W)
    vec_rows[V_LN], vec_rows[V_CB], vec_rows[V_BR] = ln_gain, conv_b, 0.5 * b_rgate
    vec_rows[V_BI], vec_rows[V_LAM], vec_rows[V_LG] = 0.5 * b_igate, lru_lambda, lru_out_gain
    vec_rows[V_AG], vec_rows[V_FG] = _group_major(attn_out_gain, 1), final_gain[None]
    vec_rows[V_CW:] = [conv_w[0][tap:tap + 1] for tap in range(CONV_W)]
    vecs = jnp.concatenate(vec_rows + [jnp.zeros((V_ROWS - len(vec_rows), D_MODEL), f32)], axis=0)

    rope_tbl = _rope_tables(seq)

    const = lambda *_: (0, 0)
    single = dict(pipeline_mode=pl.Buffered(1))
    grid_spec = pl.GridSpec(
        grid=(seq // TS,),
        in_specs=[
            pl.BlockSpec(memory_space=pltpu.SMEM),
            pl.BlockSpec((batch, TS, D_MODEL), lambda s: (0, s, 0)),
            pl.BlockSpec((3, TS, LANES), lambda s: (0, s, 0)),
            pl.BlockSpec((V_ROWS, D_MODEL), const, **single),
            pl.BlockSpec((D_MODEL, ATTN_WIDTH), const, **single),
            pl.BlockSpec((D_MODEL, 2 * KV_WIDTH), const, **single),
            pl.BlockSpec((D_MODEL, ATTN_WIDTH), const, **single),
            pl.BlockSpec((D_MODEL, 2 * LRU_WIDTH), const, **single),
            pl.BlockSpec((N_GRP, MXU_DIM, 2 * MXU_DIM), lambda s: (0, 0, 0), **single),
            pl.BlockSpec((ATTN_WIDTH, D_MODEL), const, **single),
            pl.BlockSpec((LRU_WIDTH, D_MODEL), const, **single),
        ],
        out_specs=pl.BlockSpec((batch, TS, D_MODEL), lambda s: (0, s, 0)),
        scratch_shapes=[
            pltpu.VMEM((rows, D_MODEL), bf16),
            pltpu.VMEM((rows, ATTN_WIDTH), bf16),
            pltpu.VMEM((batch, N_KV_HEADS, KEYS, LANES), bf16),
            pltpu.VMEM((batch, N_SLOT, KV_WIDTH, TS), bf16),
            pltpu.VMEM((rows, ATTN_WIDTH), f32),
            pltpu.VMEM((N_SLAB, batch * PITCH, LANES), f32),
            pltpu.VMEM((rows, LRU_WIDTH), f32),
            pltpu.VMEM((N_GRP, rows, MXU_DIM), f32),
            pltpu.VMEM((N_GRP, rows, MXU_DIM), f32),
            pltpu.VMEM((N_GRP, rows, MXU_DIM), f32),
            pltpu.VMEM((N_SLAB, batch * PITCH, LANES), f32),
            pltpu.VMEM((rows, 2 * ATTN_WIDTH), bf16),
            pltpu.VMEM((CONV_W - 1, batch, LRU_WIDTH), f32),
            pltpu.VMEM((batch, LRU_WIDTH), f32),
            pltpu.VMEM((V_ROWS, SUBLANES, D_MODEL), f32),
        ],
    )
    return pl.pallas_call(
        functools.partial(_layer_kernel, batch=batch),
        grid_spec=grid_spec,
        out_shape=jax.ShapeDtypeStruct(x.shape, x.dtype),
        compiler_params=pltpu.CompilerParams(
            dimension_semantics=("arbitrary",), vmem_limit_bytes=VMEM_LIMIT_BYTES),
        name="hymba_layer",
    )(sinks[0], x, rope_tbl, vecs, wq, wkv, wga, wlru, wg, woa, wol)
```

```python
import functools

import numpy as np
import jax
import jax.numpy as jnp
from jax import lax
from jax.experimental import pallas as pl
from jax.experimental.pallas import tpu as pltpu

D_MODEL = 1024
HEAD_DIM = 64
N_Q_HEADS = 16
N_KV_HEADS = 4
GROUP = N_Q_HEADS // N_KV_HEADS
ATTN_WIDTH = N_Q_HEADS * HEAD_DIM
KV_WIDTH = N_KV_HEADS * HEAD_DIM
WINDOW = 128
ROT_DIM = HEAD_DIM // 4
ROPE_THETA = 500000.0
NEG_INF = -1e30
LRU_WIDTH = 1024
LRU_BLOCKS = 16
LRU_BLOCK_W = LRU_WIDTH // LRU_BLOCKS
CONV_W = 4
LRU_C = 8.0
EPS = 1e-6

LANES = 128
SUBLANES = 8
MXU_DIM = 256
VMEM_LIMIT_BYTES = 58 * 1024 * 1024

TS = 64
N_SLOT = WINDOW // TS + 1
KEYS = N_SLOT * TS
LOG2E = 1.4426950408889634
Q_SCALE = HEAD_DIM ** -0.5 * LOG2E
PITCH = TS + 8
N_SLAB = LRU_WIDTH // LANES
N_GRP = LRU_WIDTH // MXU_DIM

OFF_KV, OFF_GA, OFF_XL = ATTN_WIDTH, ATTN_WIDTH + 2 * KV_WIDTH, 2 * ATTN_WIDTH + 2 * KV_WIDTH

V_LN, V_CB, V_BR, V_BI, V_LAM, V_AG, V_LG, V_FG, V_CW = 0, 1, 2, 3, 4, 5, 6, 7, 8
V_HS = V_CW + CONV_W
V_ROWS = 16


def _silu_from_half(hx):
    return hx * (jnp.tanh(hx) + 1.0)


def _rms_scale(v):
    return lax.rsqrt(jnp.mean(v * v, axis=-1, keepdims=True) + EPS)


def _layer_kernel(sinks_ref, x_ref, rope_ref, vecs_ref, wq_ref, wkv_ref, wga_ref, wlru_ref,
                  wg_ref, woa_ref, wol_ref, o_ref,
                  h_scr, q_scr, km_scr, vt_scr, ga_scr, xl_scr, gl_scr, xc_scr, a_scr, u_scr,
                  hs_scr, ycat_scr, cw_scr, hst_scr, vb_scr, *, batch):
    f32, bf16 = jnp.float32, jnp.bfloat16
    rows = batch * TS
    step = pl.program_id(0)

    @pl.when(step == 0)
    def _init_carries():
        km_scr[...] = jnp.zeros_like(km_scr)
        vt_scr[...] = jnp.zeros_like(vt_scr)
        cw_scr[...] = jnp.zeros_like(cw_scr)
        hst_scr[...] = jnp.zeros_like(hst_scr)
        for row in range(V_ROWS):
            vb_scr[row] = jnp.broadcast_to(vecs_ref[row:row + 1, :], (SUBLANES, D_MODEL))
        lam = vb_scr[V_LAM]
        softplus_neg_lam = jnp.maximum(-lam, 0.0) + jnp.log1p(jnp.exp(-jnp.abs(lam)))
        vb_scr[V_HS] = (-0.5 * LRU_C) * softplus_neg_lam

    def by_tile(v, op, tile):
        return op(v.reshape(-1, SUBLANES, v.shape[-1]), tile[None]).reshape(v.shape)

    def scale_by(v, row):
        return by_tile(v, jnp.multiply, vb_scr[row])

    def norm_b(b, c):
        xb = x_ref[b]
        hb = scale_by(xb * _rms_scale(xb), V_LN)
        h_scr[pl.ds(pl.multiple_of(b * TS, TS), TS), :] = hb.astype(bf16)
        return c
    lax.fori_loop(0, batch, norm_b, 0, unroll=True)

    hmat = h_scr[...]

    def proj(w_ref, c0, c1):
        return jnp.dot(hmat, w_ref[:, c0:c1], preferred_element_type=f32)

    k_tab = (rope_ref[0], rope_ref[1], rope_ref[2])
    q_tab = tuple(t * Q_SCALE for t in k_tab)

    def rope(z, tab):
        cos_t, sin_hi, sin_lo = tab
        return (z * cos_t[None] + pltpu.roll(z, 8, 2) * sin_hi[None]
                + pltpu.roll(z, LANES - 8, 2) * sin_lo[None])

    for g in range(GROUP):
        zq = proj(wq_ref, g * MXU_DIM, (g + 1) * MXU_DIM).reshape(batch, TS, MXU_DIM)
        for sl in range(MXU_DIM // LANES):
            r = rope(zq[:, :, sl * LANES:(sl + 1) * LANES], q_tab)
            c0 = g * MXU_DIM + sl * LANES
            q_scr[:, c0:c0 + LANES] = r.reshape(rows, LANES).astype(bf16)

    slot = lax.rem(step, N_SLOT)
    slot_row = pl.multiple_of(slot * TS, TS)
    zkv = proj(wkv_ref, 0, 2 * KV_WIDTH).reshape(batch, TS, 2 * KV_WIDTH)
    lane_half = lax.broadcasted_iota(jnp.int32, (1, 1, LANES), 2) // HEAD_DIM
    for sl in range(KV_WIDTH // LANES):
        r = rope(zkv[:, :, sl * LANES:(sl + 1) * LANES], k_tab).astype(bf16)
        for hh in range(LANES // HEAD_DIM):
            km_scr[:, sl * (LANES // HEAD_DIM) + hh, pl.ds(slot_row, TS), :] = jnp.where(
                lane_half == hh, r, jnp.zeros_like(r))
    for b in range(batch):
        vt_scr[b, slot] = zkv[b, :, KV_WIDTH:].T.astype(bf16)

    half = ATTN_WIDTH // 2
    for c in range(2):
        ga_scr[:, c * half:(c + 1) * half] = proj(wga_ref, c * half, (c + 1) * half)
        gl_scr[:, c * half:(c + 1) * half] = proj(wlru_ref, LRU_WIDTH + c * half, LRU_WIDTH + (c + 1) * half)
        zx = proj(wlru_ref, c * half, (c + 1) * half)
        for sl in range(half // LANES):
            for b in range(batch):
                xl_scr[c * (half // LANES) + sl, b * PITCH:b * PITCH + TS, :] = (
                    zx[b * TS:(b + 1) * TS, sl * LANES:(sl + 1) * LANES])

    jj = lax.broadcasted_iota(jnp.int32, (TS, GROUP * TS), 0)
    ii = lax.broadcasted_iota(jnp.int32, (TS, GROUP * TS), 1) % TS
    blocks = []
    for sg in range(N_SLOT):
        age = lax.rem(step + (N_SLOT - sg), N_SLOT)
        valid = ((age == 0) & (jj <= ii)) | (age == 1) | ((age == 2) & (jj > ii))
        blocks.append(jnp.where(valid & (step >= age), 0.0, NEG_INF).astype(f32))
    bias_t = jnp.concatenate(blocks, axis=0)
    lane_g = lax.broadcasted_iota(jnp.int32, (1, GROUP * TS), 1) // TS
    sink_rows = []
    for k in range(N_KV_HEADS):
        row = jnp.zeros((1, GROUP * TS), f32)
        for g in range(GROUP):
            row = jnp.where(lane_g == g, sinks_ref[k * GROUP + g] * LOG2E, row)
        sink_rows.append(row)

    def attn_b(b, c):
        r0 = pl.multiple_of(b * TS, TS)
        q_pair = [jnp.concatenate(
            [q_scr[pl.ds(r0, TS), g * MXU_DIM + sl * LANES:g * MXU_DIM + (sl + 1) * LANES]
             for g in range(GROUP)], axis=0) for sl in range(KV_WIDTH // LANES)]
        o_heads = []
        for k in range(N_KV_HEADS):
            st = lax.dot_general(km_scr[b, k], q_pair[k // 2], (((1,), (1,)), ((), ())),
                                 preferred_element_type=f32)
            st = st + bias_t
            m = jnp.maximum(jnp.max(st, axis=0, keepdims=True), sink_rows[k])
            p = jnp.exp2(st - m)
            den = jnp.sum(p, axis=0, keepdims=True) + jnp.exp2(sink_rows[k] - m)
            pb = p.astype(bf16)
            o = jnp.zeros((HEAD_DIM, GROUP * TS), f32)
            for sg in range(N_SLOT):
                o = o + jnp.dot(vt_scr[b, sg, k * HEAD_DIM:(k + 1) * HEAD_DIM, :],
                                pb[sg * TS:(sg + 1) * TS], preferred_element_type=f32)
            o_heads.append(o * (1.0 / den))
        o_rows = jnp.concatenate(o_heads, axis=0).T
        ob = jnp.concatenate([o_rows[g * TS:(g + 1) * TS] for g in range(GROUP)], axis=1)
        v = ob * _silu_from_half(ga_scr[pl.ds(r0, TS), :])
        y = scale_by(v * _rms_scale(v), V_AG)
        ycat_scr[pl.ds(r0, TS), 0:ATTN_WIDTH] = y.astype(bf16)
        return c
    lax.fori_loop(0, batch, attn_b, 0, unroll=8)

    def conv_t(t, carry):
        x1, x2, x3 = carry
        xt = jnp.concatenate(
            [xl_scr[c, pl.ds(t, batch, stride=PITCH), :] for c in range(N_SLAB)], axis=1)
        u = (vb_scr[V_CB] + vb_scr[V_CW + 3] * xt + vb_scr[V_CW + 2] * x1
             + vb_scr[V_CW + 1] * x2 + vb_scr[V_CW] * x3)
        t8 = pl.multiple_of(t * batch, batch)
        for j in range(N_GRP):
            xc_scr[j, pl.ds(t8, batch), :] = u[:, j * MXU_DIM:(j + 1) * MXU_DIM]
        return xt, x1, x2
    tail = lax.fori_loop(0, TS, conv_t, (cw_scr[0], cw_scr[1], cw_scr[2]), unroll=True)
    for i in range(CONV_W - 1):
        cw_scr[i] = tail[i]

    half_scale = vb_scr[V_HS]
    for j in range(N_GRP):
        cs = slice(j * MXU_DIM, (j + 1) * MXU_DIM)
        xg = xc_scr[j]
        pre = jnp.dot(xg.astype(bf16), wg_ref[j], preferred_element_type=f32)
        t_r = jnp.tanh(by_tile(pre[:, :MXU_DIM], jnp.add, vb_scr[V_BR][:, cs]))
        t_i = jnp.tanh(by_tile(pre[:, MXU_DIM:], jnp.add, vb_scr[V_BI][:, cs]))
        log_a = by_tile(by_tile(t_r, jnp.multiply, half_scale[:, cs]), jnp.add, half_scale[:, cs])
        a_scr[j] = jnp.exp(log_a)
        th = jnp.tanh(log_a)
        a_half = -0.5 * th
        gate = jnp.where(a_half > 0.0, a_half * lax.rsqrt(a_half * (1.0 - th)), 0.0)
        u_scr[j] = gate * ((t_i + 1.0) * xg)

    def scan_t(t, h):
        t8 = pl.multiple_of(t * batch, batch)
        at = jnp.concatenate([a_scr[j, pl.ds(t8, batch), :] for j in range(N_GRP)], axis=1)
        ut = jnp.concatenate([u_scr[j, pl.ds(t8, batch), :] for j in range(N_GRP)], axis=1)
        h = at * h + ut
        for c in range(N_SLAB):
            hs_scr[c, pl.ds(t, batch, stride=PITCH), :] = h[:, c * LANES:(c + 1) * LANES]
        return h
    hst_scr[...] = lax.fori_loop(0, TS, scan_t, hst_scr[...], unroll=True)

    def lru_out_b(b, c):
        r0 = pl.multiple_of(b * TS, TS)
        p0 = pl.multiple_of(b * PITCH, 8)
        hb = jnp.concatenate([hs_scr[c2, pl.ds(p0, TS), :] for c2 in range(N_SLAB)], axis=1)
        v = hb * _silu_from_half(gl_scr[pl.ds(r0, TS), :])
        y = scale_by(v * _rms_scale(v), V_LG)
        ycat_scr[pl.ds(r0, TS), ATTN_WIDTH:] = y.astype(bf16)
        return c
    lax.fori_loop(0, batch, lru_out_b, 0, unroll=True)

    for r0 in range(0, rows, rows // 2):
        rs = slice(r0, r0 + rows // 2)
        ga_scr[rs, :] = (
            jnp.dot(ycat_scr[rs, :ATTN_WIDTH], woa_ref[...], preferred_element_type=f32)
            + jnp.dot(ycat_scr[rs, ATTN_WIDTH:], wol_ref[...], preferred_element_type=f32))

    def out_b(b, c):
        x2 = x_ref[b] + ga_scr[pl.ds(pl.multiple_of(b * TS, TS), TS), :]
        o_ref[b] = scale_by(x2 * _rms_scale(x2), V_FG)
        return c
    lax.fori_loop(0, batch, out_b, 0, unroll=True)


def _rope_tables(seq):
    half = ROT_DIM // 2
    pos = jnp.arange(seq, dtype=jnp.float32)
    inv_freq = ROPE_THETA ** (-jnp.arange(0, ROT_DIM, 2, dtype=jnp.float32) / ROT_DIM)
    ang = pos[:, None] * jnp.tile(inv_freq, LANES // half)[None, :]
    cos, sin = jnp.cos(ang), jnp.sin(ang)
    d = np.arange(LANES) % HEAD_DIM
    lo = (d < half)[None, :]
    hi = ((d >= half) & (d < ROT_DIM))[None, :]
    return jnp.stack([jnp.where(lo | hi, cos, 1.0),
                      jnp.where(hi, sin, 0.0),
                      jnp.where(lo, -sin, 0.0)])


def _group_major(a, axis):
    shp = a.shape
    a = a.reshape(shp[:axis] + (N_KV_HEADS, GROUP, HEAD_DIM) + shp[axis + 1:])
    return jnp.swapaxes(a, axis, axis + 1).reshape(shp)


def kernel(x, ln_gain, w_in, sinks, conv_w, conv_b, w_rgate, b_rgate, w_igate, b_igate,
           lru_lambda, attn_out_gain, lru_out_gain, w_out, final_gain):
    batch, seq, d_model = x.shape
    assert d_model == D_MODEL and seq % TS == 0 and ln_gain.shape[0] == 1 and N_SLOT == 3
    assert batch == SUBLANES
    f32, bf16 = jnp.float32, jnp.bfloat16
    rows = batch * TS

    col_scale = np.ones((1, OFF_XL + 2 * LRU_WIDTH), np.float32)
    col_scale[:, OFF_GA:OFF_XL] = 0.5
    col_scale[:, OFF_XL + LRU_WIDTH:] = 0.5
    w_all = (w_in[0] * col_scale).astype(bf16)
    wo_all = w_out[0].astype(bf16)
    old_col = np.arange(ATTN_WIDTH).reshape(N_KV_HEADS, GROUP, HEAD_DIM).transpose(1, 0, 2).reshape(-1)
    perm = np.zeros((ATTN_WIDTH, ATTN_WIDTH), np.float32)
    perm[old_col, np.arange(ATTN_WIDTH)] = 1.0
    perm = jnp.asarray(perm, dtype=bf16)
    wq = jnp.dot(w_all[:, :OFF_KV], perm, preferred_element_type=f32).astype(bf16)
    wga = jnp.dot(w_all[:, OFF_GA:OFF_XL], perm, preferred_element_type=f32).astype(bf16)
    woa = jnp.dot(perm.T, wo_all[:ATTN_WIDTH], preferred_element_type=f32).astype(bf16)
    wkv, wlru, wol = w_all[:, OFF_KV:OFF_GA], w_all[:, OFF_XL:], wo_all[ATTN_WIDTH:]

    def block_diag(wb):
        per = MXU_DIM // LRU_BLOCK_W
        wb = wb.reshape(N_GRP, per, LRU_BLOCK_W, 1, LRU_BLOCK_W)
        eye = np.eye(per, dtype=np.float32).reshape(1, per, 1, per, 1)
        return (wb * (0.5 * eye)).reshape(N_GRP, MXU_DIM, MXU_DIM)
    wg = jnp.concatenate([block_diag(w_rgate[0]), block_diag(w_igate[0])], axis=2).astype(bf16)

    vec_rows = [None] * (V_CW + CONV_W)
    vec_rows[V_LN], vec_rows[V_CB], vec_rows[V_BR] = ln_gain, conv_b, 0.5 * b_rgate
    vec_rows[V_BI], vec_rows[V_LAM], vec_rows[V_LG] = 0.5 * b_igate, lru_lambda, lru_out_gain
    vec_rows[V_AG], vec_rows[V_FG] = _group_major(attn_out_gain, 1), final_gain[None]
    vec_rows[V_CW:] = [conv_w[0][tap:tap + 1] for tap in range(CONV_W)]
    vecs = jnp.concatenate(vec_rows + [jnp.zeros((V_ROWS - len(vec_rows), D_MODEL), f32)], axis=0)

    rope_tbl = _rope_tables(seq)

    const = lambda *_: (0, 0)
    single = dict(pipeline_mode=pl.Buffered(1))
    grid_spec = pl.GridSpec(
        grid=(seq // TS,),
        in_specs=[
            pl.BlockSpec(memory_space=pltpu.SMEM),
            pl.BlockSpec((batch, TS, D_MODEL), lambda s: (0, s, 0)),
            pl.BlockSpec((3, TS, LANES), lambda s: (0, s, 0)),
            pl.BlockSpec((V_ROWS, D_MODEL), const, **single),
            pl.BlockSpec((D_MODEL, ATTN_WIDTH), const, **single),
            pl.BlockSpec((D_MODEL, 2 * KV_WIDTH), const, **single),
            pl.BlockSpec((D_MODEL, ATTN_WIDTH), const, **single),
            pl.BlockSpec((D_MODEL, 2 * LRU_WIDTH), const, **single),
            pl.BlockSpec((N_GRP, MXU_DIM, 2 * MXU_DIM), lambda s: (0, 0, 0), **single),
            pl.BlockSpec((ATTN_WIDTH, D_MODEL), const, **single),
            pl.BlockSpec((LRU_WIDTH, D_MODEL), const, **single),
        ],
        out_specs=pl.BlockSpec((batch, TS, D_MODEL), lambda s: (0, s, 0)),
        scratch_shapes=[
            pltpu.VMEM((rows, D_MODEL), bf16),
            pltpu.VMEM((rows, ATTN_WIDTH), bf16),
            pltpu.VMEM((batch, N_KV_HEADS, KEYS, LANES), bf16),
            pltpu.VMEM((batch, N_SLOT, KV_WIDTH, TS), bf16),
            pltpu.VMEM((rows, ATTN_WIDTH), f32),
            pltpu.VMEM((N_SLAB, batch * PITCH, LANES), f32),
            pltpu.VMEM((rows, LRU_WIDTH), f32),
            pltpu.VMEM((N_GRP, rows, MXU_DIM), f32),
            pltpu.VMEM((N_GRP, rows, MXU_DIM), f32),
            pltpu.VMEM((N_GRP, rows, MXU_DIM), f32),
            pltpu.VMEM((N_SLAB, batch * PITCH, LANES), f32),
            pltpu.VMEM((rows, 2 * ATTN_WIDTH), bf16),
            pltpu.VMEM((CONV_W - 1, batch, LRU_WIDTH), f32),
            pltpu.VMEM((batch, LRU_WIDTH), f32),
            pltpu.VMEM((V_ROWS, SUBLANES, D_MODEL), f32),
        ],
    )
    return pl.pallas_call(
        functools.partial(_layer_kernel, batch=batch),
        grid_spec=grid_spec,
        out_shape=jax.ShapeDtypeStruct(x.shape, x.dtype),
        compiler_params=pltpu.CompilerParams(
            dimension_semantics=("arbitrary",), vmem_limit_bytes=VMEM_LIMIT_BYTES),
        name="hymba_layer",
    )(sinks[0], x, rope_tbl, vecs, wq, wkv, wga, wlru, wg, woa, wol)
```

```python
import functools

import numpy as np
import jax
import jax.numpy as jnp
from jax import lax
from jax.experimental import pallas as pl
from jax.experimental.pallas import tpu as pltpu

D_MODEL = 1024
HEAD_DIM = 64
N_Q_HEADS = 16
N_KV_HEADS = 4
GROUP = N_Q_HEADS // N_KV_HEADS
ATTN_WIDTH = N_Q_HEADS * HEAD_DIM
KV_WIDTH = N_KV_HEADS * HEAD_DIM
WINDOW = 128
ROT_DIM = HEAD_DIM // 4
ROPE_THETA = 500000.0
NEG_INF = -1e30
LRU_WIDTH = 1024
LRU_BLOCKS = 16
LRU_BLOCK_W = LRU_WIDTH // LRU_BLOCKS
CONV_W = 4
LRU_C = 8.0
EPS = 1e-6

LANES = 128
SUBLANES = 8
MXU_DIM = 256
VMEM_LIMIT_BYTES = 58 * 1024 * 1024

TS = 64
N_SLOT = WINDOW // TS + 1
KEYS = N_SLOT * TS
LOG2E = 1.4426950408889634
Q_SCALE = HEAD_DIM ** -0.5 * LOG2E
PITCH = TS + 8
N_SLAB = LRU_WIDTH // LANES
N_GRP = LRU_WIDTH // MXU_DIM

OFF_KV, OFF_GA, OFF_XL = ATTN_WIDTH, ATTN_WIDTH + 2 * KV_WIDTH, 2 * ATTN_WIDTH + 2 * KV_WIDTH

V_LN, V_CB, V_BR, V_BI, V_LAM, V_AG, V_LG, V_FG, V_CW = 0, 1, 2, 3, 4, 5, 6, 7, 8
V_HS = V_CW + CONV_W
V_ROWS = 16


def _silu_from_half(hx):
    return hx * (jnp.tanh(hx) + 1.0)


def _rms_scale(v):
    return lax.rsqrt(jnp.mean(v * v, axis=-1, keepdims=True) + EPS)


def _layer_kernel(sinks_ref, x_ref, rope_ref, vecs_ref, wq_ref, wkv_ref, wga_ref, wlru_ref,
                  wg_ref, woa_ref, wol_ref, o_ref,
                  h_scr, q_scr, km_scr, vt_scr, ga_scr, xl_scr, gl_scr, xc_scr, a_scr, u_scr,
                  hs_scr, ycat_scr, cw_scr, hst_scr, vb_scr, *, batch):
    f32, bf16 = jnp.float32, jnp.bfloat16
    rows = batch * TS
    step = pl.program_id(0)

    @pl.when(step == 0)
    def _init_carries():
        km_scr[...] = jnp.zeros_like(km_scr)
        vt_scr[...] = jnp.zeros_like(vt_scr)
        cw_scr[...] = jnp.zeros_like(cw_scr)
        hst_scr[...] = jnp.zeros_like(hst_scr)
        for row in range(V_ROWS):
            vb_scr[row] = jnp.broadcast_to(vecs_ref[row:row + 1, :], (SUBLANES, D_MODEL))
        lam = vb_scr[V_LAM]
        softplus_neg_lam = jnp.maximum(-lam, 0.0) + jnp.log1p(jnp.exp(-jnp.abs(lam)))
        vb_scr[V_HS] = (-0.5 * LRU_C) * softplus_neg_lam

    def by_tile(v, op, tile):
        return op(v.reshape(-1, SUBLANES, v.shape[-1]), tile[None]).reshape(v.shape)

    def scale_by(v, row):
        return by_tile(v, jnp.multiply, vb_scr[row])

    def norm_b(b, c):
        xb = x_ref[b]
        hb = scale_by(xb * _rms_scale(xb), V_LN)
        h_scr[pl.ds(pl.multiple_of(b * TS, TS), TS), :] = hb.astype(bf16)
        return c
    lax.fori_loop(0, batch, norm_b, 0, unroll=True)

    hmat = h_scr[...]

    def proj(w_ref, c0, c1):
        return jnp.dot(hmat, w_ref[:, c0:c1], preferred_element_type=f32)

    k_tab = (rope_ref[0], rope_ref[1], rope_ref[2])
    q_tab = tuple(t * Q_SCALE for t in k_tab)

    def rope(z, tab):
        cos_t, sin_hi, sin_lo = tab
        return (z * cos_t[None] + pltpu.roll(z, 8, 2) * sin_hi[None]
                + pltpu.roll(z, LANES - 8, 2) * sin_lo[None])

    q_cols = ATTN_WIDTH // 2
    for c in range(2):
        zq = proj(wq_ref, c * q_cols, (c + 1) * q_cols).reshape(batch, TS, q_cols)
        for sl in range(q_cols // LANES):
            r = rope(zq[:, :, sl * LANES:(sl + 1) * LANES], q_tab)
            c0 = c * q_cols + sl * LANES
            q_scr[:, c0:c0 + LANES] = r.reshape(rows, LANES).astype(bf16)

    slot = lax.rem(step, N_SLOT)
    slot_row = pl.multiple_of(slot * TS, TS)
    zkv = proj(wkv_ref, 0, 2 * KV_WIDTH).reshape(batch, TS, 2 * KV_WIDTH)
    lane_half = lax.broadcasted_iota(jnp.int32, (1, 1, LANES), 2) // HEAD_DIM
    for sl in range(KV_WIDTH // LANES):
        r = rope(zkv[:, :, sl * LANES:(sl + 1) * LANES], k_tab).astype(bf16)
        for hh in range(LANES // HEAD_DIM):
            km_scr[:, sl * (LANES // HEAD_DIM) + hh, pl.ds(slot_row, TS), :] = jnp.where(
                lane_half == hh, r, jnp.zeros_like(r))
    for b in range(batch):
        vt_scr[b, slot] = zkv[b, :, KV_WIDTH:].T.astype(bf16)

    half = ATTN_WIDTH // 2
    for c in range(2):
        ga_scr[:, c * half:(c + 1) * half] = proj(wga_ref, c * half, (c + 1) * half)
        gl_scr[:, c * half:(c + 1) * half] = proj(wlru_ref, LRU_WIDTH + c * half, LRU_WIDTH + (c + 1) * half)
        zx = proj(wlru_ref, c * half, (c + 1) * half)
        for sl in range(half // LANES):
            for b in range(batch):
                xl_scr[c * (half // LANES) + sl, b * PITCH:b * PITCH + TS, :] = (
                    zx[b * TS:(b + 1) * TS, sl * LANES:(sl + 1) * LANES])

    jj = lax.broadcasted_iota(jnp.int32, (TS, GROUP * TS), 0)
    ii = lax.broadcasted_iota(jnp.int32, (TS, GROUP * TS), 1) % TS
    blocks = []
    for sg in range(N_SLOT):
        age = lax.rem(step + (N_SLOT - sg), N_SLOT)
        valid = ((age == 0) & (jj <= ii)) | (age == 1) | ((age == 2) & (jj > ii))
        blocks.append(jnp.where(valid & (step >= age), 0.0, NEG_INF).astype(f32))
    bias_t = jnp.concatenate(blocks, axis=0)
    lane_g = lax.broadcasted_iota(jnp.int32, (1, GROUP * TS), 1) // TS
    sink_rows = []
    for k in range(N_KV_HEADS):
        row = jnp.zeros((1, GROUP * TS), f32)
        for g in range(GROUP):
            row = jnp.where(lane_g == g, sinks_ref[k * GROUP + g] * LOG2E, row)
        sink_rows.append(row)

    def attn_b(b, c):
        r0 = pl.multiple_of(b * TS, TS)
        q_pair = [jnp.concatenate(
            [q_scr[pl.ds(r0, TS), g * MXU_DIM + sl * LANES:g * MXU_DIM + (sl + 1) * LANES]
             for g in range(GROUP)], axis=0) for sl in range(KV_WIDTH // LANES)]
        o_heads = []
        for k in range(N_KV_HEADS):
            st = lax.dot_general(km_scr[b, k], q_pair[k // 2], (((1,), (1,)), ((), ())),
                                 preferred_element_type=f32)
            st = st + bias_t
            m = jnp.maximum(jnp.max(st, axis=0, keepdims=True), sink_rows[k])
            p = jnp.exp2(st - m)
            den = jnp.sum(p, axis=0, keepdims=True) + jnp.exp2(sink_rows[k] - m)
            pb = p.astype(bf16)
            o = jnp.zeros((HEAD_DIM, GROUP * TS), f32)
            for sg in range(N_SLOT):
                o = o + jnp.dot(vt_scr[b, sg, k * HEAD_DIM:(k + 1) * HEAD_DIM, :],
                                pb[sg * TS:(sg + 1) * TS], preferred_element_type=f32)
            o_heads.append(o * (1.0 / den))
        o_rows = jnp.concatenate(o_heads, axis=0).T
        ob = jnp.concatenate([o_rows[g * TS:(g + 1) * TS] for g in range(GROUP)], axis=1)
        v = ob * _silu_from_half(ga_scr[pl.ds(r0, TS), :])
        y = scale_by(v * _rms_scale(v), V_AG)
        ycat_scr[pl.ds(r0, TS), 0:ATTN_WIDTH] = y.astype(bf16)
        return c
    lax.fori_loop(0, batch, attn_b, 0, unroll=8)

    def conv_t(t, carry):
        x1, x2, x3 = carry
        xt = jnp.concatenate(
            [xl_scr[c, pl.ds(t, batch, stride=PITCH), :] for c in range(N_SLAB)], axis=1)
        u = (vb_scr[V_CB] + vb_scr[V_CW + 3] * xt + vb_scr[V_CW + 2] * x1
             + vb_scr[V_CW + 1] * x2 + vb_scr[V_CW] * x3)
        t8 = pl.multiple_of(t * batch, batch)
        for j in range(N_GRP):
            xc_scr[j, pl.ds(t8, batch), :] = u[:, j * MXU_DIM:(j + 1) * MXU_DIM]
        return xt, x1, x2
    tail = lax.fori_loop(0, TS, conv_t, (cw_scr[0], cw_scr[1], cw_scr[2]), unroll=True)
    for i in range(CONV_W - 1):
        cw_scr[i] = tail[i]

    half_scale = vb_scr[V_HS]
    for j in range(N_GRP):
        cs = slice(j * MXU_DIM, (j + 1) * MXU_DIM)
        xg = xc_scr[j]
        pre = jnp.dot(xg.astype(bf16), wg_ref[j], preferred_element_type=f32)
        t_r = jnp.tanh(by_tile(pre[:, :MXU_DIM], jnp.add, vb_scr[V_BR][:, cs]))
        t_i = jnp.tanh(by_tile(pre[:, MXU_DIM:], jnp.add, vb_scr[V_BI][:, cs]))
        log_a = by_tile(by_tile(t_r, jnp.multiply, half_scale[:, cs]), jnp.add, half_scale[:, cs])
        a_scr[j] = jnp.exp(log_a)
        th = jnp.tanh(log_a)
        a_half = -0.5 * th
        gate = jnp.where(a_half > 0.0, a_half * lax.rsqrt(a_half * (1.0 - th)), 0.0)
        u_scr[j] = gate * ((t_i + 1.0) * xg)

    def scan_t(t, h):
        t8 = pl.multiple_of(t * batch, batch)
        at = jnp.concatenate([a_scr[j, pl.ds(t8, batch), :] for j in range(N_GRP)], axis=1)
        ut = jnp.concatenate([u_scr[j, pl.ds(t8, batch), :] for j in range(N_GRP)], axis=1)
        h = at * h + ut
        for c in range(N_SLAB):
            hs_scr[c, pl.ds(t, batch, stride=PITCH), :] = h[:, c * LANES:(c + 1) * LANES]
        return h
    hst_scr[...] = lax.fori_loop(0, TS, scan_t, hst_scr[...], unroll=True)

    def lru_out_b(b, c):
        r0 = pl.multiple_of(b * TS, TS)
        p0 = pl.multiple_of(b * PITCH, 8)
        hb = jnp.concatenate([hs_scr[c2, pl.ds(p0, TS), :] for c2 in range(N_SLAB)], axis=1)
        v = hb * _silu_from_half(gl_scr[pl.ds(r0, TS), :])
        y = scale_by(v * _rms_scale(v), V_LG)
        ycat_scr[pl.ds(r0, TS), ATTN_WIDTH:] = y.astype(bf16)
        return c
    lax.fori_loop(0, batch, lru_out_b, 0, unroll=True)

    ga_scr[...] = (
        jnp.dot(ycat_scr[:, :ATTN_WIDTH], woa_ref[...], preferred_element_type=f32)
        + jnp.dot(ycat_scr[:, ATTN_WIDTH:], wol_ref[...], preferred_element_type=f32))

    def out_b(b, c):
        x2 = x_ref[b] + ga_scr[pl.ds(pl.multiple_of(b * TS, TS), TS), :]
        o_ref[b] = scale_by(x2 * _rms_scale(x2), V_FG)
        return c
    lax.fori_loop(0, batch, out_b, 0, unroll=True)


def _rope_tables(seq):
    half = ROT_DIM // 2
    pos = jnp.arange(seq, dtype=jnp.float32)
    inv_freq = ROPE_THETA ** (-jnp.arange(0, ROT_DIM, 2, dtype=jnp.float32) / ROT_DIM)
    ang = pos[:, None] * jnp.tile(inv_freq, LANES // half)[None, :]
    cos, sin = jnp.cos(ang), jnp.sin(ang)
    d = np.arange(LANES) % HEAD_DIM
    lo = (d < half)[None, :]
    hi = ((d >= half) & (d < ROT_DIM))[None, :]
    return jnp.stack([jnp.where(lo | hi, cos, 1.0),
                      jnp.where(hi, sin, 0.0),
                      jnp.where(lo, -sin, 0.0)])


def _group_major(a, axis):
    shp = a.shape
    a = a.reshape(shp[:axis] + (N_KV_HEADS, GROUP, HEAD_DIM) + shp[axis + 1:])
    return jnp.swapaxes(a, axis, axis + 1).reshape(shp)


def kernel(x, ln_gain, w_in, sinks, conv_w, conv_b, w_rgate, b_rgate, w_igate, b_igate,
           lru_lambda, attn_out_gain, lru_out_gain, w_out, final_gain):
    batch, seq, d_model = x.shape
    assert d_model == D_MODEL and seq % TS == 0 and ln_gain.shape[0] == 1 and N_SLOT == 3
    assert batch == SUBLANES
    f32, bf16 = jnp.float32, jnp.bfloat16
    rows = batch * TS

    col_scale = np.ones((1, OFF_XL + 2 * LRU_WIDTH), np.float32)
    col_scale[:, OFF_GA:OFF_XL] = 0.5
    col_scale[:, OFF_XL + LRU_WIDTH:] = 0.5
    w_all = (w_in[0] * col_scale).astype(bf16)
    wo_all = w_out[0].astype(bf16)
    old_col = np.arange(ATTN_WIDTH).reshape(N_KV_HEADS, GROUP, HEAD_DIM).transpose(1, 0, 2).reshape(-1)
    perm = np.zeros((ATTN_WIDTH, ATTN_WIDTH), np.float32)
    perm[old_col, np.arange(ATTN_WIDTH)] = 1.0
    perm = jnp.asarray(perm, dtype=bf16)
    wq = jnp.dot(w_all[:, :OFF_KV], perm, preferred_element_type=f32).astype(bf16)
    wga = jnp.dot(w_all[:, OFF_GA:OFF_XL], perm, preferred_element_type=f32).astype(bf16)
    woa = jnp.dot(perm.T, wo_all[:ATTN_WIDTH], preferred_element_type=f32).astype(bf16)
    wkv, wlru, wol = w_all[:, OFF_KV:OFF_GA], w_all[:, OFF_XL:], wo_all[ATTN_WIDTH:]

    def block_diag(wb):
        per = MXU_DIM // LRU_BLOCK_W
        wb = wb.reshape(N_GRP, per, LRU_BLOCK_W, 1, LRU_BLOCK_W)
        eye = np.eye(per, dtype=np.float32).reshape(1, per, 1, per, 1)
        return (wb * (0.5 * eye)).reshape(N_GRP, MXU_DIM, MXU_DIM)
    wg = jnp.concatenate([block_diag(w_rgate[0]), block_diag(w_igate[0])], axis=2).astype(bf16)

    vec_rows = [None] * (V_CW + CONV_W)
    vec_rows[V_LN], vec_rows[V_CB], vec_rows[V_BR] = ln_gain, conv_b, 0.5 * b_rgate
    vec_rows[V_BI], vec_rows[V_LAM], vec_rows[V_LG] = 0.5 * b_igate, lru_lambda, lru_out_gain
    vec_rows[V_AG], vec_rows[V_FG] = _group_major(attn_out_gain, 1), final_gain[None]
    vec_rows[V_CW:] = [conv_w[0][tap:tap + 1] for tap in range(CONV_W)]
    vecs = jnp.concatenate(vec_rows + [jnp.zeros((V_ROWS - len(vec_rows), D_MODEL), f32)], axis=0)

    rope_tbl = _rope_tables(seq)

    const = lambda *_: (0, 0)
    single = dict(pipeline_mode=pl.Buffered(1))
    grid_spec = pl.GridSpec(
        grid=(seq // TS,),
        in_specs=[
            pl.BlockSpec(memory_space=pltpu.SMEM),
            pl.BlockSpec((batch, TS, D_MODEL), lambda s: (0, s, 0)),
            pl.BlockSpec((3, TS, LANES), lambda s: (0, s, 0)),
            pl.BlockSpec((V_ROWS, D_MODEL), const, **single),
            pl.BlockSpec((D_MODEL, ATTN_WIDTH), const, **single),
            pl.BlockSpec((D_MODEL, 2 * KV_WIDTH), const, **single),
            pl.BlockSpec((D_MODEL, ATTN_WIDTH), const, **single),
            pl.BlockSpec((D_MODEL, 2 * LRU_WIDTH), const, **single),
            pl.BlockSpec((N_GRP, MXU_DIM, 2 * MXU_DIM), lambda s: (0, 0, 0), **single),
            pl.BlockSpec((ATTN_WIDTH, D_MODEL), const, **single),
            pl.BlockSpec((LRU_WIDTH, D_MODEL), const, **single),
        ],
        out_specs=pl.BlockSpec((batch, TS, D_MODEL), lambda s: (0, s, 0)),
        scratch_shapes=[
            pltpu.VMEM((rows, D_MODEL), bf16),
            pltpu.VMEM((rows, ATTN_WIDTH), bf16),
            pltpu.VMEM((batch, N_KV_HEADS, KEYS, LANES), bf16),
            pltpu.VMEM((batch, N_SLOT, KV_WIDTH, TS), bf16),
            pltpu.VMEM((rows, ATTN_WIDTH), f32),
            pltpu.VMEM((N_SLAB, batch * PITCH, LANES), f32),
            pltpu.VMEM((rows, LRU_WIDTH), f32),
            pltpu.VMEM((N_GRP, rows, MXU_DIM), f32),
            pltpu.VMEM((N_GRP, rows, MXU_DIM), f32),
            pltpu.VMEM((N_GRP, rows, MXU_DIM), f32),
            pltpu.VMEM((N_SLAB, batch * PITCH, LANES), f32),
            pltpu.VMEM((rows, 2 * ATTN_WIDTH), bf16),
            pltpu.VMEM((CONV_W - 1, batch, LRU_WIDTH), f32),
            pltpu.VMEM((batch, LRU_WIDTH), f32),
            pltpu.VMEM((V_ROWS, SUBLANES, D_MODEL), f32),
        ],
    )
    return pl.pallas_call(
        functools.partial(_layer_kernel, batch=batch),
        grid_spec=grid_spec,
        out_shape=jax.ShapeDtypeStruct(x.shape, x.dtype),
        compiler_params=pltpu.CompilerParams(
            dimension_semantics=("arbitrary",), vmem_limit_bytes=VMEM_LIMIT_BYTES),
        name="hymba_layer",
    )(sinks[0], x, rope_tbl, vecs, wq, wkv, wga, wlru, wg, woa, wol)
```

```python
import functools

import numpy as np
import jax
import jax.numpy as jnp
from jax import lax
from jax.experimental import pallas as pl
from jax.experimental.pallas import tpu as pltpu

D_MODEL = 1024
HEAD_DIM = 64
N_Q_HEADS = 16
N_KV_HEADS = 4
GROUP = N_Q_HEADS // N_KV_HEADS
ATTN_WIDTH = N_Q_HEADS * HEAD_DIM
KV_WIDTH = N_KV_HEADS * HEAD_DIM
WINDOW = 128
ROT_DIM = HEAD_DIM // 4
ROPE_THETA = 500000.0
NEG_INF = -1e30
LRU_WIDTH = 1024
LRU_BLOCKS = 16
LRU_BLOCK_W = LRU_WIDTH // LRU_BLOCKS
CONV_W = 4
LRU_C = 8.0
EPS = 1e-6

LANES = 128
SUBLANES = 8
MXU_DIM = 256
VMEM_LIMIT_BYTES = 58 * 1024 * 1024

TS = 64
N_SLOT = WINDOW // TS + 1
KEYS = N_SLOT * TS
LOG2E = 1.4426950408889634
Q_SCALE = HEAD_DIM ** -0.5 * LOG2E
PITCH = TS + 8
N_SLAB = LRU_WIDTH // LANES
N_GRP = LRU_WIDTH // MXU_DIM
BIAS_ROWS = 2

OFF_KV, OFF_GA, OFF_XL = ATTN_WIDTH, ATTN_WIDTH + 2 * KV_WIDTH, 2 * ATTN_WIDTH + 2 * KV_WIDTH

V_LN, V_CB, V_BR, V_BI, V_LAM, V_AG, V_LG, V_FG, V_CW = 0, 1, 2, 3, 4, 5, 6, 7, 8
V_HS = V_CW + CONV_W
V_ROWS = 16


def _silu_from_half(hx):
    return hx * (jnp.tanh(hx) + 1.0)


def _rms_scale(v):
    return lax.rsqrt(jnp.mean(v * v, axis=-1, keepdims=True) + EPS)


def _layer_kernel(sinks_ref, x_ref, rope_ref, vecs_ref, wq_ref, wkv_ref, wga_ref, wlru_ref,
                  wg_ref, woa_ref, wol_ref, o_ref,
                  h_scr, q_scr, km_scr, vt_scr, ga_scr, xl_scr, gl_scr, xc_scr, a_scr, u_scr,
                  hs_scr, ycat_scr, cw_scr, hst_scr, vb_scr, *, batch):
    f32, bf16 = jnp.float32, jnp.bfloat16
    rows = batch * TS
    step = pl.program_id(0)

    @pl.when(step == 0)
    def _init_carries():
        km_scr[...] = jnp.zeros_like(km_scr)
        vt_scr[...] = jnp.zeros_like(vt_scr)
        cw_scr[...] = jnp.zeros_like(cw_scr)
        hst_scr[...] = jnp.zeros_like(hst_scr)
        for row in range(V_ROWS):
            vb_scr[row] = jnp.broadcast_to(vecs_ref[row:row + 1, :], (SUBLANES, D_MODEL))
        lam = vb_scr[V_LAM]
        softplus_neg_lam = jnp.maximum(-lam, 0.0) + jnp.log1p(jnp.exp(-jnp.abs(lam)))
        vb_scr[V_HS] = (-0.5 * LRU_C) * softplus_neg_lam

    def by_tile(v, op, tile):
        return op(v.reshape(-1, SUBLANES, v.shape[-1]), tile[None]).reshape(v.shape)

    def scale_by(v, row):
        return by_tile(v, jnp.multiply, vb_scr[row])

    def norm_b(b, c):
        xb = x_ref[b]
        hb = scale_by(xb * _rms_scale(xb), V_LN)
        h_scr[pl.ds(pl.multiple_of(b * TS, TS), TS), :] = hb.astype(bf16)
        return c
    lax.fori_loop(0, batch, norm_b, 0, unroll=True)

    hmat = h_scr[...]

    def proj(w_ref, c0, c1):
        return jnp.dot(hmat, w_ref[:, c0:c1], preferred_element_type=f32)

    k_tab = (rope_ref[0], rope_ref[1], rope_ref[2])
    q_tab = tuple(t * Q_SCALE for t in k_tab)

    def rope(z, tab):
        cos_t, sin_hi, sin_lo = tab
        return (z * cos_t[None] + pltpu.roll(z, 8, 2) * sin_hi[None]
                + pltpu.roll(z, LANES - 8, 2) * sin_lo[None])

    for g in range(GROUP):
        zq = proj(wq_ref, g * MXU_DIM, (g + 1) * MXU_DIM).reshape(batch, TS, MXU_DIM)
        for sl in range(MXU_DIM // LANES):
            r = rope(zq[:, :, sl * LANES:(sl + 1) * LANES], q_tab)
            c0 = g * MXU_DIM + sl * LANES
            q_scr[:, c0:c0 + LANES] = r.reshape(rows, LANES).astype(bf16)

    slot = lax.rem(step, N_SLOT)
    slot_row = pl.multiple_of(slot * TS, TS)
    zkv = proj(wkv_ref, 0, 2 * KV_WIDTH).reshape(batch, TS, 2 * KV_WIDTH)
    lane_half = lax.broadcasted_iota(jnp.int32, (1, 1, LANES), 2) // HEAD_DIM
    for sl in range(KV_WIDTH // LANES):
        r = rope(zkv[:, :, sl * LANES:(sl + 1) * LANES], k_tab).astype(bf16)
        for hh in range(LANES // HEAD_DIM):
            km_scr[:, sl * (LANES // HEAD_DIM) + hh, pl.ds(slot_row, TS), :] = jnp.where(
                lane_half == hh, r, jnp.zeros_like(r))
    for b in range(batch):
        vt_scr[b, slot] = zkv[b, :, KV_WIDTH:].T.astype(bf16)

    half = ATTN_WIDTH // 2
    for c in range(2):
        ga_scr[:, c * half:(c + 1) * half] = proj(wga_ref, c * half, (c + 1) * half)
        gl_scr[:, c * half:(c + 1) * half] = proj(wlru_ref, LRU_WIDTH + c * half, LRU_WIDTH + (c + 1) * half)
        zx = proj(wlru_ref, c * half, (c + 1) * half)
        for sl in range(half // LANES):
            for b in range(batch):
                xl_scr[c * (half // LANES) + sl, b * PITCH:b * PITCH + TS, :] = (
                    zx[b * TS:(b + 1) * TS, sl * LANES:(sl + 1) * LANES])

    jj = lax.broadcasted_iota(jnp.int32, (TS, GROUP * TS), 0)
    ii = lax.broadcasted_iota(jnp.int32, (TS, GROUP * TS), 1) % TS
    blocks = []
    for sg in range(N_SLOT):
        age = lax.rem(step + (N_SLOT - sg), N_SLOT)
        valid = ((age == 0) & (jj <= ii)) | (age == 1) | ((age == 2) & (jj > ii))
        blocks.append(jnp.where(valid & (step >= age), 0.0, NEG_INF).astype(f32))
    bias_t = jnp.concatenate(blocks, axis=0)
    lane_g = lax.broadcasted_iota(jnp.int32, (1, GROUP * TS), 1) // TS
    sink_rows = []
    for k in range(N_KV_HEADS):
        row = jnp.zeros((1, GROUP * TS), f32)
        for g in range(GROUP):
            row = jnp.where(lane_g == g, sinks_ref[k * GROUP + g] * LOG2E, row)
        sink_rows.append(row)

    def attn_b(b, c):
        r0 = pl.multiple_of(b * TS, TS)
        q_pair = [jnp.concatenate(
            [q_scr[pl.ds(r0, TS), g * MXU_DIM + sl * LANES:g * MXU_DIM + (sl + 1) * LANES]
             for g in range(GROUP)], axis=0) for sl in range(KV_WIDTH // LANES)]
        o_heads = []
        for k in range(N_KV_HEADS):
            st = lax.dot_general(km_scr[b, k], q_pair[k // 2], (((1,), (1,)), ((), ())),
                                 preferred_element_type=f32)
            st = st + bias_t
            m = jnp.maximum(jnp.max(st, axis=0, keepdims=True), sink_rows[k])
            p = jnp.exp2(st - m)
            den = jnp.sum(p, axis=0, keepdims=True) + jnp.exp2(sink_rows[k] - m)
            pb = p.astype(bf16)
            o = jnp.zeros((HEAD_DIM, GROUP * TS), f32)
            for sg in range(N_SLOT):
                o = o + jnp.dot(vt_scr[b, sg, k * HEAD_DIM:(k + 1) * HEAD_DIM, :],
                                pb[sg * TS:(sg + 1) * TS], preferred_element_type=f32)
            o_heads.append(o * (1.0 / den))
        o_rows = jnp.concatenate(o_heads, axis=0).T
        ob = jnp.concatenate([o_rows[g * TS:(g + 1) * TS] for g in range(GROUP)], axis=1)
        v = ob * _silu_from_half(ga_scr[pl.ds(r0, TS), :])
        y = scale_by(v * _rms_scale(v), V_AG)
        ycat_scr[pl.ds(r0, TS), 0:ATTN_WIDTH] = y.astype(bf16)
        return c
    lax.fori_loop(0, batch, attn_b, 0, unroll=8)

    def conv_t(t, carry):
        x1, x2, x3 = carry
        xt = jnp.concatenate(
            [xl_scr[c, pl.ds(t, batch, stride=PITCH), :] for c in range(N_SLAB)], axis=1)
        u = (vb_scr[V_CB] + vb_scr[V_CW + 3] * xt + vb_scr[V_CW + 2] * x1
             + vb_scr[V_CW + 1] * x2 + vb_scr[V_CW] * x3)
        t8 = pl.multiple_of(t * batch, batch)
        for j in range(N_GRP):
            xc_scr[j, pl.ds(t8, batch), :] = u[:, j * MXU_DIM:(j + 1) * MXU_DIM]
        return xt, x1, x2
    tail = lax.fori_loop(0, TS, conv_t, (cw_scr[0], cw_scr[1], cw_scr[2]), unroll=True)
    for i in range(CONV_W - 1):
        cw_scr[i] = tail[i]

    half_scale = vb_scr[V_HS]
    one_cols = (lax.broadcasted_iota(jnp.int32, (rows, LANES), 1) < BIAS_ROWS).astype(bf16)
    for j in range(N_GRP):
        cs = slice(j * MXU_DIM, (j + 1) * MXU_DIM)
        xg = xc_scr[j]
        pre = jnp.dot(jnp.concatenate([xg.astype(bf16), one_cols], axis=1), wg_ref[j],
                      preferred_element_type=f32)
        t_r = jnp.tanh(pre[:, :MXU_DIM])
        t_i = jnp.tanh(pre[:, MXU_DIM:])
        log_a = by_tile(by_tile(t_r, jnp.multiply, half_scale[:, cs]), jnp.add, half_scale[:, cs])
        a_scr[j] = jnp.exp(log_a)
        th = jnp.tanh(log_a)
        a_half = -0.5 * th
        gate = jnp.where(a_half > 0.0, a_half * lax.rsqrt(a_half * (1.0 - th)), 0.0)
        u_scr[j] = gate * ((t_i + 1.0) * xg)

    def scan_t(t, h):
        t8 = pl.multiple_of(t * batch, batch)
        at = jnp.concatenate([a_scr[j, pl.ds(t8, batch), :] for j in range(N_GRP)], axis=1)
        ut = jnp.concatenate([u_scr[j, pl.ds(t8, batch), :] for j in range(N_GRP)], axis=1)
        h = at * h + ut
        for c in range(N_SLAB):
            hs_scr[c, pl.ds(t, batch, stride=PITCH), :] = h[:, c * LANES:(c + 1) * LANES]
        return h
    hst_scr[...] = lax.fori_loop(0, TS, scan_t, hst_scr[...], unroll=True)

    def lru_out_b(b, c):
        r0 = pl.multiple_of(b * TS, TS)
        p0 = pl.multiple_of(b * PITCH, 8)
        hb = jnp.concatenate([hs_scr[c2, pl.ds(p0, TS), :] for c2 in range(N_SLAB)], axis=1)
        v = hb * _silu_from_half(gl_scr[pl.ds(r0, TS), :])
        y = scale_by(v * _rms_scale(v), V_LG)
        ycat_scr[pl.ds(r0, TS), ATTN_WIDTH:] = y.astype(bf16)
        return c
    lax.fori_loop(0, batch, lru_out_b, 0, unroll=True)

    ga_scr[...] = (
        jnp.dot(ycat_scr[:, :ATTN_WIDTH], woa_ref[...], preferred_element_type=f32)
        + jnp.dot(ycat_scr[:, ATTN_WIDTH:], wol_ref[...], preferred_element_type=f32))

    def out_b(b, c):
        x2 = x_ref[b] + ga_scr[pl.ds(pl.multiple_of(b * TS, TS), TS), :]
        o_ref[b] = scale_by(x2 * _rms_scale(x2), V_FG)
        return c
    lax.fori_loop(0, batch, out_b, 0, unroll=True)


def _rope_tables(seq):
    half = ROT_DIM // 2
    pos = jnp.arange(seq, dtype=jnp.float32)
    inv_freq = ROPE_THETA ** (-jnp.arange(0, ROT_DIM, 2, dtype=jnp.float32) / ROT_DIM)
    ang = pos[:, None] * jnp.tile(inv_freq, LANES // half)[None, :]
    cos, sin = jnp.cos(ang), jnp.sin(ang)
    d = np.arange(LANES) % HEAD_DIM
    lo = (d < half)[None, :]
    hi = ((d >= half) & (d < ROT_DIM))[None, :]
    return jnp.stack([jnp.where(lo | hi, cos, 1.0),
                      jnp.where(hi, sin, 0.0),
                      jnp.where(lo, -sin, 0.0)])


def _group_major(a, axis):
    shp = a.shape
    a = a.reshape(shp[:axis] + (N_KV_HEADS, GROUP, HEAD_DIM) + shp[axis + 1:])
    return jnp.swapaxes(a, axis, axis + 1).reshape(shp)


def kernel(x, ln_gain, w_in, sinks, conv_w, conv_b, w_rgate, b_rgate, w_igate, b_igate,
           lru_lambda, attn_out_gain, lru_out_gain, w_out, final_gain):
    batch, seq, d_model = x.shape
    assert d_model == D_MODEL and seq % TS == 0 and ln_gain.shape[0] == 1 and N_SLOT == 3
    assert batch == SUBLANES
    f32, bf16 = jnp.float32, jnp.bfloat16
    rows = batch * TS

    col_scale = np.ones((1, OFF_XL + 2 * LRU_WIDTH), np.float32)
    col_scale[:, OFF_GA:OFF_XL] = 0.5
    col_scale[:, OFF_XL + LRU_WIDTH:] = 0.5
    w_all = (w_in[0] * col_scale).astype(bf16)
    wo_all = w_out[0].astype(bf16)
    old_col = np.arange(ATTN_WIDTH).reshape(N_KV_HEADS, GROUP, HEAD_DIM).transpose(1, 0, 2).reshape(-1)
    perm = np.zeros((ATTN_WIDTH, ATTN_WIDTH), np.float32)
    perm[old_col, np.arange(ATTN_WIDTH)] = 1.0
    perm = jnp.asarray(perm, dtype=bf16)
    wq = jnp.dot(w_all[:, :OFF_KV], perm, preferred_element_type=f32).astype(bf16)
    wga = jnp.dot(w_all[:, OFF_GA:OFF_XL], perm, preferred_element_type=f32).astype(bf16)
    woa = jnp.dot(perm.T, wo_all[:ATTN_WIDTH], preferred_element_type=f32).astype(bf16)
    wkv, wlru, wol = w_all[:, OFF_KV:OFF_GA], w_all[:, OFF_XL:], wo_all[ATTN_WIDTH:]

    def block_diag(wb):
        per = MXU_DIM // LRU_BLOCK_W
        wb = wb.reshape(N_GRP, per, LRU_BLOCK_W, 1, LRU_BLOCK_W)
        eye = np.eye(per, dtype=np.float32).reshape(1, per, 1, per, 1)
        return (wb * (0.5 * eye)).reshape(N_GRP, MXU_DIM, MXU_DIM)
    wg = jnp.concatenate([block_diag(w_rgate[0]), block_diag(w_igate[0])], axis=2).astype(bf16)
    gate_b = 0.5 * jnp.concatenate([b_rgate[0].reshape(N_GRP, 1, MXU_DIM),
                                    b_igate[0].reshape(N_GRP, 1, MXU_DIM)], axis=2)
    b_hi = gate_b.astype(bf16)
    b_lo = (gate_b - b_hi.astype(f32)).astype(bf16)
    wg = jnp.concatenate(
        [wg, b_hi, b_lo, jnp.zeros((N_GRP, LANES - BIAS_ROWS, 2 * MXU_DIM), bf16)], axis=1)

    vec_rows = [None] * (V_CW + CONV_W)
    vec_rows[V_LN], vec_rows[V_CB], vec_rows[V_BR] = ln_gain, conv_b, 0.5 * b_rgate
    vec_rows[V_BI], vec_rows[V_LAM], vec_rows[V_LG] = 0.5 * b_igate, lru_lambda, lru_out_gain
    vec_rows[V_AG], vec_rows[V_FG] = _group_major(attn_out_gain, 1), final_gain[None]
    vec_rows[V_CW:] = [conv_w[0][tap:tap + 1] for tap in range(CONV_W)]
    vecs = jnp.concatenate(vec_rows + [jnp.zeros((V_ROWS - len(vec_rows), D_MODEL), f32)], axis=0)

    rope_tbl = _rope_tables(seq)

    const = lambda *_: (0, 0)
    single = dict(pipeline_mode=pl.Buffered(1))
    grid_spec = pl.GridSpec(
        grid=(seq // TS,),
        in_specs=[
            pl.BlockSpec(memory_space=pltpu.SMEM),
            pl.BlockSpec((batch, TS, D_MODEL), lambda s: (0, s, 0)),
            pl.BlockSpec((3, TS, LANES), lambda s: (0, s, 0)),
            pl.BlockSpec((V_ROWS, D_MODEL), const, **single),
            pl.BlockSpec((D_MODEL, ATTN_WIDTH), const, **single),
            pl.BlockSpec((D_MODEL, 2 * KV_WIDTH), const, **single),
            pl.BlockSpec((D_MODEL, ATTN_WIDTH), const, **single),
            pl.BlockSpec((D_MODEL, 2 * LRU_WIDTH), const, **single),
            pl.BlockSpec((N_GRP, MXU_DIM + LANES, 2 * MXU_DIM), lambda s: (0, 0, 0), **single),
            pl.BlockSpec((ATTN_WIDTH, D_MODEL), const, **single),
            pl.BlockSpec((LRU_WIDTH, D_MODEL), const, **single),
        ],
        out_specs=pl.BlockSpec((batch, TS, D_MODEL), lambda s: (0, s, 0)),
        scratch_shapes=[
            pltpu.VMEM((rows, D_MODEL), bf16),
            pltpu.VMEM((rows, ATTN_WIDTH), bf16),
            pltpu.VMEM((batch, N_KV_HEADS, KEYS, LANES), bf16),
            pltpu.VMEM((batch, N_SLOT, KV_WIDTH, TS), bf16),
            pltpu.VMEM((rows, ATTN_WIDTH), f32),
            pltpu.VMEM((N_SLAB, batch * PITCH, LANES), f32),
            pltpu.VMEM((rows, LRU_WIDTH), f32),
            pltpu.VMEM((N_GRP, rows, MXU_DIM), f32),
            pltpu.VMEM((N_GRP, rows, MXU_DIM), f32),
            pltpu.VMEM((N_GRP, rows, MXU_DIM), f32),
            pltpu.VMEM((N_SLAB, batch * PITCH, LANES), f32),
            pltpu.VMEM((rows, 2 * ATTN_WIDTH), bf16),
            pltpu.VMEM((CONV_W - 1, batch, LRU_WIDTH), f32),
            pltpu.VMEM((batch, LRU_WIDTH), f32),
            pltpu.VMEM((V_ROWS, SUBLANES, D_MODEL), f32),
        ],
    )
    return pl.pallas_call(
        functools.partial(_layer_kernel, batch=batch),
        grid_spec=grid_spec,
        out_shape=jax.ShapeDtypeStruct(x.shape, x.dtype),
        compiler_params=pltpu.CompilerParams(
            dimension_semantics=("arbitrary",), vmem_limit_bytes=VMEM_LIMIT_BYTES),
        name="hymba_layer",
    )(sinks[0], x, rope_tbl, vecs, wq, wkv, wga, wlru, wg, woa, wol)
```

```python
import functools

import numpy as np
import jax
import jax.numpy as jnp
from jax import lax
from jax.experimental import pallas as pl
from jax.experimental.pallas import tpu as pltpu

D_MODEL = 1024
HEAD_DIM = 64
N_Q_HEADS = 16
N_KV_HEADS = 4
GROUP = N_Q_HEADS // N_KV_HEADS
ATTN_WIDTH = N_Q_HEADS * HEAD_DIM
KV_WIDTH = N_KV_HEADS * HEAD_DIM
WINDOW = 128
ROT_DIM = HEAD_DIM // 4
ROPE_THETA = 500000.0
NEG_INF = -1e30
LRU_WIDTH = 1024
LRU_BLOCKS = 16
LRU_BLOCK_W = LRU_WIDTH // LRU_BLOCKS
CONV_W = 4
LRU_C = 8.0
EPS = 1e-6

LANES = 128
SUBLANES = 8
MXU_DIM = 256
VMEM_LIMIT_BYTES = 58 * 1024 * 1024

TS = 64
N_SLOT = WINDOW // TS + 1
KEYS = N_SLOT * TS
LOG2E = 1.4426950408889634
Q_SCALE = HEAD_DIM ** -0.5 * LOG2E
PITCH = TS + 8
N_SLAB = LRU_WIDTH // LANES
N_GRP = LRU_WIDTH // MXU_DIM

OFF_KV, OFF_GA, OFF_XL = ATTN_WIDTH, ATTN_WIDTH + 2 * KV_WIDTH, 2 * ATTN_WIDTH + 2 * KV_WIDTH

V_LN, V_CB, V_BR, V_BI, V_LAM, V_AG, V_LG, V_FG, V_CW = 0, 1, 2, 3, 4, 5, 6, 7, 8
V_HS = V_CW + CONV_W
V_ROWS = 16


def _silu_from_half(hx):
    return hx * (jnp.tanh(hx) + 1.0)


def _rms_scale(v):
    return lax.rsqrt(jnp.mean(v * v, axis=-1, keepdims=True) + EPS)


def _layer_kernel(sinks_ref, x_ref, xn_ref, rope_ref, vecs_ref, wq_ref, wkv_ref, wga_ref, wlru_ref,
                  wg_ref, woa_ref, wol_ref, o_ref,
                  h_scr, q_scr, km_scr, vt_scr, ga_scr, xl_scr, gl_scr, xc_scr, a_scr, u_scr,
                  hs_scr, ycat_scr, cw_scr, hst_scr, vb_scr, *, batch):
    f32, bf16 = jnp.float32, jnp.bfloat16
    rows = batch * TS
    step = pl.program_id(0)

    @pl.when(step == 0)
    def _init_carries():
        km_scr[...] = jnp.zeros_like(km_scr)
        vt_scr[...] = jnp.zeros_like(vt_scr)
        cw_scr[...] = jnp.zeros_like(cw_scr)
        hst_scr[...] = jnp.zeros_like(hst_scr)
        for row in range(V_ROWS):
            vb_scr[row] = jnp.broadcast_to(vecs_ref[row:row + 1, :], (SUBLANES, D_MODEL))
        lam = vb_scr[V_LAM]
        softplus_neg_lam = jnp.maximum(-lam, 0.0) + jnp.log1p(jnp.exp(-jnp.abs(lam)))
        vb_scr[V_HS] = (-0.5 * LRU_C) * softplus_neg_lam

    def by_tile(v, op, tile):
        return op(v.reshape(-1, SUBLANES, v.shape[-1]), tile[None]).reshape(v.shape)

    def scale_by(v, row):
        return by_tile(v, jnp.multiply, vb_scr[row])

    def norm_into_h(src_ref):
        for b in range(batch):
            xb = src_ref[b]
            hb = scale_by(xb * _rms_scale(xb), V_LN)
            h_scr[b * TS:(b + 1) * TS, :] = hb.astype(bf16)

    @pl.when(step == 0)
    def _first_norm():
        norm_into_h(x_ref)

    hmat = h_scr[...]

    def proj(w_ref, c0, c1):
        return jnp.dot(hmat, w_ref[:, c0:c1], preferred_element_type=f32)

    k_tab = (rope_ref[0], rope_ref[1], rope_ref[2])
    q_tab = tuple(t * Q_SCALE for t in k_tab)

    def rope(z, tab):
        cos_t, sin_hi, sin_lo = tab
        return (z * cos_t[None] + pltpu.roll(z, 8, 2) * sin_hi[None]
                + pltpu.roll(z, LANES - 8, 2) * sin_lo[None])

    for g in range(GROUP):
        zq = proj(wq_ref, g * MXU_DIM, (g + 1) * MXU_DIM).reshape(batch, TS, MXU_DIM)
        for sl in range(MXU_DIM // LANES):
            r = rope(zq[:, :, sl * LANES:(sl + 1) * LANES], q_tab)
            c0 = g * MXU_DIM + sl * LANES
            q_scr[:, c0:c0 + LANES] = r.reshape(rows, LANES).astype(bf16)

    slot = lax.rem(step, N_SLOT)
    slot_row = pl.multiple_of(slot * TS, TS)
    zkv = proj(wkv_ref, 0, 2 * KV_WIDTH).reshape(batch, TS, 2 * KV_WIDTH)
    lane_half = lax.broadcasted_iota(jnp.int32, (1, 1, LANES), 2) // HEAD_DIM
    for sl in range(KV_WIDTH // LANES):
        r = rope(zkv[:, :, sl * LANES:(sl + 1) * LANES], k_tab).astype(bf16)
        for hh in range(LANES // HEAD_DIM):
            km_scr[:, sl * (LANES // HEAD_DIM) + hh, pl.ds(slot_row, TS), :] = jnp.where(
                lane_half == hh, r, jnp.zeros_like(r))
    for b in range(batch):
        vt_scr[b, slot] = zkv[b, :, KV_WIDTH:].T.astype(bf16)

    half = ATTN_WIDTH // 2
    for c in range(2):
        ga_scr[:, c * half:(c + 1) * half] = proj(wga_ref, c * half, (c + 1) * half)
        gl_scr[:, c * half:(c + 1) * half] = proj(wlru_ref, LRU_WIDTH + c * half, LRU_WIDTH + (c + 1) * half)
        zx = proj(wlru_ref, c * half, (c + 1) * half)
        for sl in range(half // LANES):
            for b in range(batch):
                xl_scr[c * (half // LANES) + sl, b * PITCH:b * PITCH + TS, :] = (
                    zx[b * TS:(b + 1) * TS, sl * LANES:(sl + 1) * LANES])

    jj = lax.broadcasted_iota(jnp.int32, (TS, GROUP * TS), 0)
    ii = lax.broadcasted_iota(jnp.int32, (TS, GROUP * TS), 1) % TS
    blocks = []
    for sg in range(N_SLOT):
        age = lax.rem(step + (N_SLOT - sg), N_SLOT)
        valid = ((age == 0) & (jj <= ii)) | (age == 1) | ((age == 2) & (jj > ii))
        blocks.append(jnp.where(valid & (step >= age), 0.0, NEG_INF).astype(f32))
    bias_t = jnp.concatenate(blocks, axis=0)
    lane_g = lax.broadcasted_iota(jnp.int32, (1, GROUP * TS), 1) // TS
    sink_rows = []
    for k in range(N_KV_HEADS):
        row = jnp.zeros((1, GROUP * TS), f32)
        for g in range(GROUP):
            row = jnp.where(lane_g == g, sinks_ref[k * GROUP + g] * LOG2E, row)
        sink_rows.append(row)

    def attn_b(b, c):
        r0 = pl.multiple_of(b * TS, TS)
        q_pair = [jnp.concatenate(
            [q_scr[pl.ds(r0, TS), g * MXU_DIM + sl * LANES:g * MXU_DIM + (sl + 1) * LANES]
             for g in range(GROUP)], axis=0) for sl in range(KV_WIDTH // LANES)]
        o_heads = []
        for k in range(N_KV_HEADS):
            st = lax.dot_general(km_scr[b, k], q_pair[k // 2], (((1,), (1,)), ((), ())),
                                 preferred_element_type=f32)
            st = st + bias_t
            m = jnp.maximum(jnp.max(st, axis=0, keepdims=True), sink_rows[k])
            p = jnp.exp2(st - m)
            den = jnp.sum(p, axis=0, keepdims=True) + jnp.exp2(sink_rows[k] - m)
            pb = p.astype(bf16)
            o = jnp.zeros((HEAD_DIM, GROUP * TS), f32)
            for sg in range(N_SLOT):
                o = o + jnp.dot(vt_scr[b, sg, k * HEAD_DIM:(k + 1) * HEAD_DIM, :],
                                pb[sg * TS:(sg + 1) * TS], preferred_element_type=f32)
            o_heads.append(o * (1.0 / den))
        o_rows = jnp.concatenate(o_heads, axis=0).T
        ob = jnp.concatenate([o_rows[g * TS:(g + 1) * TS] for g in range(GROUP)], axis=1)
        v = ob * _silu_from_half(ga_scr[pl.ds(r0, TS), :])
        y = scale_by(v * _rms_scale(v), V_AG)
        ycat_scr[pl.ds(r0, TS), 0:ATTN_WIDTH] = y.astype(bf16)
        return c
    lax.fori_loop(0, batch, attn_b, 0, unroll=8)

    def conv_t(t, carry):
        x1, x2, x3 = carry
        xt = jnp.concatenate(
            [xl_scr[c, pl.ds(t, batch, stride=PITCH), :] for c in range(N_SLAB)], axis=1)
        u = (vb_scr[V_CB] + vb_scr[V_CW + 3] * xt + vb_scr[V_CW + 2] * x1
             + vb_scr[V_CW + 1] * x2 + vb_scr[V_CW] * x3)
        t8 = pl.multiple_of(t * batch, batch)
        for j in range(N_GRP):
            xc_scr[j, pl.ds(t8, batch), :] = u[:, j * MXU_DIM:(j + 1) * MXU_DIM]
        return xt, x1, x2
    tail = lax.fori_loop(0, TS, conv_t, (cw_scr[0], cw_scr[1], cw_scr[2]), unroll=True)
    for i in range(CONV_W - 1):
        cw_scr[i] = tail[i]

    half_scale = vb_scr[V_HS]
    for j in range(N_GRP):
        cs = slice(j * MXU_DIM, (j + 1) * MXU_DIM)
        xg = xc_scr[j]
        pre = jnp.dot(xg.astype(bf16), wg_ref[j], preferred_element_type=f32)
        t_r = jnp.tanh(by_tile(pre[:, :MXU_DIM], jnp.add, vb_scr[V_BR][:, cs]))
        t_i = jnp.tanh(by_tile(pre[:, MXU_DIM:], jnp.add, vb_scr[V_BI][:, cs]))
        log_a = by_tile(by_tile(t_r, jnp.multiply, half_scale[:, cs]), jnp.add, half_scale[:, cs])
        a_scr[j] = jnp.exp(log_a)
        th = jnp.tanh(log_a)
        a_half = -0.5 * th
        gate = jnp.where(a_half > 0.0, a_half * lax.rsqrt(a_half * (1.0 - th)), 0.0)
        u_scr[j] = gate * ((t_i + 1.0) * xg)

    def scan_t(t, h):
        t8 = pl.multiple_of(t * batch, batch)
        at = jnp.concatenate([a_scr[j, pl.ds(t8, batch), :] for j in range(N_GRP)], axis=1)
        ut = jnp.concatenate([u_scr[j, pl.ds(t8, batch), :] for j in range(N_GRP)], axis=1)
        h = at * h + ut
        for c in range(N_SLAB):
            hs_scr[c, pl.ds(t, batch, stride=PITCH), :] = h[:, c * LANES:(c + 1) * LANES]
        return h
    hst_scr[...] = lax.fori_loop(0, TS, scan_t, hst_scr[...], unroll=True)

    def lru_out_b(b, c):
        r0 = pl.multiple_of(b * TS, TS)
        p0 = pl.multiple_of(b * PITCH, 8)
        hb = jnp.concatenate([hs_scr[c2, pl.ds(p0, TS), :] for c2 in range(N_SLAB)], axis=1)
        v = hb * _silu_from_half(gl_scr[pl.ds(r0, TS), :])
        y = scale_by(v * _rms_scale(v), V_LG)
        ycat_scr[pl.ds(r0, TS), ATTN_WIDTH:] = y.astype(bf16)
        return c
    lax.fori_loop(0, batch, lru_out_b, 0, unroll=True)

    norm_into_h(xn_ref)

    ga_scr[...] = (
        jnp.dot(ycat_scr[:, :ATTN_WIDTH], woa_ref[...], preferred_element_type=f32)
        + jnp.dot(ycat_scr[:, ATTN_WIDTH:], wol_ref[...], preferred_element_type=f32))

    def out_b(b, c):
        x2 = x_ref[b] + ga_scr[pl.ds(pl.multiple_of(b * TS, TS), TS), :]
        o_ref[b] = scale_by(x2 * _rms_scale(x2), V_FG)
        return c
    lax.fori_loop(0, batch, out_b, 0, unroll=True)


def _rope_tables(seq):
    half = ROT_DIM // 2
    pos = jnp.arange(seq, dtype=jnp.float32)
    inv_freq = ROPE_THETA ** (-jnp.arange(0, ROT_DIM, 2, dtype=jnp.float32) / ROT_DIM)
    ang = pos[:, None] * jnp.tile(inv_freq, LANES // half)[None, :]
    cos, sin = jnp.cos(ang), jnp.sin(ang)
    d = np.arange(LANES) % HEAD_DIM
    lo = (d < half)[None, :]
    hi = ((d >= half) & (d < ROT_DIM))[None, :]
    return jnp.stack([jnp.where(lo | hi, cos, 1.0),
                      jnp.where(hi, sin, 0.0),
                      jnp.where(lo, -sin, 0.0)])


def _group_major(a, axis):
    shp = a.shape
    a = a.reshape(shp[:axis] + (N_KV_HEADS, GROUP, HEAD_DIM) + shp[axis + 1:])
    return jnp.swapaxes(a, axis, axis + 1).reshape(shp)


def kernel(x, ln_gain, w_in, sinks, conv_w, conv_b, w_rgate, b_rgate, w_igate, b_igate,
           lru_lambda, attn_out_gain, lru_out_gain, w_out, final_gain):
    batch, seq, d_model = x.shape
    assert d_model == D_MODEL and seq % TS == 0 and ln_gain.shape[0] == 1 and N_SLOT == 3
    assert batch == SUBLANES
    f32, bf16 = jnp.float32, jnp.bfloat16
    rows = batch * TS

    col_scale = np.ones((1, OFF_XL + 2 * LRU_WIDTH), np.float32)
    col_scale[:, OFF_GA:OFF_XL] = 0.5
    col_scale[:, OFF_XL + LRU_WIDTH:] = 0.5
    w_all = (w_in[0] * col_scale).astype(bf16)
    wo_all = w_out[0].astype(bf16)
    old_col = np.arange(ATTN_WIDTH).reshape(N_KV_HEADS, GROUP, HEAD_DIM).transpose(1, 0, 2).reshape(-1)
    perm = np.zeros((ATTN_WIDTH, ATTN_WIDTH), np.float32)
    perm[old_col, np.arange(ATTN_WIDTH)] = 1.0
    perm = jnp.asarray(perm, dtype=bf16)
    wq = jnp.dot(w_all[:, :OFF_KV], perm, preferred_element_type=f32).astype(bf16)
    wga = jnp.dot(w_all[:, OFF_GA:OFF_XL], perm, preferred_element_type=f32).astype(bf16)
    woa = jnp.dot(perm.T, wo_all[:ATTN_WIDTH], preferred_element_type=f32).astype(bf16)
    wkv, wlru, wol = w_all[:, OFF_KV:OFF_GA], w_all[:, OFF_XL:], wo_all[ATTN_WIDTH:]

    def block_diag(wb):
        per = MXU_DIM // LRU_BLOCK_W
        wb = wb.reshape(N_GRP, per, LRU_BLOCK_W, 1, LRU_BLOCK_W)
        eye = np.eye(per, dtype=np.float32).reshape(1, per, 1, per, 1)
        return (wb * (0.5 * eye)).reshape(N_GRP, MXU_DIM, MXU_DIM)
    wg = jnp.concatenate([block_diag(w_rgate[0]), block_diag(w_igate[0])], axis=2).astype(bf16)

    vec_rows = [None] * (V_CW + CONV_W)
    vec_rows[V_LN], vec_rows[V_CB], vec_rows[V_BR] = ln_gain, conv_b, 0.5 * b_rgate
    vec_rows[V_BI], vec_rows[V_LAM], vec_rows[V_LG] = 0.5 * b_igate, lru_lambda, lru_out_gain
    vec_rows[V_AG], vec_rows[V_FG] = _group_major(attn_out_gain, 1), final_gain[None]
    vec_rows[V_CW:] = [conv_w[0][tap:tap + 1] for tap in range(CONV_W)]
    vecs = jnp.concatenate(vec_rows + [jnp.zeros((V_ROWS - len(vec_rows), D_MODEL), f32)], axis=0)

    rope_tbl = _rope_tables(seq)

    const = lambda *_: (0, 0)
    single = dict(pipeline_mode=pl.Buffered(1))
    grid_spec = pl.GridSpec(
        grid=(seq // TS,),
        in_specs=[
            pl.BlockSpec(memory_space=pltpu.SMEM),
            pl.BlockSpec((batch, TS, D_MODEL), lambda s: (0, s, 0)),
            pl.BlockSpec((batch, TS, D_MODEL), lambda s: (0, jnp.minimum(s + 1, seq // TS - 1), 0)),
            pl.BlockSpec((3, TS, LANES), lambda s: (0, s, 0)),
            pl.BlockSpec((V_ROWS, D_MODEL), const, **single),
            pl.BlockSpec((D_MODEL, ATTN_WIDTH), const, **single),
            pl.BlockSpec((D_MODEL, 2 * KV_WIDTH), const, **single),
            pl.BlockSpec((D_MODEL, ATTN_WIDTH), const, **single),
            pl.BlockSpec((D_MODEL, 2 * LRU_WIDTH), const, **single),
            pl.BlockSpec((N_GRP, MXU_DIM, 2 * MXU_DIM), lambda s: (0, 0, 0), **single),
            pl.BlockSpec((ATTN_WIDTH, D_MODEL), const, **single),
            pl.BlockSpec((LRU_WIDTH, D_MODEL), const, **single),
        ],
        out_specs=pl.BlockSpec((batch, TS, D_MODEL), lambda s: (0, s, 0)),
        scratch_shapes=[
            pltpu.VMEM((rows, D_MODEL), bf16),
            pltpu.VMEM((rows, ATTN_WIDTH), bf16),
            pltpu.VMEM((batch, N_KV_HEADS, KEYS, LANES), bf16),
            pltpu.VMEM((batch, N_SLOT, KV_WIDTH, TS), bf16),
            pltpu.VMEM((rows, ATTN_WIDTH), f32),
            pltpu.VMEM((N_SLAB, batch * PITCH, LANES), f32),
            pltpu.VMEM((rows, LRU_WIDTH), f32),
            pltpu.VMEM((N_GRP, rows, MXU_DIM), f32),
            pltpu.VMEM((N_GRP, rows, MXU_DIM), f32),
            pltpu.VMEM((N_GRP, rows, MXU_DIM), f32),
            pltpu.VMEM((N_SLAB, batch * PITCH, LANES), f32),
            pltpu.VMEM((rows, 2 * ATTN_WIDTH), bf16),
            pltpu.VMEM((CONV_W - 1, batch, LRU_WIDTH), f32),
            pltpu.VMEM((batch, LRU_WIDTH), f32),
            pltpu.VMEM((V_ROWS, SUBLANES, D_MODEL), f32),
        ],
    )
    return pl.pallas_call(
        functools.partial(_layer_kernel, batch=batch),
        grid_spec=grid_spec,
        out_shape=jax.ShapeDtypeStruct(x.shape, x.dtype),
        compiler_params=pltpu.CompilerParams(
            dimension_semantics=("arbitrary",), vmem_limit_bytes=VMEM_LIMIT_BYTES),
        name="hymba_layer",
    )(sinks[0], x, x, rope_tbl, vecs, wq, wkv, wga, wlru, wg, woa, wol)
```

```python
import functools

import numpy as np
import jax
import jax.numpy as jnp
from jax import lax
from jax.experimental import pallas as pl
from jax.experimental.pallas import tpu as pltpu

D_MODEL = 1024
HEAD_DIM = 64
N_Q_HEADS = 16
N_KV_HEADS = 4
GROUP = N_Q_HEADS // N_KV_HEADS
ATTN_WIDTH = N_Q_HEADS * HEAD_DIM
KV_WIDTH = N_KV_HEADS * HEAD_DIM
WINDOW = 128
ROT_DIM = HEAD_DIM // 4
ROPE_THETA = 500000.0
NEG_INF = -1e30
LRU_WIDTH = 1024
LRU_BLOCKS = 16
LRU_BLOCK_W = LRU_WIDTH // LRU_BLOCKS
CONV_W = 4
LRU_C = 8.0
EPS = 1e-6

LANES = 128
SUBLANES = 8
MXU_DIM = 256
VMEM_LIMIT_BYTES = 58 * 1024 * 1024

TS = 64
N_SLOT = WINDOW // TS + 1
KEYS = N_SLOT * TS
LOG2E = 1.4426950408889634
Q_SCALE = HEAD_DIM ** -0.5 * LOG2E
PITCH = TS + 8
N_SLAB = LRU_WIDTH // LANES
N_GRP = LRU_WIDTH // MXU_DIM

OFF_KV, OFF_GA, OFF_XL = ATTN_WIDTH, ATTN_WIDTH + 2 * KV_WIDTH, 2 * ATTN_WIDTH + 2 * KV_WIDTH

V_LN, V_CB, V_BR, V_BI, V_LAM, V_AG, V_LG, V_FG, V_CW = 0, 1, 2, 3, 4, 5, 6, 7, 8
V_HS = V_CW + CONV_W
V_ROWS = 16


def _silu_from_half(hx):
    return hx * (jnp.tanh(hx) + 1.0)


def _rms_scale(v):
    return lax.rsqrt(jnp.mean(v * v, axis=-1, keepdims=True) + EPS)


def _layer_kernel(sinks_ref, x_ref, rope_ref, vecs_ref, wq_ref, wkv_ref, wga_ref, wlru_ref,
                  wg_ref, woa_ref, wol_ref, o_ref,
                  h_scr, q_scr, km_scr, vt_scr, ga_scr, xl_scr, gl_scr, xc_scr, a_scr, u_scr,
                  hs_scr, ycat_scr, cw_scr, hst_scr, vb_scr, *, batch):
    f32, bf16 = jnp.float32, jnp.bfloat16
    rows = batch * TS
    step = pl.program_id(0)

    @pl.when(step == 0)
    def _init_carries():
        km_scr[...] = jnp.zeros_like(km_scr)
        vt_scr[...] = jnp.zeros_like(vt_scr)
        cw_scr[...] = jnp.zeros_like(cw_scr)
        hst_scr[...] = jnp.zeros_like(hst_scr)
        for row in range(V_ROWS):
            vb_scr[row] = jnp.broadcast_to(vecs_ref[row:row + 1, :], (SUBLANES, D_MODEL))
        lam = vb_scr[V_LAM]
        softplus_neg_lam = jnp.maximum(-lam, 0.0) + jnp.log1p(jnp.exp(-jnp.abs(lam)))
        vb_scr[V_HS] = (-0.5 * LRU_C) * softplus_neg_lam

    def by_tile(v, op, tile):
        return op(v.reshape(-1, SUBLANES, v.shape[-1]), tile[None]).reshape(v.shape)

    def scale_by(v, row):
        return by_tile(v, jnp.multiply, vb_scr[row])

    def norm_b(b, c):
        xb = x_ref[b]
        hb = scale_by(xb * _rms_scale(xb), V_LN)
        h_scr[pl.ds(pl.multiple_of(b * TS, TS), TS), :] = hb.astype(bf16)
        return c
    lax.fori_loop(0, batch, norm_b, 0, unroll=True)

    hmat = h_scr[...]

    def proj(w_ref, c0, c1):
        return jnp.dot(hmat, w_ref[:, c0:c1], preferred_element_type=f32)

    k_tab = (rope_ref[0], rope_ref[1], rope_ref[2])
    q_tab = tuple(t * Q_SCALE for t in k_tab)

    def rope(z, tab):
        cos_t, sin_hi, sin_lo = tab
        return (z * cos_t[None] + pltpu.roll(z, 8, 2) * sin_hi[None]
                + pltpu.roll(z, LANES - 8, 2) * sin_lo[None])

    for g in range(GROUP):
        zq = proj(wq_ref, g * MXU_DIM, (g + 1) * MXU_DIM).reshape(batch, TS, MXU_DIM)
        for sl in range(MXU_DIM // LANES):
            r = rope(zq[:, :, sl * LANES:(sl + 1) * LANES], q_tab)
            c0 = g * MXU_DIM + sl * LANES
            q_scr[:, c0:c0 + LANES] = r.reshape(rows, LANES).astype(bf16)

    slot = lax.rem(step, N_SLOT)
    slot_row = pl.multiple_of(slot * TS, TS)
    zkv = proj(wkv_ref, 0, 2 * KV_WIDTH).reshape(batch, TS, 2 * KV_WIDTH)
    lane_half = lax.broadcasted_iota(jnp.int32, (1, 1, LANES), 2) // HEAD_DIM
    for sl in range(KV_WIDTH // LANES):
        r = rope(zkv[:, :, sl * LANES:(sl + 1) * LANES], k_tab).astype(bf16)
        for hh in range(LANES // HEAD_DIM):
            km_scr[:, sl * (LANES // HEAD_DIM) + hh, pl.ds(slot_row, TS), :] = jnp.where(
                lane_half == hh, r, jnp.zeros_like(r))
    for b in range(batch):
        vt_scr[b, slot] = zkv[b, :, KV_WIDTH:].T.astype(bf16)

    half = ATTN_WIDTH // 2
    for c in range(2):
        ga_scr[:, c * half:(c + 1) * half] = proj(wga_ref, c * half, (c + 1) * half)
        gl_scr[:, c * half:(c + 1) * half] = proj(wlru_ref, LRU_WIDTH + c * half, LRU_WIDTH + (c + 1) * half)
        zx = proj(wlru_ref, c * half, (c + 1) * half)
        for sl in range(half // LANES):
            for b in range(batch):
                xl_scr[c * (half // LANES) + sl, b * PITCH:b * PITCH + TS, :] = (
                    zx[b * TS:(b + 1) * TS, sl * LANES:(sl + 1) * LANES])

    jj = lax.broadcasted_iota(jnp.int32, (TS, GROUP * TS), 0)
    ii = lax.broadcasted_iota(jnp.int32, (TS, GROUP * TS), 1) % TS
    blocks = []
    for sg in range(N_SLOT):
        age = lax.rem(step + (N_SLOT - sg), N_SLOT)
        valid = ((age == 0) & (jj <= ii)) | (age == 1) | ((age == 2) & (jj > ii))
        blocks.append(jnp.where(valid & (step >= age), 0.0, NEG_INF).astype(f32))
    bias_t = jnp.concatenate(blocks, axis=0)
    lane_g = lax.broadcasted_iota(jnp.int32, (1, GROUP * TS), 1) // TS
    sink_rows = []
    for k in range(N_KV_HEADS):
        row = jnp.zeros((1, GROUP * TS), f32)
        for g in range(GROUP):
            row = jnp.where(lane_g == g, sinks_ref[k * GROUP + g] * LOG2E, row)
        sink_rows.append(row)

    def attn_b(b, c):
        r0 = pl.multiple_of(b * TS, TS)
        q_pair = [jnp.concatenate(
            [q_scr[pl.ds(r0, TS), g * MXU_DIM + sl * LANES:g * MXU_DIM + (sl + 1) * LANES]
             for g in range(GROUP)], axis=0) for sl in range(KV_WIDTH // LANES)]
        o_heads = []
        for k in range(N_KV_HEADS):
            st = lax.dot_general(km_scr[b, k], q_pair[k // 2], (((1,), (1,)), ((), ())),
                                 preferred_element_type=f32)
            st = st + bias_t
            m = jnp.maximum(jnp.max(st, axis=0, keepdims=True), sink_rows[k])
            p = jnp.exp2(st - m)
            den = jnp.sum(p, axis=0, keepdims=True) + jnp.exp2(sink_rows[k] - m)
            pb = p.astype(bf16)
            o = jnp.zeros((HEAD_DIM, GROUP * TS), f32)
            for sg in range(N_SLOT):
                o = o + jnp.dot(vt_scr[b, sg, k * HEAD_DIM:(k + 1) * HEAD_DIM, :],
                                pb[sg * TS:(sg + 1) * TS], preferred_element_type=f32)
            o_heads.append(o * (1.0 / den))
        o_rows = jnp.concatenate(o_heads, axis=0).T
        ob = jnp.concatenate([o_rows[g * TS:(g + 1) * TS] for g in range(GROUP)], axis=1)
        v = ob * _silu_from_half(ga_scr[pl.ds(r0, TS), :])
        y = scale_by(v * _rms_scale(v), V_AG)
        ycat_scr[pl.ds(r0, TS), 0:ATTN_WIDTH] = y.astype(bf16)
        return c
    lax.fori_loop(0, batch, attn_b, 0, unroll=8)

    def conv_t(t, carry):
        x1, x2, x3 = carry
        xt = jnp.concatenate(
            [xl_scr[c, pl.ds(t, batch, stride=PITCH), :] for c in range(N_SLAB)], axis=1)
        u = (vb_scr[V_CB] + vb_scr[V_CW + 3] * xt + vb_scr[V_CW + 2] * x1
             + vb_scr[V_CW + 1] * x2 + vb_scr[V_CW] * x3)
        t8 = pl.multiple_of(t * batch, batch)
        for j in range(N_GRP):
            xc_scr[j, pl.ds(t8, batch), :] = u[:, j * MXU_DIM:(j + 1) * MXU_DIM]
        return xt, x1, x2
    tail = lax.fori_loop(0, TS, conv_t, (cw_scr[0], cw_scr[1], cw_scr[2]), unroll=True)
    for i in range(CONV_W - 1):
        cw_scr[i] = tail[i]

    half_scale = vb_scr[V_HS]
    for j in range(N_GRP):
        cs = slice(j * MXU_DIM, (j + 1) * MXU_DIM)
        xg = xc_scr[j]
        pre = jnp.dot(xg.astype(bf16), wg_ref[j], preferred_element_type=f32)
        t_r = jnp.tanh(by_tile(pre[:, :MXU_DIM], jnp.add, vb_scr[V_BR][:, cs]))
        t_i = jnp.tanh(by_tile(pre[:, MXU_DIM:], jnp.add, vb_scr[V_BI][:, cs]))
        log_a = by_tile(by_tile(t_r, jnp.multiply, half_scale[:, cs]), jnp.add, half_scale[:, cs])
        a_scr[j] = jnp.exp(log_a)
        th = jnp.tanh(log_a)
        a_half = -0.5 * th
        gate = jnp.where(a_half > 0.0, a_half * lax.rsqrt(a_half * (1.0 - th)), 0.0)
        u_scr[j] = gate * ((t_i + 1.0) * xg)

    def scan_t(t, h):
        t8 = pl.multiple_of(t * batch, batch)
        at = jnp.concatenate([a_scr[j, pl.ds(t8, batch), :] for j in range(N_GRP)], axis=1)
        ut = jnp.concatenate([u_scr[j, pl.ds(t8, batch), :] for j in range(N_GRP)], axis=1)
        h = at * h + ut
        for c in range(N_SLAB):
            hs_scr[c, pl.ds(t8, batch), :] = h[:, c * LANES:(c + 1) * LANES]
        return h
    hst_scr[...] = lax.fori_loop(0, TS, scan_t, hst_scr[...], unroll=True)

    def lru_out_b(b, c):
        r0 = pl.multiple_of(b * TS, TS)
        hb = jnp.concatenate(
            [hs_scr[c2, pl.ds(b, TS, stride=batch), :] for c2 in range(N_SLAB)], axis=1)
        v = hb * _silu_from_half(gl_scr[pl.ds(r0, TS), :])
        y = scale_by(v * _rms_scale(v), V_LG)
        ycat_scr[pl.ds(r0, TS), ATTN_WIDTH:] = y.astype(bf16)
        return c
    lax.fori_loop(0, batch, lru_out_b, 0, unroll=True)

    ga_scr[...] = (
        jnp.dot(ycat_scr[:, :ATTN_WIDTH], woa_ref[...], preferred_element_type=f32)
        + jnp.dot(ycat_scr[:, ATTN_WIDTH:], wol_ref[...], preferred_element_type=f32))

    def out_b(b, c):
        x2 = x_ref[b] + ga_scr[pl.ds(pl.multiple_of(b * TS, TS), TS), :]
        o_ref[b] = scale_by(x2 * _rms_scale(x2), V_FG)
        return c
    lax.fori_loop(0, batch, out_b, 0, unroll=True)


def _rope_tables(seq):
    half = ROT_DIM // 2
    pos = jnp.arange(seq, dtype=jnp.float32)
    inv_freq = ROPE_THETA ** (-jnp.arange(0, ROT_DIM, 2, dtype=jnp.float32) / ROT_DIM)
    ang = pos[:, None] * jnp.tile(inv_freq, LANES // half)[None, :]
    cos, sin = jnp.cos(ang), jnp.sin(ang)
    d = np.arange(LANES) % HEAD_DIM
    lo = (d < half)[None, :]
    hi = ((d >= half) & (d < ROT_DIM))[None, :]
    return jnp.stack([jnp.where(lo | hi, cos, 1.0),
                      jnp.where(hi, sin, 0.0),
                      jnp.where(lo, -sin, 0.0)])


def _group_major(a, axis):
    shp = a.shape
    a = a.reshape(shp[:axis] + (N_KV_HEADS, GROUP, HEAD_DIM) + shp[axis + 1:])
    return jnp.swapaxes(a, axis, axis + 1).reshape(shp)


def kernel(x, ln_gain, w_in, sinks, conv_w, conv_b, w_rgate, b_rgate, w_igate, b_igate,
           lru_lambda, attn_out_gain, lru_out_gain, w_out, final_gain):
    batch, seq, d_model = x.shape
    assert d_model == D_MODEL and seq % TS == 0 and ln_gain.shape[0] == 1 and N_SLOT == 3
    assert batch == SUBLANES
    f32, bf16 = jnp.float32, jnp.bfloat16
    rows = batch * TS

    col_scale = np.ones((1, OFF_XL + 2 * LRU_WIDTH), np.float32)
    col_scale[:, OFF_GA:OFF_XL] = 0.5
    col_scale[:, OFF_XL + LRU_WIDTH:] = 0.5
    w_all = (w_in[0] * col_scale).astype(bf16)
    wo_all = w_out[0].astype(bf16)
    old_col = np.arange(ATTN_WIDTH).reshape(N_KV_HEADS, GROUP, HEAD_DIM).transpose(1, 0, 2).reshape(-1)
    perm = np.zeros((ATTN_WIDTH, ATTN_WIDTH), np.float32)
    perm[old_col, np.arange(ATTN_WIDTH)] = 1.0
    perm = jnp.asarray(perm, dtype=bf16)
    wq = jnp.dot(w_all[:, :OFF_KV], perm, preferred_element_type=f32).astype(bf16)
    wga = jnp.dot(w_all[:, OFF_GA:OFF_XL], perm, preferred_element_type=f32).astype(bf16)
    woa = jnp.dot(perm.T, wo_all[:ATTN_WIDTH], preferred_element_type=f32).astype(bf16)
    wkv, wlru, wol = w_all[:, OFF_KV:OFF_GA], w_all[:, OFF_XL:], wo_all[ATTN_WIDTH:]

    def block_diag(wb):
        per = MXU_DIM // LRU_BLOCK_W
        wb = wb.reshape(N_GRP, per, LRU_BLOCK_W, 1, LRU_BLOCK_W)
        eye = np.eye(per, dtype=np.float32).reshape(1, per, 1, per, 1)
        return (wb * (0.5 * eye)).reshape(N_GRP, MXU_DIM, MXU_DIM)
    wg = jnp.concatenate([block_diag(w_rgate[0]), block_diag(w_igate[0])], axis=2).astype(bf16)

    vec_rows = [None] * (V_CW + CONV_W)
    vec_rows[V_LN], vec_rows[V_CB], vec_rows[V_BR] = ln_gain, conv_b, 0.5 * b_rgate
    vec_rows[V_BI], vec_rows[V_LAM], vec_rows[V_LG] = 0.5 * b_igate, lru_lambda, lru_out_gain
    vec_rows[V_AG], vec_rows[V_FG] = _group_major(attn_out_gain, 1), final_gain[None]
    vec_rows[V_CW:] = [conv_w[0][tap:tap + 1] for tap in range(CONV_W)]
    vecs = jnp.concatenate(vec_rows + [jnp.zeros((V_ROWS - len(vec_rows), D_MODEL), f32)], axis=0)

    rope_tbl = _rope_tables(seq)

    const = lambda *_: (0, 0)
    single = dict(pipeline_mode=pl.Buffered(1))
    grid_spec = pl.GridSpec(
        grid=(seq // TS,),
        in_specs=[
            pl.BlockSpec(memory_space=pltpu.SMEM),
            pl.BlockSpec((batch, TS, D_MODEL), lambda s: (0, s, 0)),
            pl.BlockSpec((3, TS, LANES), lambda s: (0, s, 0)),
            pl.BlockSpec((V_ROWS, D_MODEL), const, **single),
            pl.BlockSpec((D_MODEL, ATTN_WIDTH), const, **single),
            pl.BlockSpec((D_MODEL, 2 * KV_WIDTH), const, **single),
            pl.BlockSpec((D_MODEL, ATTN_WIDTH), const, **single),
            pl.BlockSpec((D_MODEL, 2 * LRU_WIDTH), const, **single),
            pl.BlockSpec((N_GRP, MXU_DIM, 2 * MXU_DIM), lambda s: (0, 0, 0), **single),
            pl.BlockSpec((ATTN_WIDTH, D_MODEL), const, **single),
            pl.BlockSpec((LRU_WIDTH, D_MODEL), const, **single),
        ],
        out_specs=pl.BlockSpec((batch, TS, D_MODEL), lambda s: (0, s, 0)),
        scratch_shapes=[
            pltpu.VMEM((rows, D_MODEL), bf16),
            pltpu.VMEM((rows, ATTN_WIDTH), bf16),
            pltpu.VMEM((batch, N_KV_HEADS, KEYS, LANES), bf16),
            pltpu.VMEM((batch, N_SLOT, KV_WIDTH, TS), bf16),
            pltpu.VMEM((rows, ATTN_WIDTH), f32),
            pltpu.VMEM((N_SLAB, batch * PITCH, LANES), f32),
            pltpu.VMEM((rows, LRU_WIDTH), f32),
            pltpu.VMEM((N_GRP, rows, MXU_DIM), f32),
            pltpu.VMEM((N_GRP, rows, MXU_DIM), f32),
            pltpu.VMEM((N_GRP, rows, MXU_DIM), f32),
            pltpu.VMEM((N_SLAB, rows, LANES), f32),
            pltpu.VMEM((rows, 2 * ATTN_WIDTH), bf16),
            pltpu.VMEM((CONV_W - 1, batch, LRU_WIDTH), f32),
            pltpu.VMEM((batch, LRU_WIDTH), f32),
            pltpu.VMEM((V_ROWS, SUBLANES, D_MODEL), f32),
        ],
    )
    return pl.pallas_call(
        functools.partial(_layer_kernel, batch=batch),
        grid_spec=grid_spec,
        out_shape=jax.ShapeDtypeStruct(x.shape, x.dtype),
        compiler_params=pltpu.CompilerParams(
            dimension_semantics=("arbitrary",), vmem_limit_bytes=VMEM_LIMIT_BYTES),
        name="hymba_layer",
    )(sinks[0], x, rope_tbl, vecs, wq, wkv, wga, wlru, wg, woa, wol)
```

```python
import functools

import numpy as np
import jax
import jax.numpy as jnp
from jax import lax
from jax.experimental import pallas as pl
from jax.experimental.pallas import tpu as pltpu

D_MODEL = 1024
HEAD_DIM = 64
N_Q_HEADS = 16
N_KV_HEADS = 4
GROUP = N_Q_HEADS // N_KV_HEADS
ATTN_WIDTH = N_Q_HEADS * HEAD_DIM
KV_WIDTH = N_KV_HEADS * HEAD_DIM
WINDOW = 128
ROT_DIM = HEAD_DIM // 4
ROPE_THETA = 500000.0
NEG_INF = -1e30
LRU_WIDTH = 1024
LRU_BLOCKS = 16
LRU_BLOCK_W = LRU_WIDTH // LRU_BLOCKS
CONV_W = 4
LRU_C = 8.0
EPS = 1e-6

LANES = 128
SUBLANES = 8
MXU_DIM = 256
VMEM_LIMIT_BYTES = 58 * 1024 * 1024

TS = 64
N_SLOT = WINDOW // TS + 1
KEYS = N_SLOT * TS
LOG2E = 1.4426950408889634
Q_SCALE = HEAD_DIM ** -0.5 * LOG2E
PITCH = TS + 8
N_SLAB = LRU_WIDTH // LANES
N_GRP = LRU_WIDTH // MXU_DIM

OFF_KV, OFF_GA, OFF_XL = ATTN_WIDTH, ATTN_WIDTH + 2 * KV_WIDTH, 2 * ATTN_WIDTH + 2 * KV_WIDTH

V_LN, V_CB, V_BR, V_BI, V_LAM, V_AG, V_LG, V_FG, V_CW = 0, 1, 2, 3, 4, 5, 6, 7, 8
V_HS = V_CW + CONV_W
V_ROWS = 16


def _silu_from_half(hx):
    return hx * (jnp.tanh(hx) + 1.0)


def _rms_scale(v):
    return lax.rsqrt(jnp.mean(v * v, axis=-1, keepdims=True) + EPS)


def _layer_kernel(sinks_ref, x_ref, rope_ref, vecs_ref, wq_ref, wkv_ref, wga_ref, wlru_ref,
                  wg_ref, woa_ref, wol_ref, o_ref,
                  h_scr, q_scr, km_scr, vt_scr, ga_scr, xl_scr, gl_scr, xc_scr, a_scr, u_scr,
                  hs_scr, ycat_scr, cw_scr, hst_scr, vb_scr, *, batch):
    f32, bf16 = jnp.float32, jnp.bfloat16
    rows = batch * TS
    step = pl.program_id(0)

    @pl.when(step == 0)
    def _init_carries():
        km_scr[...] = jnp.zeros_like(km_scr)
        vt_scr[...] = jnp.zeros_like(vt_scr)
        cw_scr[...] = jnp.zeros_like(cw_scr)
        hst_scr[...] = jnp.zeros_like(hst_scr)
        for row in range(V_ROWS):
            vb_scr[row] = jnp.broadcast_to(vecs_ref[row:row + 1, :], (SUBLANES, D_MODEL))
        lam = vb_scr[V_LAM]
        softplus_neg_lam = jnp.maximum(-lam, 0.0) + jnp.log1p(jnp.exp(-jnp.abs(lam)))
        vb_scr[V_HS] = (-0.5 * LRU_C) * softplus_neg_lam

    def by_tile(v, op, tile):
        return op(v.reshape(-1, SUBLANES, v.shape[-1]), tile[None]).reshape(v.shape)

    def scale_by(v, row):
        return by_tile(v, jnp.multiply, vb_scr[row])

    def norm_b(b, c):
        xb = x_ref[b]
        hb = scale_by(xb * _rms_scale(xb), V_LN)
        h_scr[pl.ds(pl.multiple_of(b * TS, TS), TS), :] = hb.astype(bf16)
        return c
    lax.fori_loop(0, batch, norm_b, 0, unroll=True)

    def proj(w_ref, c0, c1):
        return jnp.dot(h_scr[...], w_ref[:, c0:c1], preferred_element_type=f32)

    k_tab = (rope_ref[0], rope_ref[1], rope_ref[2])
    q_tab = tuple(t * Q_SCALE for t in k_tab)

    def rope(z, tab):
        cos_t, sin_hi, sin_lo = tab
        return (z * cos_t[None] + pltpu.roll(z, 8, 2) * sin_hi[None]
                + pltpu.roll(z, LANES - 8, 2) * sin_lo[None])

    for g in range(GROUP):
        zq = proj(wq_ref, g * MXU_DIM, (g + 1) * MXU_DIM).reshape(batch, TS, MXU_DIM)
        for sl in range(MXU_DIM // LANES):
            r = rope(zq[:, :, sl * LANES:(sl + 1) * LANES], q_tab)
            c0 = g * MXU_DIM + sl * LANES
            q_scr[:, c0:c0 + LANES] = r.reshape(rows, LANES).astype(bf16)

    slot = lax.rem(step, N_SLOT)
    slot_row = pl.multiple_of(slot * TS, TS)
    zkv = proj(wkv_ref, 0, 2 * KV_WIDTH).reshape(batch, TS, 2 * KV_WIDTH)
    lane_half = lax.broadcasted_iota(jnp.int32, (1, 1, LANES), 2) // HEAD_DIM
    for sl in range(KV_WIDTH // LANES):
        r = rope(zkv[:, :, sl * LANES:(sl + 1) * LANES], k_tab).astype(bf16)
        for hh in range(LANES // HEAD_DIM):
            km_scr[:, sl * (LANES // HEAD_DIM) + hh, pl.ds(slot_row, TS), :] = jnp.where(
                lane_half == hh, r, jnp.zeros_like(r))
    for b in range(batch):
        vt_scr[b, slot] = zkv[b, :, KV_WIDTH:].T.astype(bf16)

    half = ATTN_WIDTH // 2
    for c in range(2):
        ga_scr[:, c * half:(c + 1) * half] = proj(wga_ref, c * half, (c + 1) * half)
        gl_scr[:, c * half:(c + 1) * half] = proj(wlru_ref, LRU_WIDTH + c * half, LRU_WIDTH + (c + 1) * half)
        zx = proj(wlru_ref, c * half, (c + 1) * half)
        for sl in range(half // LANES):
            for b in range(batch):
                xl_scr[c * (half // LANES) + sl, b * PITCH:b * PITCH + TS, :] = (
                    zx[b * TS:(b + 1) * TS, sl * LANES:(sl + 1) * LANES])

    jj = lax.broadcasted_iota(jnp.int32, (TS, GROUP * TS), 0)
    ii = lax.broadcasted_iota(jnp.int32, (TS, GROUP * TS), 1) % TS
    blocks = []
    for sg in range(N_SLOT):
        age = lax.rem(step + (N_SLOT - sg), N_SLOT)
        valid = ((age == 0) & (jj <= ii)) | (age == 1) | ((age == 2) & (jj > ii))
        blocks.append(jnp.where(valid & (step >= age), 0.0, NEG_INF).astype(f32))
    bias_t = jnp.concatenate(blocks, axis=0)
    lane_g = lax.broadcasted_iota(jnp.int32, (1, GROUP * TS), 1) // TS
    sink_rows = []
    for k in range(N_KV_HEADS):
        row = jnp.zeros((1, GROUP * TS), f32)
        for g in range(GROUP):
            row = jnp.where(lane_g == g, sinks_ref[k * GROUP + g] * LOG2E, row)
        sink_rows.append(row)

    def attn_b(b, c):
        r0 = pl.multiple_of(b * TS, TS)
        q_pair = [jnp.concatenate(
            [q_scr[pl.ds(r0, TS), g * MXU_DIM + sl * LANES:g * MXU_DIM + (sl + 1) * LANES]
             for g in range(GROUP)], axis=0) for sl in range(KV_WIDTH // LANES)]
        o_heads = []
        for k in range(N_KV_HEADS):
            st = lax.dot_general(km_scr[b, k], q_pair[k // 2], (((1,), (1,)), ((), ())),
                                 preferred_element_type=f32)
            st = st + bias_t
            m = jnp.maximum(jnp.max(st, axis=0, keepdims=True), sink_rows[k])
            p = jnp.exp2(st - m)
            den = jnp.sum(p, axis=0, keepdims=True) + jnp.exp2(sink_rows[k] - m)
            pb = p.astype(bf16)
            o = jnp.zeros((HEAD_DIM, GROUP * TS), f32)
            for sg in range(N_SLOT):
                o = o + jnp.dot(vt_scr[b, sg, k * HEAD_DIM:(k + 1) * HEAD_DIM, :],
                                pb[sg * TS:(sg + 1) * TS], preferred_element_type=f32)
            o_heads.append(o * (1.0 / den))
        o_rows = jnp.concatenate(o_heads, axis=0).T
        ob = jnp.concatenate([o_rows[g * TS:(g + 1) * TS] for g in range(GROUP)], axis=1)
        v = ob * _silu_from_half(ga_scr[pl.ds(r0, TS), :])
        y = scale_by(v * _rms_scale(v), V_AG)
        ycat_scr[pl.ds(r0, TS), 0:ATTN_WIDTH] = y.astype(bf16)
        return c
    lax.fori_loop(0, batch, attn_b, 0, unroll=8)

    def conv_t(t, carry):
        x1, x2, x3 = carry
        xt = jnp.concatenate(
            [xl_scr[c, pl.ds(t, batch, stride=PITCH), :] for c in range(N_SLAB)], axis=1)
        u = (vb_scr[V_CB] + vb_scr[V_CW + 3] * xt + vb_scr[V_CW + 2] * x1
             + vb_scr[V_CW + 1] * x2 + vb_scr[V_CW] * x3)
        t8 = pl.multiple_of(t * batch, batch)
        for j in range(N_GRP):
            xc_scr[j, pl.ds(t8, batch), :] = u[:, j * MXU_DIM:(j + 1) * MXU_DIM]
        return xt, x1, x2
    tail = lax.fori_loop(0, TS, conv_t, (cw_scr[0], cw_scr[1], cw_scr[2]), unroll=True)
    for i in range(CONV_W - 1):
        cw_scr[i] = tail[i]

    half_scale = vb_scr[V_HS]
    for j in range(N_GRP):
        cs = slice(j * MXU_DIM, (j + 1) * MXU_DIM)
        xg = xc_scr[j]
        pre = jnp.dot(xg.astype(bf16), wg_ref[j], preferred_element_type=f32)
        t_r = jnp.tanh(by_tile(pre[:, :MXU_DIM], jnp.add, vb_scr[V_BR][:, cs]))
        t_i = jnp.tanh(by_tile(pre[:, MXU_DIM:], jnp.add, vb_scr[V_BI][:, cs]))
        log_a = by_tile(by_tile(t_r, jnp.multiply, half_scale[:, cs]), jnp.add, half_scale[:, cs])
        a_scr[j] = jnp.exp(log_a)
        th = jnp.tanh(log_a)
        a_half = -0.5 * th
        gate = jnp.where(a_half > 0.0, a_half * lax.rsqrt(a_half * (1.0 - th)), 0.0)
        u_scr[j] = gate * ((t_i + 1.0) * xg)

    def scan_t(t, h):
        t8 = pl.multiple_of(t * batch, batch)
        at = jnp.concatenate([a_scr[j, pl.ds(t8, batch), :] for j in range(N_GRP)], axis=1)
        ut = jnp.concatenate([u_scr[j, pl.ds(t8, batch), :] for j in range(N_GRP)], axis=1)
        h = at * h + ut
        for c in range(N_SLAB):
            hs_scr[c, pl.ds(t, batch, stride=PITCH), :] = h[:, c * LANES:(c + 1) * LANES]
        return h
    hst_scr[...] = lax.fori_loop(0, TS, scan_t, hst_scr[...], unroll=True)

    def lru_out_b(b, c):
        r0 = pl.multiple_of(b * TS, TS)
        p0 = pl.multiple_of(b * PITCH, 8)
        hb = jnp.concatenate([hs_scr[c2, pl.ds(p0, TS), :] for c2 in range(N_SLAB)], axis=1)
        v = hb * _silu_from_half(gl_scr[pl.ds(r0, TS), :])
        y = scale_by(v * _rms_scale(v), V_LG)
        ycat_scr[pl.ds(r0, TS), ATTN_WIDTH:] = y.astype(bf16)
        return c
    lax.fori_loop(0, batch, lru_out_b, 0, unroll=True)

    ga_scr[...] = (
        jnp.dot(ycat_scr[:, :ATTN_WIDTH], woa_ref[...], preferred_element_type=f32)
        + jnp.dot(ycat_scr[:, ATTN_WIDTH:], wol_ref[...], preferred_element_type=f32))

    def out_b(b, c):
        x2 = x_ref[b] + ga_scr[pl.ds(pl.multiple_of(b * TS, TS), TS), :]
        o_ref[b] = scale_by(x2 * _rms_scale(x2), V_FG)
        return c
    lax.fori_loop(0, batch, out_b, 0, unroll=True)


def _rope_tables(seq):
    half = ROT_DIM // 2
    pos = jnp.arange(seq, dtype=jnp.float32)
    inv_freq = ROPE_THETA ** (-jnp.arange(0, ROT_DIM, 2, dtype=jnp.float32) / ROT_DIM)
    ang = pos[:, None] * jnp.tile(inv_freq, LANES // half)[None, :]
    cos, sin = jnp.cos(ang), jnp.sin(ang)
    d = np.arange(LANES) % HEAD_DIM
    lo = (d < half)[None, :]
    hi = ((d >= half) & (d < ROT_DIM))[None, :]
    return jnp.stack([jnp.where(lo | hi, cos, 1.0),
                      jnp.where(hi, sin, 0.0),
                      jnp.where(lo, -sin, 0.0)])


def _group_major(a, axis):
    shp = a.shape
    a = a.reshape(shp[:axis] + (N_KV_HEADS, GROUP, HEAD_DIM) + shp[axis + 1:])
    return jnp.swapaxes(a, axis, axis + 1).reshape(shp)


def kernel(x, ln_gain, w_in, sinks, conv_w, conv_b, w_rgate, b_rgate, w_igate, b_igate,
           lru_lambda, attn_out_gain, lru_out_gain, w_out, final_gain):
    batch, seq, d_model = x.shape
    assert d_model == D_MODEL and seq % TS == 0 and ln_gain.shape[0] == 1 and N_SLOT == 3
    assert batch == SUBLANES
    f32, bf16 = jnp.float32, jnp.bfloat16
    rows = batch * TS

    col_scale = np.ones((1, OFF_XL + 2 * LRU_WIDTH), np.float32)
    col_scale[:, OFF_GA:OFF_XL] = 0.5
    col_scale[:, OFF_XL + LRU_WIDTH:] = 0.5
    w_all = (w_in[0] * col_scale).astype(bf16)
    wo_all = w_out[0].astype(bf16)
    old_col = np.arange(ATTN_WIDTH).reshape(N_KV_HEADS, GROUP, HEAD_DIM).transpose(1, 0, 2).reshape(-1)
    perm = np.zeros((ATTN_WIDTH, ATTN_WIDTH), np.float32)
    perm[old_col, np.arange(ATTN_WIDTH)] = 1.0
    perm = jnp.asarray(perm, dtype=bf16)
    wq = jnp.dot(w_all[:, :OFF_KV], perm, preferred_element_type=f32).astype(bf16)
    wga = jnp.dot(w_all[:, OFF_GA:OFF_XL], perm, preferred_element_type=f32).astype(bf16)
    woa = jnp.dot(perm.T, wo_all[:ATTN_WIDTH], preferred_element_type=f32).astype(bf16)
    wkv, wlru, wol = w_all[:, OFF_KV:OFF_GA], w_all[:, OFF_XL:], wo_all[ATTN_WIDTH:]

    def block_diag(wb):
        per = MXU_DIM // LRU_BLOCK_W
        wb = wb.reshape(N_GRP, per, LRU_BLOCK_W, 1, LRU_BLOCK_W)
        eye = np.eye(per, dtype=np.float32).reshape(1, per, 1, per, 1)
        return (wb * (0.5 * eye)).reshape(N_GRP, MXU_DIM, MXU_DIM)
    wg = jnp.concatenate([block_diag(w_rgate[0]), block_diag(w_igate[0])], axis=2).astype(bf16)

    vec_rows = [None] * (V_CW + CONV_W)
    vec_rows[V_LN], vec_rows[V_CB], vec_rows[V_BR] = ln_gain, conv_b, 0.5 * b_rgate
    vec_rows[V_BI], vec_rows[V_LAM], vec_rows[V_LG] = 0.5 * b_igate, lru_lambda, lru_out_gain
    vec_rows[V_AG], vec_rows[V_FG] = _group_major(attn_out_gain, 1), final_gain[None]
    vec_rows[V_CW:] = [conv_w[0][tap:tap + 1] for tap in range(CONV_W)]
    vecs = jnp.concatenate(vec_rows + [jnp.zeros((V_ROWS - len(vec_rows), D_MODEL), f32)], axis=0)

    rope_tbl = _rope_tables(seq)

    const = lambda *_: (0, 0)
    single = dict(pipeline_mode=pl.Buffered(1))
    grid_spec = pl.GridSpec(
        grid=(seq // TS,),
        in_specs=[
            pl.BlockSpec(memory_space=pltpu.SMEM),
            pl.BlockSpec((batch, TS, D_MODEL), lambda s: (0, s, 0)),
            pl.BlockSpec((3, TS, LANES), lambda s: (0, s, 0)),
            pl.BlockSpec((V_ROWS, D_MODEL), const, **single),
            pl.BlockSpec((D_MODEL, ATTN_WIDTH), const, **single),
            pl.BlockSpec((D_MODEL, 2 * KV_WIDTH), const, **single),
            pl.BlockSpec((D_MODEL, ATTN_WIDTH), const, **single),
            pl.BlockSpec((D_MODEL, 2 * LRU_WIDTH), const, **single),
            pl.BlockSpec((N_GRP, MXU_DIM, 2 * MXU_DIM), lambda s: (0, 0, 0), **single),
            pl.BlockSpec((ATTN_WIDTH, D_MODEL), const, **single),
            pl.BlockSpec((LRU_WIDTH, D_MODEL), const, **single),
        ],
        out_specs=pl.BlockSpec((batch, TS, D_MODEL), lambda s: (0, s, 0)),
        scratch_shapes=[
            pltpu.VMEM((rows, D_MODEL), bf16),
            pltpu.VMEM((rows, ATTN_WIDTH), bf16),
            pltpu.VMEM((batch, N_KV_HEADS, KEYS, LANES), bf16),
            pltpu.VMEM((batch, N_SLOT, KV_WIDTH, TS), bf16),
            pltpu.VMEM((rows, ATTN_WIDTH), f32),
            pltpu.VMEM((N_SLAB, batch * PITCH, LANES), f32),
            pltpu.VMEM((rows, LRU_WIDTH), f32),
            pltpu.VMEM((N_GRP, rows, MXU_DIM), f32),
            pltpu.VMEM((N_GRP, rows, MXU_DIM), f32),
            pltpu.VMEM((N_GRP, rows, MXU_DIM), f32),
            pltpu.VMEM((N_SLAB, batch * PITCH, LANES), f32),
            pltpu.VMEM((rows, 2 * ATTN_WIDTH), bf16),
            pltpu.VMEM((CONV_W - 1, batch, LRU_WIDTH), f32),
            pltpu.VMEM((batch, LRU_WIDTH), f32),
            pltpu.VMEM((V_ROWS, SUBLANES, D_MODEL), f32),
        ],
    )
    return pl.pallas_call(
        functools.partial(_layer_kernel, batch=batch),
        grid_spec=grid_spec,
        out_shape=jax.ShapeDtypeStruct(x.shape, x.dtype),
        compiler_params=pltpu.CompilerParams(
            dimension_semantics=("arbitrary",), vmem_limit_bytes=VMEM_LIMIT_BYTES),
        name="hymba_layer",
    )(sinks[0], x, rope_tbl, vecs, wq, wkv, wga, wlru, wg, woa, wol)
```

```python
import functools

import numpy as np
import jax
import jax.numpy as jnp
from jax import lax
from jax.experimental import pallas as pl
from jax.experimental.pallas import tpu as pltpu

D_MODEL = 1024
HEAD_DIM = 64
N_Q_HEADS = 16
N_KV_HEADS = 4
GROUP = N_Q_HEADS // N_KV_HEADS
ATTN_WIDTH = N_Q_HEADS * HEAD_DIM
KV_WIDTH = N_KV_HEADS * HEAD_DIM
WINDOW = 128
ROT_DIM = HEAD_DIM // 4
ROPE_THETA = 500000.0
NEG_INF = -1e30
LRU_WIDTH = 1024
LRU_BLOCKS = 16
LRU_BLOCK_W = LRU_WIDTH // LRU_BLOCKS
CONV_W = 4
LRU_C = 8.0
EPS = 1e-6

LANES = 128
SUBLANES = 8
MXU_DIM = 256
VMEM_LIMIT_BYTES = 58 * 1024 * 1024

TS = 64
N_SLOT = WINDOW // TS + 1
KEYS = N_SLOT * TS
LOG2E = 1.4426950408889634
Q_SCALE = HEAD_DIM ** -0.5 * LOG2E
PITCH = TS + 8
N_SLAB = LRU_WIDTH // LANES
N_GRP = LRU_WIDTH // MXU_DIM

OFF_KV, OFF_GA, OFF_XL = ATTN_WIDTH, ATTN_WIDTH + 2 * KV_WIDTH, 2 * ATTN_WIDTH + 2 * KV_WIDTH

V_LN, V_CB, V_BR, V_BI, V_LAM, V_AG, V_LG, V_FG, V_CW = 0, 1, 2, 3, 4, 5, 6, 7, 8
V_HS = V_CW + CONV_W
V_ROWS = 16


def _silu_from_half(hx):
    return hx * (jnp.tanh(hx) + 1.0)


def _rms_scale(v):
    return lax.rsqrt(jnp.mean(v * v, axis=-1, keepdims=True) + EPS)


def _layer_kernel(sinks_ref, x_ref, rope_ref, vecs_ref, wq_ref, wkv_ref, wga_ref, wlru_ref,
                  wg_ref, woa_ref, wol_ref, o_ref,
                  h_scr, q_scr, km_scr, vt_scr, ga_scr, xl_scr, gl_scr, xc_scr, a_scr, u_scr,
                  hs_scr, ycat_scr, cw_scr, hst_scr, vb_scr, *, batch):
    f32, bf16 = jnp.float32, jnp.bfloat16
    rows = batch * TS
    step = pl.program_id(0)

    @pl.when(step == 0)
    def _init_carries():
        km_scr[...] = jnp.zeros_like(km_scr)
        vt_scr[...] = jnp.zeros_like(vt_scr)
        cw_scr[...] = jnp.zeros_like(cw_scr)
        hst_scr[...] = jnp.zeros_like(hst_scr)
        for row in range(V_ROWS):
            vb_scr[row] = jnp.broadcast_to(vecs_ref[row:row + 1, :], (SUBLANES, D_MODEL))
        lam = vb_scr[V_LAM]
        softplus_neg_lam = jnp.maximum(-lam, 0.0) + jnp.log1p(jnp.exp(-jnp.abs(lam)))
        vb_scr[V_HS] = (-0.5 * LRU_C) * softplus_neg_lam

    def by_tile(v, op, tile):
        return op(v.reshape(-1, SUBLANES, v.shape[-1]), tile[None]).reshape(v.shape)

    def scale_by(v, row):
        return by_tile(v, jnp.multiply, vb_scr[row])

    def norm_b(b, c):
        xb = x_ref[b]
        hb = scale_by(xb * _rms_scale(xb), V_LN)
        h_scr[pl.ds(pl.multiple_of(b * TS, TS), TS), :] = hb.astype(bf16)
        return c
    lax.fori_loop(0, batch, norm_b, 0, unroll=True)

    hmat = h_scr[...]

    def proj(w_ref, c0, c1):
        return jnp.dot(hmat, w_ref[:, c0:c1], preferred_element_type=f32)

    k_tab = (rope_ref[0], rope_ref[1], rope_ref[2])
    q_tab = tuple(t * Q_SCALE for t in k_tab)

    def rope(z, tab):
        cos_t, sin_hi, sin_lo = tab
        return (z * cos_t[None] + pltpu.roll(z, 8, 2) * sin_hi[None]
                + pltpu.roll(z, LANES - 8, 2) * sin_lo[None])

    for g in range(GROUP):
        zq = proj(wq_ref, g * MXU_DIM, (g + 1) * MXU_DIM).reshape(batch, TS, MXU_DIM)
        for sl in range(MXU_DIM // LANES):
            r = rope(zq[:, :, sl * LANES:(sl + 1) * LANES], q_tab)
            c0 = g * MXU_DIM + sl * LANES
            q_scr[:, c0:c0 + LANES] = r.reshape(rows, LANES).astype(bf16)

    slot = lax.rem(step, N_SLOT)
    slot_row = pl.multiple_of(slot * TS, TS)
    zkv = proj(wkv_ref, 0, 2 * KV_WIDTH).reshape(batch, TS, 2 * KV_WIDTH)
    lane_half = lax.broadcasted_iota(jnp.int32, (1, 1, LANES), 2) // HEAD_DIM
    for sl in range(KV_WIDTH // LANES):
        r = rope(zkv[:, :, sl * LANES:(sl + 1) * LANES], k_tab).astype(bf16)
        for hh in range(LANES // HEAD_DIM):
            km_scr[:, sl * (LANES // HEAD_DIM) + hh, pl.ds(slot_row, TS), :] = jnp.where(
                lane_half == hh, r, jnp.zeros_like(r))
    for b in range(batch):
        vt_scr[b, slot] = zkv[b, :, KV_WIDTH:].T.astype(bf16)

    half = ATTN_WIDTH // 2
    for c in range(2):
        ga_scr[:, c * half:(c + 1) * half] = proj(wga_ref, c * half, (c + 1) * half)
        gl_scr[:, c * half:(c + 1) * half] = proj(wlru_ref, LRU_WIDTH + c * half, LRU_WIDTH + (c + 1) * half)
        zx = proj(wlru_ref, c * half, (c + 1) * half)
        for sl in range(half // LANES):
            for b in range(batch):
                xl_scr[c * (half // LANES) + sl, b * PITCH:b * PITCH + TS, :] = (
                    zx[b * TS:(b + 1) * TS, sl * LANES:(sl + 1) * LANES])

    jj = lax.broadcasted_iota(jnp.int32, (TS, GROUP * TS), 0)
    ii = lax.broadcasted_iota(jnp.int32, (TS, GROUP * TS), 1) % TS
    blocks = []
    for sg in range(N_SLOT):
        age = lax.rem(step + (N_SLOT - sg), N_SLOT)
        valid = ((age == 0) & (jj <= ii)) | (age == 1) | ((age == 2) & (jj > ii))
        blocks.append(jnp.where(valid & (step >= age), 0.0, NEG_INF).astype(f32))
    bias_t = jnp.concatenate(blocks, axis=0)
    lane_g = lax.broadcasted_iota(jnp.int32, (1, GROUP * TS), 1) // TS
    sink_rows = []
    for k in range(N_KV_HEADS):
        row = jnp.zeros((1, GROUP * TS), f32)
        for g in range(GROUP):
            row = jnp.where(lane_g == g, sinks_ref[k * GROUP + g] * LOG2E, row)
        sink_rows.append(row)

    def attn_b(b, c):
        r0 = pl.multiple_of(b * TS, TS)
        q_pair = [jnp.concatenate(
            [q_scr[pl.ds(r0, TS), g * MXU_DIM + sl * LANES:g * MXU_DIM + (sl + 1) * LANES]
             for g in range(GROUP)], axis=0) for sl in range(KV_WIDTH // LANES)]
        o_heads = []
        for k in range(N_KV_HEADS):
            st = lax.dot_general(km_scr[b, k], q_pair[k // 2], (((1,), (1,)), ((), ())),
                                 preferred_element_type=f32)
            st = st + bias_t
            m = jnp.maximum(jnp.max(st, axis=0, keepdims=True), sink_rows[k])
            p = jnp.exp2(st - m)
            den = jnp.sum(p, axis=0, keepdims=True) + jnp.exp2(sink_rows[k] - m)
            pb = p.astype(bf16)
            o = jnp.zeros((HEAD_DIM, GROUP * TS), f32)
            for sg in range(N_SLOT):
                o = o + jnp.dot(vt_scr[b, sg, k * HEAD_DIM:(k + 1) * HEAD_DIM, :],
                                pb[sg * TS:(sg + 1) * TS], preferred_element_type=f32)
            o_heads.append(o * (1.0 / den))
        o_rows = jnp.concatenate(o_heads, axis=0).T
        ob = jnp.concatenate([o_rows[g * TS:(g + 1) * TS] for g in range(GROUP)], axis=1)
        v = ob * _silu_from_half(ga_scr[pl.ds(r0, TS), :])
        y = scale_by(v * _rms_scale(v), V_AG)
        ycat_scr[pl.ds(r0, TS), 0:ATTN_WIDTH] = y.astype(bf16)
        return c
    lax.fori_loop(0, batch, attn_b, 0, unroll=8)

    def conv_t(t, carry):
        x1, x2, x3 = carry
        xt = jnp.concatenate(
            [xl_scr[c, pl.ds(t, batch, stride=PITCH), :] for c in range(N_SLAB)], axis=1)
        u = (vb_scr[V_CB] + vb_scr[V_CW + 3] * xt + vb_scr[V_CW + 2] * x1
             + vb_scr[V_CW + 1] * x2 + vb_scr[V_CW] * x3)
        t8 = pl.multiple_of(t * batch, batch)
        for j in range(N_GRP):
            xc_scr[j, pl.ds(t8, batch), :] = u[:, j * MXU_DIM:(j + 1) * MXU_DIM]
        return xt, x1, x2
    tail = lax.fori_loop(0, TS, conv_t, (cw_scr[0], cw_scr[1], cw_scr[2]), unroll=True)
    for i in range(CONV_W - 1):
        cw_scr[i] = tail[i]

    half_scale = vb_scr[V_HS]
    for j in range(N_GRP):
        cs = slice(j * MXU_DIM, (j + 1) * MXU_DIM)
        xg = xc_scr[j]
        pre = jnp.dot(xg.astype(bf16), wg_ref[j], preferred_element_type=f32)
        for r0 in range(0, rows, rows // 2):
            rs = slice(r0, r0 + rows // 2)
            t_r = jnp.tanh(by_tile(pre[rs, :MXU_DIM], jnp.add, vb_scr[V_BR][:, cs]))
            t_i = jnp.tanh(by_tile(pre[rs, MXU_DIM:], jnp.add, vb_scr[V_BI][:, cs]))
            log_a = by_tile(by_tile(t_r, jnp.multiply, half_scale[:, cs]), jnp.add, half_scale[:, cs])
            a_scr[j, rs, :] = jnp.exp(log_a)
            th = jnp.tanh(log_a)
            a_half = -0.5 * th
            gate = jnp.where(a_half > 0.0, a_half * lax.rsqrt(a_half * (1.0 - th)), 0.0)
            u_scr[j, rs, :] = gate * ((t_i + 1.0) * xg[rs])

    def scan_t(t, h):
        t8 = pl.multiple_of(t * batch, batch)
        at = jnp.concatenate([a_scr[j, pl.ds(t8, batch), :] for j in range(N_GRP)], axis=1)
        ut = jnp.concatenate([u_scr[j, pl.ds(t8, batch), :] for j in range(N_GRP)], axis=1)
        h = at * h + ut
        for c in range(N_SLAB):
            hs_scr[c, pl.ds(t, batch, stride=PITCH), :] = h[:, c * LANES:(c + 1) * LANES]
        return h
    hst_scr[...] = lax.fori_loop(0, TS, scan_t, hst_scr[...], unroll=True)

    def lru_out_b(b, c):
        r0 = pl.multiple_of(b * TS, TS)
        p0 = pl.multiple_of(b * PITCH, 8)
        hb = jnp.concatenate([hs_scr[c2, pl.ds(p0, TS), :] for c2 in range(N_SLAB)], axis=1)
        v = hb * _silu_from_half(gl_scr[pl.ds(r0, TS), :])
        y = scale_by(v * _rms_scale(v), V_LG)
        ycat_scr[pl.ds(r0, TS), ATTN_WIDTH:] = y.astype(bf16)
        return c
    lax.fori_loop(0, batch, lru_out_b, 0, unroll=True)

    ga_scr[...] = (
        jnp.dot(ycat_scr[:, :ATTN_WIDTH], woa_ref[...], preferred_element_type=f32)
        + jnp.dot(ycat_scr[:, ATTN_WIDTH:], wol_ref[...], preferred_element_type=f32))

    def out_b(b, c):
        x2 = x_ref[b] + ga_scr[pl.ds(pl.multiple_of(b * TS, TS), TS), :]
        o_ref[b] = scale_by(x2 * _rms_scale(x2), V_FG)
        return c
    lax.fori_loop(0, batch, out_b, 0, unroll=True)


def _rope_tables(seq):
    half = ROT_DIM // 2
    pos = jnp.arange(seq, dtype=jnp.float32)
    inv_freq = ROPE_THETA ** (-jnp.arange(0, ROT_DIM, 2, dtype=jnp.float32) / ROT_DIM)
    ang = pos[:, None] * jnp.tile(inv_freq, LANES // half)[None, :]
    cos, sin = jnp.cos(ang), jnp.sin(ang)
    d = np.arange(LANES) % HEAD_DIM
    lo = (d < half)[None, :]
    hi = ((d >= half) & (d < ROT_DIM))[None, :]
    return jnp.stack([jnp.where(lo | hi, cos, 1.0),
                      jnp.where(hi, sin, 0.0),
                      jnp.where(lo, -sin, 0.0)])


def _group_major(a, axis):
    shp = a.shape
    a = a.reshape(shp[:axis] + (N_KV_HEADS, GROUP, HEAD_DIM) + shp[axis + 1:])
    return jnp.swapaxes(a, axis, axis + 1).reshape(shp)


def kernel(x, ln_gain, w_in, sinks, conv_w, conv_b, w_rgate, b_rgate, w_igate, b_igate,
           lru_lambda, attn_out_gain, lru_out_gain, w_out, final_gain):
    batch, seq, d_model = x.shape
    assert d_model == D_MODEL and seq % TS == 0 and ln_gain.shape[0] == 1 and N_SLOT == 3
    assert batch == SUBLANES
    f32, bf16 = jnp.float32, jnp.bfloat16
    rows = batch * TS

    col_scale = np.ones((1, OFF_XL + 2 * LRU_WIDTH), np.float32)
    col_scale[:, OFF_GA:OFF_XL] = 0.5
    col_scale[:, OFF_XL + LRU_WIDTH:] = 0.5
    w_all = (w_in[0] * col_scale).astype(bf16)
    wo_all = w_out[0].astype(bf16)
    old_col = np.arange(ATTN_WIDTH).reshape(N_KV_HEADS, GROUP, HEAD_DIM).transpose(1, 0, 2).reshape(-1)
    perm = np.zeros((ATTN_WIDTH, ATTN_WIDTH), np.float32)
    perm[old_col, np.arange(ATTN_WIDTH)] = 1.0
    perm = jnp.asarray(perm, dtype=bf16)
    wq = jnp.dot(w_all[:, :OFF_KV], perm, preferred_element_type=f32).astype(bf16)
    wga = jnp.dot(w_all[:, OFF_GA:OFF_XL], perm, preferred_element_type=f32).astype(bf16)
    woa = jnp.dot(perm.T, wo_all[:ATTN_WIDTH], preferred_element_type=f32).astype(bf16)
    wkv, wlru, wol = w_all[:, OFF_KV:OFF_GA], w_all[:, OFF_XL:], wo_all[ATTN_WIDTH:]

    def block_diag(wb):
        per = MXU_DIM // LRU_BLOCK_W
        wb = wb.reshape(N_GRP, per, LRU_BLOCK_W, 1, LRU_BLOCK_W)
        eye = np.eye(per, dtype=np.float32).reshape(1, per, 1, per, 1)
        return (wb * (0.5 * eye)).reshape(N_GRP, MXU_DIM, MXU_DIM)
    wg = jnp.concatenate([block_diag(w_rgate[0]), block_diag(w_igate[0])], axis=2).astype(bf16)

    vec_rows = [None] * (V_CW + CONV_W)
    vec_rows[V_LN], vec_rows[V_CB], vec_rows[V_BR] = ln_gain, conv_b, 0.5 * b_rgate
    vec_rows[V_BI], vec_rows[V_LAM], vec_rows[V_LG] = 0.5 * b_igate, lru_lambda, lru_out_gain
    vec_rows[V_AG], vec_rows[V_FG] = _group_major(attn_out_gain, 1), final_gain[None]
    vec_rows[V_CW:] = [conv_w[0][tap:tap + 1] for tap in range(CONV_W)]
    vecs = jnp.concatenate(vec_rows + [jnp.zeros((V_ROWS - len(vec_rows), D_MODEL), f32)], axis=0)

    rope_tbl = _rope_tables(seq)

    const = lambda *_: (0, 0)
    single = dict(pipeline_mode=pl.Buffered(1))
    grid_spec = pl.GridSpec(
        grid=(seq // TS,),
        in_specs=[
            pl.BlockSpec(memory_space=pltpu.SMEM),
            pl.BlockSpec((batch, TS, D_MODEL), lambda s: (0, s, 0)),
            pl.BlockSpec((3, TS, LANES), lambda s: (0, s, 0)),
            pl.BlockSpec((V_ROWS, D_MODEL), const, **single),
            pl.BlockSpec((D_MODEL, ATTN_WIDTH), const, **single),
            pl.BlockSpec((D_MODEL, 2 * KV_WIDTH), const, **single),
            pl.BlockSpec((D_MODEL, ATTN_WIDTH), const, **single),
            pl.BlockSpec((D_MODEL, 2 * LRU_WIDTH), const, **single),
            pl.BlockSpec((N_GRP, MXU_DIM, 2 * MXU_DIM), lambda s: (0, 0, 0), **single),
            pl.BlockSpec((ATTN_WIDTH, D_MODEL), const, **single),
            pl.BlockSpec((LRU_WIDTH, D_MODEL), const, **single),
        ],
        out_specs=pl.BlockSpec((batch, TS, D_MODEL), lambda s: (0, s, 0)),
        scratch_shapes=[
            pltpu.VMEM((rows, D_MODEL), bf16),
            pltpu.VMEM((rows, ATTN_WIDTH), bf16),
            pltpu.VMEM((batch, N_KV_HEADS, KEYS, LANES), bf16),
            pltpu.VMEM((batch, N_SLOT, KV_WIDTH, TS), bf16),
            pltpu.VMEM((rows, ATTN_WIDTH), f32),
            pltpu.VMEM((N_SLAB, batch * PITCH, LANES), f32),
            pltpu.VMEM((rows, LRU_WIDTH), f32),
            pltpu.VMEM((N_GRP, rows, MXU_DIM), f32),
            pltpu.VMEM((N_GRP, rows, MXU_DIM), f32),
            pltpu.VMEM((N_GRP, rows, MXU_DIM), f32),
            pltpu.VMEM((N_SLAB, batch * PITCH, LANES), f32),
            pltpu.VMEM((rows, 2 * ATTN_WIDTH), bf16),
            pltpu.VMEM((CONV_W - 1, batch, LRU_WIDTH), f32),
            pltpu.VMEM((batch, LRU_WIDTH), f32),
            pltpu.VMEM((V_ROWS, SUBLANES, D_MODEL), f32),
        ],
    )
    return pl.pallas_call(
        functools.partial(_layer_kernel, batch=batch),
        grid_spec=grid_spec,
        out_shape=jax.ShapeDtypeStruct(x.shape, x.dtype),
        compiler_params=pltpu.CompilerParams(
            dimension_semantics=("arbitrary",), vmem_limit_bytes=VMEM_LIMIT_BYTES),
        name="hymba_layer",
    )(sinks[0], x, rope_tbl, vecs, wq, wkv, wga, wlru, wg, woa, wol)
```
